```python
import jax, jax.numpy as jnp
from jax import lax
import numpy as np

D_MODEL = 1024
BATCH = 32
SEQ = 2048
DEPTH = 1
DEC_BATCH = 8
DEC_SEQ = 4096
PAST_LEN = 128

D_MIX = D_MODEL
A_WIDTH = D_MIX // 2
B_WIDTH = D_MIX - A_WIDTH
CHUNK = 128
A_HEADS = 4
A_HEAD_DIM = A_WIDTH // A_HEADS
POOL_WINDOWS = (2, 4, 8, 16)
B_GROUPS = len(POOL_WINDOWS)
B_GROUP_DIM = B_WIDTH // B_GROUPS
N_GROUPS = 4
EXPERTS_PER_GROUP = 8
N_EXPERTS = N_GROUPS * EXPERTS_PER_GROUP
TOP_K = 2
D_EXPERT = D_MODEL // 2
ROW_BLOCK = 128
N_MOD = 6
LN_EPS = 1e-5
ALPHA = (2.0 * DEPTH) ** 0.25
BETA = (8.0 * DEPTH) ** -0.25

kernel_name = "hybrid_sgu_pool_hmoe_encoder"


def layer_norm(x, g, b):
    xf = x.astype(jnp.float32)
    mu = jnp.mean(xf, axis=-1, keepdims=True)
    xc = xf - mu
    var = jnp.mean(xc * xc, axis=-1, keepdims=True)
    return (xc * lax.rsqrt(var + LN_EPS) * g + b).astype(x.dtype)


def multiscale_pool(p):
    bsz, s, _ = p.shape
    pg = p.reshape(bsz, s, B_GROUPS, B_GROUP_DIM).astype(jnp.float32)
    cs = jnp.concatenate([jnp.zeros((bsz, 1, B_GROUPS, B_GROUP_DIM), jnp.float32),
                          jnp.cumsum(pg, axis=1)], axis=1)
    t = jnp.arange(s)
    outs = []
    for g, w in enumerate(POOL_WINDOWS):
        lo = jnp.clip(t - w // 2, 0, s)
        hi = jnp.clip(t + w // 2, 0, s)
        win_sum = cs[:, hi, g] - cs[:, lo, g]
        cnt = (hi - lo).astype(jnp.float32)[None, :, None]
        outs.append(win_sum / cnt - pg[:, :, g])
    return jnp.stack(outs, axis=2).astype(p.dtype)


def token_mixer(h, w_in, v_norm_g, v_norm_b, w_spatial, b_spatial, w_pool, pool_scale, w_out):
    bsz, s, _ = h.shape
    z = h @ w_in
    za = jax.nn.gelu(z[..., :2 * A_WIDTH])
    u = za[..., :A_WIDTH]
    v = layer_norm(za[..., A_WIDTH:], v_norm_g, v_norm_b)
    v = v.reshape(bsz, s // CHUNK, CHUNK, A_HEADS, A_HEAD_DIM)
    v = jnp.einsum('hqp,bnphc->bnqhc', w_spatial, v) + b_spatial.T[:, :, None]
    a_out = u * v.reshape(bsz, s, A_WIDTH)
    pooled = multiscale_pool(z[..., 2 * A_WIDTH:])
    b_out = jnp.einsum('bsgc,gcd->bsgd', pooled, w_pool).reshape(bsz, s, B_WIDTH) * pool_scale
    return jnp.concatenate([a_out, b_out], axis=-1) @ w_out


def grouped_experts(h, e_id, e_w, w_gate_up, w_down):
    t_tok, d = h.shape
    tk = t_tok * TOP_K
    flat_e = e_id.reshape(-1)
    flat_w = e_w.reshape(-1)
    order = jnp.argsort(flat_e)
    sorted_e = flat_e[order]
    sorted_tok = (order // TOP_K).astype(jnp.int32)
    sorted_w = flat_w[order].astype(h.dtype)
    counts = jnp.bincount(flat_e, length=N_EXPERTS)
    padded = (counts + ROW_BLOCK - 1) // ROW_BLOCK * ROW_BLOCK
    seg_start = jnp.cumsum(counts) - counts
    pad_end = jnp.cumsum(padded)
    pad_start = pad_end - padded
    dest = pad_start[sorted_e] + jnp.arange(tk) - seg_start[sorted_e]
    n_blocks = -(-(tk + N_EXPERTS * (ROW_BLOCK - 1)) // ROW_BLOCK)
    rows = n_blocks * ROW_BLOCK
    buf_tok = jnp.full((rows,), t_tok, jnp.int32).at[dest].set(sorted_tok)
    buf_w = jnp.zeros((rows,), h.dtype).at[dest].set(sorted_w)
    block_e = jnp.minimum(jnp.searchsorted(pad_end, jnp.arange(n_blocks) * ROW_BLOCK, side='right'),
                          N_EXPERTS - 1)
    h_pad = jnp.concatenate([h, jnp.zeros((1, d), h.dtype)], axis=0)

    def block_fn(args):
        tok, w, e = args
        xb = h_pad[tok]
        gu = xb @ w_gate_up[e]
        y = (jax.nn.silu(gu[:, :D_EXPERT]) * gu[:, D_EXPERT:]) @ w_down[e]
        return y * w[:, None]

    yb = lax.map(block_fn, (buf_tok.reshape(n_blocks, ROW_BLOCK),
                            buf_w.reshape(n_blocks, ROW_BLOCK), block_e))
    out = jnp.zeros((t_tok + 1, d), h.dtype).at[buf_tok].add(yb.reshape(rows, d))
    return out[:t_tok]


def hierarchical_moe(h, w_route_group, b_route_group, w_route_expert, b_route_expert, w_gate_up, w_down):
    t_tok = h.shape[0]
    g_logits = (h @ w_route_group).astype(jnp.float32) + b_route_group
    g_prob = jax.nn.softmax(g_logits, axis=-1)
    g_idx = jnp.argmax(g_logits, axis=-1)
    g_w = jnp.take_along_axis(g_prob, g_idx[:, None], axis=-1)[:, 0]
    e_logits = (h @ w_route_expert).astype(jnp.float32).reshape(t_tok, N_GROUPS, EXPERTS_PER_GROUP) + b_route_expert
    e_logits = jnp.take_along_axis(e_logits, g_idx[:, None, None], axis=1)[:, 0]
    top_v, top_j = lax.top_k(e_logits, TOP_K)
    e_w = jax.nn.softmax(top_v, axis=-1) * g_w[:, None]
    e_id = g_idx[:, None].astype(jnp.int32) * EXPERTS_PER_GROUP + top_j.astype(jnp.int32)
    return grouped_experts(h, e_id, e_w, w_gate_up, w_down)


def encoder_layer(x, c, w_ada, b_ada, w_in, v_norm_g, v_norm_b, w_spatial, b_spatial, w_pool, pool_scale,
                  w_out, ln1_g, ln1_b, w_route_group, b_route_group, w_route_expert, b_route_expert,
                  w_gate_up, w_down, ln2_g, ln2_b):
    bsz, s, d = x.shape
    mod = (jax.nn.silu(c) @ w_ada + b_ada).reshape(bsz, 1, N_MOD, d)
    sh1, sc1, g1, sh2, sc2, g2 = (mod[:, :, i] for i in range(N_MOD))
    h = x * (1 + sc1) + sh1
    mix = token_mixer(h, w_in, v_norm_g, v_norm_b, w_spatial, b_spatial, w_pool, pool_scale, w_out)
    x = layer_norm(ALPHA * x + g1 * mix, ln1_g, ln1_b)
    h = x * (1 + sc2) + sh2
    y = hierarchical_moe(h.reshape(bsz * s, d), w_route_group, b_route_group, w_route_expert,
                         b_route_expert, w_gate_up, w_down).reshape(bsz, s, d)
    return layer_norm(ALPHA * x + g2 * y, ln2_g, ln2_b)


def trunk(x, c, w_ada, b_ada, w_in, v_norm_g, v_norm_b, w_spatial, b_spatial, w_pool, pool_scale,
          w_out, ln1_g, ln1_b, w_route_group, b_route_group, w_route_expert, b_route_expert,
          w_gate_up, w_down, ln2_g, ln2_b):
    for l in range(DEPTH):
        x = encoder_layer(x, c, w_ada[l], b_ada[l], w_in[l], v_norm_g[l], v_norm_b[l], w_spatial[l],
                          b_spatial[l], w_pool[l], pool_scale[l], w_out[l], ln1_g[l], ln1_b[l],
                          w_route_group[l], b_route_group[l], w_route_expert[l], b_route_expert[l],
                          w_gate_up[l], w_down[l], ln2_g[l], ln2_b[l])
    return x


def setup_inputs(seed: int = 0) -> dict:
    key = jax.random.key(seed)
    ks = jax.random.split(key, 26)
    L = DEPTH

    def nrm(k, shape, s):
        return jax.random.normal(k, shape, jnp.float32) * s

    return {
        "x_prompt": nrm(ks[0], (BATCH, SEQ, D_MODEL), 1.0),
        "x_sample": nrm(ks[1], (DEC_BATCH, DEC_SEQ, D_MODEL), 1.0),
        "c_prompt": nrm(ks[2], (BATCH, D_MODEL), 1.0),
        "c_sample": nrm(ks[3], (DEC_BATCH, D_MODEL), 1.0),
        "w_ada": nrm(ks[4], (L, D_MODEL, N_MOD * D_MODEL), D_MODEL ** -0.5),
        "b_ada": nrm(ks[5], (L, N_MOD * D_MODEL), 0.02),
        "w_in": nrm(ks[6], (L, D_MODEL, 2 * A_WIDTH + B_WIDTH), D_MODEL ** -0.5),
        "v_norm_g": 1.0 + nrm(ks[7], (L, A_WIDTH), 0.02),
        "v_norm_b": nrm(ks[8], (L, A_WIDTH), 0.02),
        "w_spatial": nrm(ks[9], (L, A_HEADS, CHUNK, CHUNK), CHUNK ** -0.5),
        "b_spatial": 1.0 + nrm(ks[10], (L, A_HEADS, CHUNK), 0.1),
        "w_pool": nrm(ks[11], (L, B_GROUPS, B_GROUP_DIM, B_GROUP_DIM), B_GROUP_DIM ** -0.5),
        "pool_scale": 1.0 + nrm(ks[12], (L, B_WIDTH), 0.1),
        "w_out": nrm(ks[13], (L, D_MIX, D_MODEL), BETA * D_MIX ** -0.5),
        "ln1_g": 1.0 + nrm(ks[14], (L, D_MODEL), 0.02),
        "ln1_b": nrm(ks[15], (L, D_MODEL), 0.02),
        "w_route_group": nrm(ks[16], (L, D_MODEL, N_GROUPS), D_MODEL ** -0.5),
        "b_route_group": nrm(ks[17], (L, N_GROUPS), 0.01),
        "w_route_expert": nrm(ks[18], (L, D_MODEL, N_EXPERTS), D_MODEL ** -0.5),
        "b_route_expert": nrm(ks[19], (L, N_GROUPS, EXPERTS_PER_GROUP), 0.01),
        "w_gate_up": nrm(ks[20], (L, N_EXPERTS, D_MODEL, 2 * D_EXPERT), D_MODEL ** -0.5),
        "w_down": nrm(ks[21], (L, N_EXPERTS, D_EXPERT, D_MODEL), BETA * D_EXPERT ** -0.5),
        "ln2_g": 1.0 + nrm(ks[22], (L, D_MODEL), 0.02),
        "ln2_b": nrm(ks[23], (L, D_MODEL), 0.02),
    }


def reference(x_prompt, x_sample, c_prompt, c_sample, w_ada, b_ada, w_in, v_norm_g, v_norm_b, w_spatial,
              b_spatial, w_pool, pool_scale, w_out, ln1_g, ln1_b, w_route_group, b_route_group,
              w_route_expert, b_route_expert, w_gate_up, w_down, ln2_g, ln2_b):
    y_prompt = trunk(x_prompt, c_prompt, w_ada, b_ada, w_in, v_norm_g, v_norm_b, w_spatial, b_spatial,
                     w_pool, pool_scale, w_out, ln1_g, ln1_b, w_route_group, b_route_group,
                     w_route_expert, b_route_expert, w_gate_up, w_down, ln2_g, ln2_b)
    y_sample = trunk(x_sample, c_sample, w_ada, b_ada, w_in, v_norm_g, v_norm_b, w_spatial, b_spatial,
                     w_pool, pool_scale, w_out, ln1_g, ln1_b, w_route_group, b_route_group,
                     w_route_expert, b_route_expert, w_gate_up, w_down, ln2_g, ln2_b)
    return (y_prompt, y_sample)
```

```python
import functools

import numpy as np
import jax
import jax.numpy as jnp
from jax import lax
from jax.experimental import pallas as pl
from jax.experimental.pallas import tpu as pltpu

F32 = jnp.float32
I32 = jnp.int32
MXU_DTYPE = jnp.bfloat16

D_MODEL = 1024
A_WIDTH = 512
B_WIDTH = 512
CHUNK = 128
A_HEADS = 4
HEAD_DIM = A_WIDTH // A_HEADS
POOL_WINDOWS = (2, 4, 8, 16)
POOL_HALO = 8
GROUP_DIM = B_WIDTH // len(POOL_WINDOWS)
N_GROUPS = 4
EPG = 8
N_EXPERTS = N_GROUPS * EPG
D_EXPERT = 512
N_MOD = 6
LN_EPS = 1e-5
ALPHA = 2.0 ** 0.25

PAIRS = EPG * (EPG - 1) // 2
N_CLASSES = N_GROUPS * PAIRS
CLS_PAD = 128
N_ROUTE = N_GROUPS + N_EXPERTS
ROUTE_PAD = 40
U32 = jnp.uint32
LANES = 128
SUBLANES = 8
PACK_ROWS = D_MODEL // (2 * LANES)
Y_ROWS = D_MODEL // LANES
assert PACK_ROWS < SUBLANES and Y_ROWS == SUBLANES
HI_MASK = 0xFFFF0000

TS = 512
TD = 512
RB = 256
BL = 2048
SUB = 256
VMEM_LIMIT = 56 * 1024 * 1024

_NT = (((1,), (1,)), ((), ()))


def _layer_norm(x, g, b):
    mu = jnp.mean(x, axis=-1, keepdims=True)
    xc = x - mu
    var = jnp.mean(xc * xc, axis=-1, keepdims=True)
    return xc * lax.rsqrt(var + LN_EPS) * g + b


def _mdot(a, b):
    return jnp.dot(a, b, preferred_element_type=F32)


def _mod_kernel(c_ref, w_ref, b_ref, o_ref):
    a = jax.nn.silu(c_ref[...])
    o_ref[...] = jnp.dot(a, w_ref[...], precision=lax.Precision.HIGHEST,
                         preferred_element_type=F32) + b_ref[...]


def _modulation(c_all, w_ada, b_ada):
    nb, d = c_all.shape
    n = w_ada.shape[1]
    bn = 1536
    return pl.pallas_call(
        _mod_kernel,
        out_shape=jax.ShapeDtypeStruct((nb, n), F32),
        grid=(n // bn,),
        in_specs=[pl.BlockSpec((nb, d), lambda j: (0, 0)),
                  pl.BlockSpec((d, bn), lambda j: (0, j)),
                  pl.BlockSpec((1, bn), lambda j: (0, j))],
        out_specs=pl.BlockSpec((nb, bn), lambda j: (0, j)),
        compiler_params=pltpu.CompilerParams(dimension_semantics=("arbitrary",),
                                             vmem_limit_bytes=VMEM_LIMIT),
        name="adaln_mod",
    )(c_all, w_ada, b_ada)


def _mixer_kernel(xp_ref, xpp_ref, xpn_ref, xs_ref, xsp_ref, xsn_ref, mod_ref, win_ref, vng_ref, vnb_ref, ws_ref,
                  bst_ref, band_ref, wpool_ref, pscale_ref, wout_ref, ln1g_ref, ln1b_ref, wr_ref, rb_ref,
                  x1_ref, rows_ref, rid_ref, *, prompt_tiles, prompt_seq, sample_seq):
    ts = xp_ref.shape[1]
    i = pl.program_id(0)
    is_p = i < prompt_tiles
    seq_len = jnp.where(is_p, prompt_seq, sample_seq)
    ns = jnp.where(is_p, prompt_seq // ts, sample_seq // ts)
    s = jnp.where(is_p, i, i - prompt_tiles) % ns
    md = mod_ref[0]
    sh1, sc1, g1, sh2, sc2 = md[0:1], md[1:2], md[2:3], md[3:4], md[4:5]

    xt = jnp.where(is_p, xp_ref[0], xs_ref[0])
    h = xt * (1.0 + sc1) + sh1
    hp = jnp.where(s > 0, jnp.where(is_p, xpp_ref[0], xsp_ref[0]) * (1.0 + sc1) + sh1, 0.0)
    hn = jnp.where(s < ns - 1, jnp.where(is_p, xpn_ref[0], xsn_ref[0]) * (1.0 + sc1) + sh1, 0.0)
    hext = jnp.concatenate([h, hp, hn], axis=0).astype(MXU_DTYPE)
    zext = _mdot(hext, win_ref[...])
    z = zext[:ts]

    za = jax.nn.gelu(z[:, :2 * A_WIDTH])
    u = za[:, :A_WIDTH]
    v = _layer_norm(za[:, A_WIDTH:], vng_ref[...], vnb_ref[...]).astype(MXU_DTYPE)
    bst = bst_ref[...]
    row_blocks = []
    for c in range(ts // CHUNK):
        cols = []
        for hh in range(A_HEADS):
            vv = v[c * CHUNK:(c + 1) * CHUNK, hh * HEAD_DIM:(hh + 1) * HEAD_DIM]
            cols.append(_mdot(ws_ref[hh], vv) + bst[:, hh:hh + 1])
        row_blocks.append(jnp.concatenate(cols, axis=1))
    a_out = u * jnp.concatenate(row_blocks, axis=0)

    p = z[:, 2 * A_WIDTH:]
    pall = jnp.concatenate([zext[ts:ts + POOL_HALO, 2 * A_WIDTH:], p,
                            zext[ts + POOL_HALO:ts + 2 * POOL_HALO, 2 * A_WIDTH:]], axis=0)
    p_hi = pall.astype(MXU_DTYPE)
    p_lo = (pall - p_hi.astype(F32)).astype(MXU_DTYPE)
    pos = s * ts + lax.broadcasted_iota(I32, (ts, 1), 0)
    b_cols = []
    for g, w in enumerate(POOL_WINDOWS):
        wins = []
        for c in range(ts // CHUNK):
            r0 = c * CHUNK
            seg_hi = p_hi[r0:r0 + CHUNK + 2 * POOL_HALO, g * GROUP_DIM:(g + 1) * GROUP_DIM]
            seg_lo = p_lo[r0:r0 + CHUNK + 2 * POOL_HALO, g * GROUP_DIM:(g + 1) * GROUP_DIM]
            wins.append(_mdot(band_ref[g], seg_hi) + _mdot(band_ref[g], seg_lo))
        win = jnp.concatenate(wins, axis=0)
        cnt = (jnp.minimum(pos + w // 2, seq_len) - jnp.maximum(pos - w // 2, 0)).astype(F32)
        pooled = win / cnt - p[:, g * GROUP_DIM:(g + 1) * GROUP_DIM]
        b_cols.append(_mdot(pooled.astype(MXU_DTYPE), wpool_ref[g]))
    b_out = jnp.concatenate(b_cols, axis=1) * pscale_ref[...]

    mix_in = jnp.concatenate([a_out, b_out], axis=1).astype(MXU_DTYPE)
    mix = _mdot(mix_in, wout_ref[...])
    x1 = _layer_norm(ALPHA * xt + g1 * mix, ln1g_ref[...], ln1b_ref[...])
    x1_ref[...] = x1

    h2 = x1 * (1.0 + sc2) + sh2
    h2_hi = h2.astype(MXU_DTYPE)
    h2_lo = (h2 - h2_hi.astype(F32)).astype(MXU_DTYPE)
    l1 = lax.dot_general(wr_ref[...], h2_hi, _NT, preferred_element_type=F32)
    l2 = lax.dot_general(wr_ref[0:ROUTE_PAD], h2_lo, _NT, preferred_element_type=F32)
    lt = l1[:ROUTE_PAD] + l1[ROUTE_PAD:] + l2 + rb_ref[...]

    def row(r):
        return lt[r:r + 1, :]

    gl = [row(r) for r in range(N_GROUPS)]
    gmax = jnp.maximum(jnp.maximum(gl[0], gl[1]), jnp.maximum(gl[2], gl[3]))
    gidx = jnp.where(gl[0] == gmax, 0, jnp.where(gl[1] == gmax, 1, jnp.where(gl[2] == gmax, 2, 3)))
    gsum = (jnp.exp(gl[0] - gmax) + jnp.exp(gl[1] - gmax)) + (jnp.exp(gl[2] - gmax) + jnp.exp(gl[3] - gmax))
    gw = 1.0 / gsum
    ev = [jnp.where(gidx == 0, row(N_GROUPS + j),
                    jnp.where(gidx == 1, row(N_GROUPS + EPG + j),
                              jnp.where(gidx == 2, row(N_GROUPS + 2 * EPG + j), row(N_GROUPS + 3 * EPG + j))))
          for j in range(EPG)]

    def top1(vals):
        m = vals[0]
        for t in vals[1:]:
            m = jnp.maximum(m, t)
        idx = jnp.full(m.shape, EPG - 1, I32)
        for j in range(EPG - 2, -1, -1):
            idx = jnp.where(vals[j] == m, j, idx)
        return m, idx

    v1, j1 = top1(ev)
    v2, j2 = top1([jnp.where(j1 == j, -jnp.inf, ev[j]) for j in range(EPG)])
    t2 = jnp.exp(v2 - v1)
    den = 1.0 + t2
    w1 = (1.0 / den) * gw
    w2 = (t2 / den) * gw
    first = j1 < j2
    ea = jnp.minimum(j1, j2)
    eb = jnp.maximum(j1, j2)
    wa = jnp.where(first, w1, w2)
    wb = jnp.where(first, w2, w1)
    cls = gidx * PAIRS + jnp.right_shift(ea * (2 * EPG - 1 - ea), 1) + (eb - ea - 1)

    r8 = lax.broadcasted_iota(I32, (8, ts), 0)
    rid_ref[...] = jnp.where(r8 == 0, cls, jnp.where(r8 == 1, gidx * EPG + ea,
                                                      jnp.where(r8 == 2, gidx * EPG + eb, 0)))
    w8 = jnp.where(r8 == 0, wa, jnp.where(r8 == 1, wb, 0.0))
    w128 = jnp.concatenate([w8, jnp.zeros((LANES - 8, ts), F32)], axis=0)
    bits = lax.bitcast_convert_type(h2.astype(jnp.bfloat16).astype(F32), U32)
    half = D_MODEL // 2
    for k in range(PACK_ROWS):
        lo = jnp.right_shift(bits[:, k * LANES:(k + 1) * LANES], 16)
        hi = jnp.bitwise_and(bits[:, half + k * LANES:half + (k + 1) * LANES], jnp.uint32(HI_MASK))
        rows_ref[pl.ds(k, ts, stride=SUBLANES), :] = jnp.bitwise_or(lo, hi)
    rows_ref[pl.ds(PACK_ROWS, ts, stride=SUBLANES), :] = lax.bitcast_convert_type(w128.T, U32)
    for k in range(PACK_ROWS + 1, SUBLANES):
        rows_ref[pl.ds(k, ts, stride=SUBLANES), :] = jnp.zeros((ts, LANES), U32)


def _mixer_call(x_prompt, x_sample, mod, weights):
    bp, sp, d = x_prompt.shape
    bs, ss, _ = x_sample.shape
    nsp, nss = sp // TS, ss // TS
    ntp, nts = bp * nsp, bs * nss
    n_tok = bp * sp + bs * ss
    hb = TS // POOL_HALO

    def p_tile(i):
        t = jnp.minimum(i, ntp - 1)
        return t // nsp, t % nsp

    def s_tile(i):
        t = jnp.maximum(i - ntp, 0)
        return t // nss, t % nss

    def specs(tile_fn, seq):
        def cur(i):
            b, s = tile_fn(i)
            return (b, s, 0)

        def prev(i):
            b, s = tile_fn(i)
            return (b, jnp.maximum(s * hb - 1, 0), 0)

        def nxt(i):
            b, s = tile_fn(i)
            return (b, jnp.minimum((s + 1) * hb, seq // POOL_HALO - 1), 0)

        return [pl.BlockSpec((1, TS, d), cur), pl.BlockSpec((1, POOL_HALO, d), prev),
                pl.BlockSpec((1, POOL_HALO, d), nxt)]

    def mod_map(i):
        return (jnp.where(i < ntp, p_tile(i)[0], bp + s_tile(i)[0]), 0, 0)

    def const(w):
        return pl.BlockSpec(w.shape, lambda i, nd=w.ndim: (0,) * nd)

    in_specs = (specs(p_tile, sp) + specs(s_tile, ss) + [pl.BlockSpec((1, N_MOD, d), mod_map)]
                + [const(w) for w in weights])
    out_shape = (jax.ShapeDtypeStruct((n_tok, d), F32),
                 jax.ShapeDtypeStruct((n_tok * SUBLANES, LANES), U32),
                 jax.ShapeDtypeStruct((8, n_tok), I32))
    out_specs = (pl.BlockSpec((TS, d), lambda i: (i, 0)),
                 pl.BlockSpec((TS * SUBLANES, LANES), lambda i: (i, 0)),
                 pl.BlockSpec((8, TS), lambda i: (0, i)))
    return pl.pallas_call(
        functools.partial(_mixer_kernel, prompt_tiles=ntp, prompt_seq=sp, sample_seq=ss),
        out_shape=out_shape,
        grid=(ntp + nts,),
        in_specs=in_specs,
        out_specs=out_specs,
        compiler_params=pltpu.CompilerParams(dimension_semantics=("arbitrary",),
                                             vmem_limit_bytes=VMEM_LIMIT),
        name="mixer_ln1_route",
    )(x_prompt, x_prompt, x_prompt, x_sample, x_sample, x_sample, mod, *weights)


def _rank_kernel(rid_ref, u_ref, tri_ref, dest_ref, meta_ref, cnt_ref, base_ref, *, nb_pad):
    phase = pl.program_id(0)
    j = pl.program_id(1)
    cls_iota = lax.broadcasted_iota(I32, (CLS_PAD, SUB), 0)

    @pl.when((phase == 0) & (j == 0))
    def _():
        cnt_ref[...] = jnp.zeros_like(cnt_ref)

    @pl.when(phase == 0)
    def _():
        acc = cnt_ref[...]
        for sb in range(BL // SUB):
            ids = rid_ref[0:1, sb * SUB:(sb + 1) * SUB]
            acc = acc + jnp.sum((cls_iota == ids).astype(F32), axis=1, keepdims=True)
        cnt_ref[...] = acc

    @pl.when((phase == 1) & (j == 0))
    def _():
        cnt = jnp.broadcast_to(cnt_ref[...], (CLS_PAD, CLS_PAD))
        nblk = jnp.floor((cnt + (RB - 1)) * (1.0 / RB))
        cum = jnp.dot(tri_ref[...], nblk, precision=lax.Precision.HIGHEST, preferred_element_type=F32)
        base_ref[...] = (cum[:, 0:1] - nblk[:, 0:1]) * RB
        blk = lax.broadcasted_iota(I32, (CLS_PAD, nb_pad), 1).astype(F32)
        bcls = jnp.sum((cum[:, 0:1] <= blk).astype(F32), axis=0, keepdims=True)
        bcls = jnp.minimum(bcls, N_CLASSES - 1).astype(I32)
        nused = jnp.broadcast_to(cum[CLS_PAD - 1:CLS_PAD, 0:1], (1, nb_pad)).astype(I32)
        cum_l = jnp.transpose(cum)[0:1].astype(I32)
        nblk_l = jnp.transpose(nblk)[0:1].astype(I32)
        pad = jnp.zeros((1, nb_pad - CLS_PAD), I32)
        r8 = lax.broadcasted_iota(I32, (8, nb_pad), 0)
        cum_row = jnp.concatenate([cum_l, pad], axis=1)
        nblk_row = jnp.concatenate([nblk_l, pad], axis=1)
        meta_ref[...] = jnp.where(r8 == 0, bcls, jnp.where(r8 == 1, nused,
                                  jnp.where(r8 == 2, cum_row, jnp.where(r8 == 3, nblk_row, 0))))

    @pl.when(phase == 1)
    def _():
        base = base_ref[...]
        for sb in range(BL // SUB):
            ids = rid_ref[0:1, sb * SUB:(sb + 1) * SUB]
            hit = cls_iota == ids
            incl = _mdot(hit.astype(MXU_DTYPE), u_ref[...])
            slot = jnp.sum(jnp.where(hit, base + incl - 1.0, 0.0), axis=0, keepdims=True)
            dest_ref[0:1, sb * SUB:(sb + 1) * SUB] = slot.astype(I32)
            base = base + incl[:, SUB - 1:SUB]
        base_ref[...] = base


def _rank_call(rid, u_mat, tri, nb_pad):
    n_tok = rid.shape[1]
    nj = n_tok // BL
    return pl.pallas_call(
        functools.partial(_rank_kernel, nb_pad=nb_pad),
        out_shape=(jax.ShapeDtypeStruct((1, n_tok), I32), jax.ShapeDtypeStruct((8, nb_pad), I32)),
        grid=(2, nj),
        in_specs=[pl.BlockSpec((8, BL), lambda p, j: (0, j)),
                  pl.BlockSpec(u_mat.shape, lambda p, j: (0, 0)),
                  pl.BlockSpec(tri.shape, lambda p, j: (0, 0))],
        out_specs=(pl.BlockSpec((1, BL), lambda p, j: (0, p * j)),
                   pl.BlockSpec((8, nb_pad), lambda p, j: (0, 0))),
        scratch_shapes=[pltpu.VMEM((CLS_PAD, 1), F32), pltpu.VMEM((CLS_PAD, 1), F32)],
        compiler_params=pltpu.CompilerParams(dimension_semantics=("arbitrary", "arbitrary"),
                                             vmem_limit_bytes=VMEM_LIMIT),
        name="rank_tokens",
    )(rid, u_mat, tri)


def _dispatch_kernel(cum_ref, nblk_ref, nused_ref, dest_hbm, rows_ref, out_hbm, idx_smem, zero_ref, sem_idx,
                     sem_row, sem_zero):
    i = pl.program_id(0)
    blk_rows = RB * SUBLANES
    n_blocks = out_hbm.shape[0] // blk_rows

    def zero_block(blk):
        start = pl.multiple_of(blk * blk_rows, blk_rows)
        return pltpu.make_async_copy(zero_ref, out_hbm.at[pl.ds(start, blk_rows)], sem_zero)

    @pl.when(i == 0)
    def _():
        zero_ref[...] = jnp.zeros_like(zero_ref)

        def start(c, carry):
            @pl.when(nblk_ref[c] > 0)
            def _():
                zero_block(cum_ref[c] - 1).start()
            return carry

        def wait(c, carry):
            @pl.when(nblk_ref[c] > 0)
            def _():
                zero_block(cum_ref[c] - 1).wait()
            return carry

        def start_tail(blk, carry):
            zero_block(blk).start()
            return carry

        def wait_tail(blk, carry):
            zero_block(blk).wait()
            return carry

        lax.fori_loop(0, N_CLASSES, start, 0)
        lax.fori_loop(nused_ref[0], n_blocks, start_tail, 0)
        lax.fori_loop(0, N_CLASSES, wait, 0)
        lax.fori_loop(nused_ref[0], n_blocks, wait_tail, 0)

    idx_copy = pltpu.make_async_copy(dest_hbm.at[i], idx_smem, sem_idx)
    idx_copy.start()
    idx_copy.wait()

    def scatter(t, carry):
        src = pl.multiple_of(t * SUBLANES, SUBLANES)
        dst = pl.multiple_of(idx_smem[t] * SUBLANES, SUBLANES)
        pltpu.make_async_copy(rows_ref.at[pl.ds(src, SUBLANES)], out_hbm.at[pl.ds(dst, SUBLANES)], sem_row).start()
        return carry

    lax.fori_loop(0, TD, scatter, 0, unroll=8)
    pltpu.make_async_copy(rows_ref, out_hbm.at[pl.ds(0, TD * SUBLANES)], sem_row).wait()


def _dispatch_call(cum_cls, nblk_cls, nused, dest2d, rows, n_rows):
    n_tok = rows.shape[0] // SUBLANES
    grid_spec = pltpu.PrefetchScalarGridSpec(
        num_scalar_prefetch=3,
        grid=(n_tok // TD,),
        in_specs=[pl.BlockSpec(memory_space=pl.ANY),
                  pl.BlockSpec((TD * SUBLANES, LANES), lambda i, *_: (i, 0))],
        out_specs=pl.BlockSpec(memory_space=pl.ANY),
        scratch_shapes=[pltpu.SMEM((TD,), I32), pltpu.VMEM((RB * SUBLANES, LANES), U32),
                        pltpu.SemaphoreType.DMA, pltpu.SemaphoreType.DMA, pltpu.SemaphoreType.DMA],
    )
    return pl.pallas_call(
        _dispatch_kernel,
        out_shape=jax.ShapeDtypeStruct((n_rows * SUBLANES, LANES), U32),
        grid_spec=grid_spec,
        compiler_params=pltpu.CompilerParams(dimension_semantics=("arbitrary",),
                                             vmem_limit_bytes=VMEM_LIMIT),
        name="dispatch_rows",
    )(cum_cls, nblk_cls, nused, dest2d, rows)


def _expert_kernel(ea_ref, eb_ref, nused_ref, rows_ref, wgu_a_ref, wd_a_ref, wgu_b_ref, wd_b_ref, y_ref):
    @pl.when(pl.program_id(0) < nused_ref[0])
    def _():
        def tile_row(k):
            return rows_ref[pl.ds(k, RB, stride=SUBLANES), :]

        words = [tile_row(k) for k in range(PACK_ROWS)]
        lo = [lax.bitcast_convert_type(jnp.left_shift(w, 16), F32) for w in words]
        hi = [lax.bitcast_convert_type(jnp.bitwise_and(w, jnp.uint32(HI_MASK)), F32) for w in words]
        x = jnp.concatenate(lo + hi, axis=1).astype(MXU_DTYPE)
        gate = lax.bitcast_convert_type(tile_row(PACK_ROWS), F32)

        def expert(wgu_ref, wd_ref):
            gu = _mdot(x, wgu_ref[0])
            act = jax.nn.silu(gu[:, :D_EXPERT]) * gu[:, D_EXPERT:]
            return _mdot(act.astype(MXU_DTYPE), wd_ref[0])

        y = expert(wgu_a_ref, wd_a_ref) * gate[:, 0:1] + expert(wgu_b_ref, wd_b_ref) * gate[:, 1:2]
        for k in range(Y_ROWS):
            y_ref[pl.ds(k, RB, stride=SUBLANES), :] = y[:, k * LANES:(k + 1) * LANES]

    @pl.when(pl.program_id(0) >= nused_ref[0])
    def _():
        y_ref[...] = jnp.zeros_like(y_ref)


def _expert_call(blk_ea, blk_eb, nused, rows, w_gate_up, w_down):
    n_rows = rows.shape[0] // SUBLANES
    nblk = n_rows // RB

    def live(b, nu):
        return jnp.minimum(b, nu[0] - 1)

    grid_spec = pltpu.PrefetchScalarGridSpec(
        num_scalar_prefetch=3,
        grid=(nblk,),
        in_specs=[pl.BlockSpec((RB * SUBLANES, LANES), lambda b, ea, eb, nu: (live(b, nu), 0)),
                  pl.BlockSpec((1,) + w_gate_up.shape[1:], lambda b, ea, eb, nu: (ea[live(b, nu)], 0, 0)),
                  pl.BlockSpec((1,) + w_down.shape[1:], lambda b, ea, eb, nu: (ea[live(b, nu)], 0, 0)),
                  pl.BlockSpec((1,) + w_gate_up.shape[1:], lambda b, ea, eb, nu: (eb[live(b, nu)], 0, 0)),
                  pl.BlockSpec((1,) + w_down.shape[1:], lambda b, ea, eb, nu: (eb[live(b, nu)], 0, 0))],
        out_specs=pl.BlockSpec((RB * Y_ROWS, LANES), lambda b, ea, eb, nu: (b, 0)),
    )
    return pl.pallas_call(
        _expert_kernel,
        out_shape=jax.ShapeDtypeStruct((n_rows * Y_ROWS, LANES), F32),
        grid_spec=grid_spec,
        compiler_params=pltpu.CompilerParams(dimension_semantics=("arbitrary",),
                                             vmem_limit_bytes=VMEM_LIMIT),
        name="pair_experts",
    )(blk_ea, blk_eb, nused, rows, w_gate_up, w_down, w_gate_up, w_down)


def _combine_kernel(dest_hbm, y_hbm, x1_ref, mod_ref, g_ref, b_ref, o_ref, idx_smem, ybuf, sem_idx, sem_row,
                    *, tile_off):
    ns = pl.num_programs(1)
    tile = tile_off + pl.program_id(0) * ns + pl.program_id(1)
    idx_copy = pltpu.make_async_copy(dest_hbm.at[tile], idx_smem, sem_idx)
    idx_copy.start()
    idx_copy.wait()

    def gather(t, carry):
        src = pl.multiple_of(idx_smem[t] * Y_ROWS, Y_ROWS)
        dst = pl.multiple_of(t * Y_ROWS, Y_ROWS)
        pltpu.make_async_copy(y_hbm.at[pl.ds(src, Y_ROWS)], ybuf.at[pl.ds(dst, Y_ROWS)], sem_row).start()
        return carry

    lax.fori_loop(0, TD, gather, 0, unroll=8)
    pltpu.make_async_copy(y_hbm.at[pl.ds(0, TD * Y_ROWS)], ybuf, sem_row).wait()
    y = jnp.concatenate([ybuf[pl.ds(k, TD, stride=Y_ROWS), :] for k in range(Y_ROWS)], axis=1)
    g2 = mod_ref[0][5:6]
    o_ref[0] = _layer_norm(ALPHA * x1_ref[...] + g2 * y, g_ref[...], b_ref[...])


def _combine_call(dest2d, y_rows, x1, mod, mod_off, tile_off, ln2g, ln2b, bsz, seq):
    d = x1.shape[1]
    ns = seq // TD
    return pl.pallas_call(
        functools.partial(_combine_kernel, tile_off=tile_off),
        out_shape=jax.ShapeDtypeStruct((bsz, seq, d), F32),
        grid=(bsz, ns),
        in_specs=[pl.BlockSpec(memory_space=pl.ANY),
                  pl.BlockSpec(memory_space=pl.ANY),
                  pl.BlockSpec((TD, d), lambda b, s: (tile_off + b * ns + s, 0)),
                  pl.BlockSpec((1, N_MOD, d), lambda b, s: (mod_off + b, 0, 0)),
                  pl.BlockSpec((1, d), lambda b, s: (0, 0)),
                  pl.BlockSpec((1, d), lambda b, s: (0, 0))],
        out_specs=pl.BlockSpec((1, TD, d), lambda b, s: (b, s, 0)),
        scratch_shapes=[pltpu.SMEM((TD,), I32), pltpu.VMEM((TD * Y_ROWS, LANES), F32),
                        pltpu.SemaphoreType.DMA, pltpu.SemaphoreType.DMA],
        compiler_params=pltpu.CompilerParams(dimension_semantics=("arbitrary", "arbitrary"),
                                             vmem_limit_bytes=VMEM_LIMIT),
        name="combine_ln2",
    )(dest2d, y_rows, x1, mod, ln2g, ln2b)


def _band_matrices():
    rows = np.arange(CHUNK)[:, None]
    cols = np.arange(CHUNK + 2 * POOL_HALO)[None, :] - POOL_HALO
    return np.stack([((cols >= rows - w // 2) & (cols < rows + w // 2)) for w in POOL_WINDOWS]).astype(np.float32)


def _class_tables():
    ea, eb = [], []
    for g in range(N_GROUPS):
        for a in range(EPG):
            for b in range(a + 1, EPG):
                ea.append(g * EPG + a)
                eb.append(g * EPG + b)
    return np.asarray(ea, np.int32), np.asarray(eb, np.int32)


def _split_hi_lo(w):
    hi = w.astype(MXU_DTYPE)
    lo = (w - hi.astype(F32)).astype(MXU_DTYPE)
    return hi, lo


def kernel(x_prompt, x_sample, c_prompt, c_sample, w_ada, b_ada, w_in, v_norm_g, v_norm_b, w_spatial, b_spatial,
           w_pool, pool_scale, w_out, ln1_g, ln1_b, w_route_group, b_route_group, w_route_expert, b_route_expert,
           w_gate_up, w_down, ln2_g, ln2_b):
    assert w_ada.shape[0] == 1, "single-layer kernel"
    bp, sp, d = x_prompt.shape
    bs, ss, _ = x_sample.shape
    assert d == D_MODEL and sp % TS == 0 and ss % TS == 0
    n_prompt, n_sample = bp * sp, bs * ss
    n_tok = n_prompt + n_sample
    assert n_tok % BL == 0 and n_prompt % TD == 0

    c_all = jnp.concatenate([c_prompt, c_sample], axis=0)
    mod = _modulation(c_all, w_ada[0], b_ada).reshape(bp + bs, N_MOD, d)

    wr = jnp.concatenate([w_route_group[0], w_route_expert[0]], axis=1).T
    wr = jnp.pad(wr, ((0, ROUTE_PAD - N_ROUTE), (0, 0)))
    wr_hi, wr_lo = _split_hi_lo(wr)
    rbias = jnp.concatenate([b_route_group[0], b_route_expert[0].reshape(-1),
                             jnp.zeros((ROUTE_PAD - N_ROUTE,), F32)]).reshape(ROUTE_PAD, 1)
    row = lambda a: a.reshape(1, -1)
    weights = (w_in[0].astype(MXU_DTYPE), row(v_norm_g[0]), row(v_norm_b[0]), w_spatial[0].astype(MXU_DTYPE),
               b_spatial[0].T, jnp.asarray(_band_matrices(), MXU_DTYPE), w_pool[0].astype(MXU_DTYPE),
               row(pool_scale[0]), w_out[0].astype(MXU_DTYPE), row(ln1_g[0]), row(ln1_b[0]),
               jnp.concatenate([wr_hi, wr_lo], axis=0), rbias)

    x1, rows, rid = _mixer_call(x_prompt, x_sample, mod, weights)

    nblk = (n_tok + N_CLASSES * (RB - 1)) // RB
    nb_pad = -(-nblk // 128) * 128
    u_mat = jnp.asarray(np.triu(np.ones((SUB, SUB), np.float32)), MXU_DTYPE)
    tri = jnp.asarray(np.tril(np.ones((CLS_PAD, CLS_PAD), np.float32)))
    dest, meta = _rank_call(rid, u_mat, tri, nb_pad)
    dest2d = dest.reshape(n_tok // TD, TD)
    tab_a, tab_b = _class_tables()
    blk_cls = meta[0, :nblk]
    blk_ea = jnp.asarray(tab_a)[blk_cls]
    blk_eb = jnp.asarray(tab_b)[blk_cls]
    nused = meta[1, 0:1]
    cum_cls = meta[2, :CLS_PAD]
    nblk_cls = meta[3, :CLS_PAD]

    sorted_rows = _dispatch_call(cum_cls, nblk_cls, nused, dest2d, rows, nblk * RB)
    y_rows = _expert_call(blk_ea, blk_eb, nused, sorted_rows, w_gate_up[0].astype(MXU_DTYPE),
                          w_down[0].astype(MXU_DTYPE))

    ln2g, ln2b = row(ln2_g[0]), row(ln2_b[0])
    y_prompt = _combine_call(dest2d, y_rows, x1, mod, 0, 0, ln2g, ln2b, bp, sp)
    y_sample = _combine_call(dest2d, y_rows, x1, mod, bp, n_prompt // TD, ln2g, ln2b, bs, ss)
    return (y_prompt, y_sample)
```

```python
import functools

import numpy as np
import jax
import jax.numpy as jnp
from jax import lax
from jax.experimental import pallas as pl
from jax.experimental.pallas import tpu as pltpu

F32 = jnp.float32
I32 = jnp.int32
MXU_DTYPE = jnp.bfloat16

D_MODEL = 1024
A_WIDTH = 512
B_WIDTH = 512
CHUNK = 128
A_HEADS = 4
HEAD_DIM = A_WIDTH // A_HEADS
POOL_WINDOWS = (2, 4, 8, 16)
POOL_HALO = 8
GROUP_DIM = B_WIDTH // len(POOL_WINDOWS)
N_GROUPS = 4
EPG = 8
N_EXPERTS = N_GROUPS * EPG
D_EXPERT = 512
N_MOD = 6
LN_EPS = 1e-5
ALPHA = 2.0 ** 0.25

PAIRS = EPG * (EPG - 1) // 2
N_CLASSES = N_GROUPS * PAIRS
CLS_PAD = 128
N_ROUTE = N_GROUPS + N_EXPERTS
ROUTE_PAD = 40
U32 = jnp.uint32
LANES = 128
SUBLANES = 8
PACK_ROWS = D_MODEL // (2 * LANES)
Y_ROWS = D_MODEL // LANES
assert PACK_ROWS < SUBLANES and Y_ROWS == SUBLANES
HI_MASK = 0xFFFF0000

TS = 512
TD = 512
RB = 256
ROW_UNROLL = 8
BL = 2048
SUB = 256
VMEM_LIMIT = 56 * 1024 * 1024

_NT = (((1,), (1,)), ((), ()))


def _layer_norm(x, g, b):
    mu = jnp.mean(x, axis=-1, keepdims=True)
    xc = x - mu
    var = jnp.mean(xc * xc, axis=-1, keepdims=True)
    return xc * lax.rsqrt(var + LN_EPS) * g + b


def _mdot(a, b):
    return jnp.dot(a, b, preferred_element_type=F32)


def _mod_kernel(c_ref, w_ref, b_ref, o_ref):
    a = jax.nn.silu(c_ref[...])
    o_ref[...] = jnp.dot(a, w_ref[...], precision=lax.Precision.HIGHEST,
                         preferred_element_type=F32) + b_ref[...]


def _modulation(c_all, w_ada, b_ada):
    nb, d = c_all.shape
    n = w_ada.shape[1]
    bn = 1536
    return pl.pallas_call(
        _mod_kernel,
        out_shape=jax.ShapeDtypeStruct((nb, n), F32),
        grid=(n // bn,),
        in_specs=[pl.BlockSpec((nb, d), lambda j: (0, 0)),
                  pl.BlockSpec((d, bn), lambda j: (0, j)),
                  pl.BlockSpec((1, bn), lambda j: (0, j))],
        out_specs=pl.BlockSpec((nb, bn), lambda j: (0, j)),
        compiler_params=pltpu.CompilerParams(dimension_semantics=("arbitrary",),
                                             vmem_limit_bytes=VMEM_LIMIT),
        name="adaln_mod",
    )(c_all, w_ada, b_ada)


def _mixer_kernel(xp_ref, xpp_ref, xpn_ref, xs_ref, xsp_ref, xsn_ref, mod_ref, win_ref, vng_ref, vnb_ref, ws_ref,
                  bst_ref, band_ref, wpool_ref, pscale_ref, wout_ref, ln1g_ref, ln1b_ref, wr_ref, rb_ref,
                  x1_ref, rows_ref, rid_ref, *, prompt_tiles, prompt_seq, sample_seq):
    ts = xp_ref.shape[1]
    i = pl.program_id(0)
    is_p = i < prompt_tiles
    seq_len = jnp.where(is_p, prompt_seq, sample_seq)
    ns = jnp.where(is_p, prompt_seq // ts, sample_seq // ts)
    s = jnp.where(is_p, i, i - prompt_tiles) % ns
    md = mod_ref[0]
    sh1, sc1, g1, sh2, sc2 = md[0:1], md[1:2], md[2:3], md[3:4], md[4:5]

    xt = jnp.where(is_p, xp_ref[0], xs_ref[0])
    h = xt * (1.0 + sc1) + sh1
    hp = jnp.where(s > 0, jnp.where(is_p, xpp_ref[0], xsp_ref[0]) * (1.0 + sc1) + sh1, 0.0)
    hn = jnp.where(s < ns - 1, jnp.where(is_p, xpn_ref[0], xsn_ref[0]) * (1.0 + sc1) + sh1, 0.0)
    hext = jnp.concatenate([h, hp, hn], axis=0).astype(MXU_DTYPE)
    zext = _mdot(hext, win_ref[...])
    z = zext[:ts]

    za = jax.nn.gelu(z[:, :2 * A_WIDTH])
    u = za[:, :A_WIDTH]
    v = _layer_norm(za[:, A_WIDTH:], vng_ref[...], vnb_ref[...]).astype(MXU_DTYPE)
    bst = bst_ref[...]
    row_blocks = []
    for c in range(ts // CHUNK):
        cols = []
        for hh in range(A_HEADS):
            vv = v[c * CHUNK:(c + 1) * CHUNK, hh * HEAD_DIM:(hh + 1) * HEAD_DIM]
            cols.append(_mdot(ws_ref[hh], vv) + bst[:, hh:hh + 1])
        row_blocks.append(jnp.concatenate(cols, axis=1))
    a_out = u * jnp.concatenate(row_blocks, axis=0)

    p = z[:, 2 * A_WIDTH:]
    pall = jnp.concatenate([zext[ts:ts + POOL_HALO, 2 * A_WIDTH:], p,
                            zext[ts + POOL_HALO:ts + 2 * POOL_HALO, 2 * A_WIDTH:]], axis=0)
    p_hi = pall.astype(MXU_DTYPE)
    p_lo = (pall - p_hi.astype(F32)).astype(MXU_DTYPE)
    pos = s * ts + lax.broadcasted_iota(I32, (ts, 1), 0)
    b_cols = []
    for g, w in enumerate(POOL_WINDOWS):
        wins = []
        for c in range(ts // CHUNK):
            r0 = c * CHUNK
            seg_hi = p_hi[r0:r0 + CHUNK + 2 * POOL_HALO, g * GROUP_DIM:(g + 1) * GROUP_DIM]
            seg_lo = p_lo[r0:r0 + CHUNK + 2 * POOL_HALO, g * GROUP_DIM:(g + 1) * GROUP_DIM]
            wins.append(_mdot(band_ref[g], seg_hi) + _mdot(band_ref[g], seg_lo))
        win = jnp.concatenate(wins, axis=0)
        cnt = (jnp.minimum(pos + w // 2, seq_len) - jnp.maximum(pos - w // 2, 0)).astype(F32)
        pooled = win / cnt - p[:, g * GROUP_DIM:(g + 1) * GROUP_DIM]
        b_cols.append(_mdot(pooled.astype(MXU_DTYPE), wpool_ref[g]))
    b_out = jnp.concatenate(b_cols, axis=1) * pscale_ref[...]

    mix_in = jnp.concatenate([a_out, b_out], axis=1).astype(MXU_DTYPE)
    mix = _mdot(mix_in, wout_ref[...])
    x1 = _layer_norm(ALPHA * xt + g1 * mix, ln1g_ref[...], ln1b_ref[...])
    x1_ref[...] = x1

    h2 = x1 * (1.0 + sc2) + sh2
    h2_hi = h2.astype(MXU_DTYPE)
    h2_lo = (h2 - h2_hi.astype(F32)).astype(MXU_DTYPE)
    l1 = lax.dot_general(wr_ref[...], h2_hi, _NT, preferred_element_type=F32)
    l2 = lax.dot_general(wr_ref[0:ROUTE_PAD], h2_lo, _NT, preferred_element_type=F32)
    lt = l1[:ROUTE_PAD] + l1[ROUTE_PAD:] + l2 + rb_ref[...]

    def row(r):
        return lt[r:r + 1, :]

    gl = [row(r) for r in range(N_GROUPS)]
    gmax = jnp.maximum(jnp.maximum(gl[0], gl[1]), jnp.maximum(gl[2], gl[3]))
    gidx = jnp.where(gl[0] == gmax, 0, jnp.where(gl[1] == gmax, 1, jnp.where(gl[2] == gmax, 2, 3)))
    gsum = (jnp.exp(gl[0] - gmax) + jnp.exp(gl[1] - gmax)) + (jnp.exp(gl[2] - gmax) + jnp.exp(gl[3] - gmax))
    gw = 1.0 / gsum
    ev = [jnp.where(gidx == 0, row(N_GROUPS + j),
                    jnp.where(gidx == 1, row(N_GROUPS + EPG + j),
                              jnp.where(gidx == 2, row(N_GROUPS + 2 * EPG + j), row(N_GROUPS + 3 * EPG + j))))
          for j in range(EPG)]

    def top1(vals):
        m = vals[0]
        for t in vals[1:]:
            m = jnp.maximum(m, t)
        idx = jnp.full(m.shape, EPG - 1, I32)
        for j in range(EPG - 2, -1, -1):
            idx = jnp.where(vals[j] == m, j, idx)
        return m, idx

    v1, j1 = top1(ev)
    v2, j2 = top1([jnp.where(j1 == j, -jnp.inf, ev[j]) for j in range(EPG)])
    t2 = jnp.exp(v2 - v1)
    den = 1.0 + t2
    w1 = (1.0 / den) * gw
    w2 = (t2 / den) * gw
    first = j1 < j2
    ea = jnp.minimum(j1, j2)
    eb = jnp.maximum(j1, j2)
    wa = jnp.where(first, w1, w2)
    wb = jnp.where(first, w2, w1)
    cls = gidx * PAIRS + jnp.right_shift(ea * (2 * EPG - 1 - ea), 1) + (eb - ea - 1)

    r8 = lax.broadcasted_iota(I32, (8, ts), 0)
    rid_ref[...] = jnp.where(r8 == 0, cls, jnp.where(r8 == 1, gidx * EPG + ea,
                                                      jnp.where(r8 == 2, gidx * EPG + eb, 0)))
    w8 = jnp.where(r8 == 0, wa, jnp.where(r8 == 1, wb, 0.0))
    w128 = jnp.concatenate([w8, jnp.zeros((LANES - 8, ts), F32)], axis=0)
    bits = lax.bitcast_convert_type(h2.astype(jnp.bfloat16).astype(F32), U32)
    half = D_MODEL // 2
    for k in range(PACK_ROWS):
        lo = jnp.right_shift(bits[:, k * LANES:(k + 1) * LANES], 16)
        hi = jnp.bitwise_and(bits[:, half + k * LANES:half + (k + 1) * LANES], jnp.uint32(HI_MASK))
        rows_ref[pl.ds(k, ts, stride=SUBLANES), :] = jnp.bitwise_or(lo, hi)
    rows_ref[pl.ds(PACK_ROWS, ts, stride=SUBLANES), :] = lax.bitcast_convert_type(w128.T, U32)
    for k in range(PACK_ROWS + 1, SUBLANES):
        rows_ref[pl.ds(k, ts, stride=SUBLANES), :] = jnp.zeros((ts, LANES), U32)


def _mixer_call(x_prompt, x_sample, mod, weights):
    bp, sp, d = x_prompt.shape
    bs, ss, _ = x_sample.shape
    nsp, nss = sp // TS, ss // TS
    ntp, nts = bp * nsp, bs * nss
    n_tok = bp * sp + bs * ss
    hb = TS // POOL_HALO

    def p_tile(i):
        t = jnp.minimum(i, ntp - 1)
        return t // nsp, t % nsp

    def s_tile(i):
        t = jnp.maximum(i - ntp, 0)
        return t // nss, t % nss

    def specs(tile_fn, seq):
        def cur(i):
            b, s = tile_fn(i)
            return (b, s, 0)

        def prev(i):
            b, s = tile_fn(i)
            return (b, jnp.maximum(s * hb - 1, 0), 0)

        def nxt(i):
            b, s = tile_fn(i)
            return (b, jnp.minimum((s + 1) * hb, seq // POOL_HALO - 1), 0)

        return [pl.BlockSpec((1, TS, d), cur), pl.BlockSpec((1, POOL_HALO, d), prev),
                pl.BlockSpec((1, POOL_HALO, d), nxt)]

    def mod_map(i):
        return (jnp.where(i < ntp, p_tile(i)[0], bp + s_tile(i)[0]), 0, 0)

    def const(w):
        return pl.BlockSpec(w.shape, lambda i, nd=w.ndim: (0,) * nd)

    in_specs = (specs(p_tile, sp) + specs(s_tile, ss) + [pl.BlockSpec((1, N_MOD, d), mod_map)]
                + [const(w) for w in weights])
    out_shape = (jax.ShapeDtypeStruct((n_tok, d), F32),
                 jax.ShapeDtypeStruct((n_tok * SUBLANES, LANES), U32),
                 jax.ShapeDtypeStruct((8, n_tok), I32))
    out_specs = (pl.BlockSpec((TS, d), lambda i: (i, 0)),
                 pl.BlockSpec((TS * SUBLANES, LANES), lambda i: (i, 0)),
                 pl.BlockSpec((8, TS), lambda i: (0, i)))
    return pl.pallas_call(
        functools.partial(_mixer_kernel, prompt_tiles=ntp, prompt_seq=sp, sample_seq=ss),
        out_shape=out_shape,
        grid=(ntp + nts,),
        in_specs=in_specs,
        out_specs=out_specs,
        compiler_params=pltpu.CompilerParams(dimension_semantics=("arbitrary",),
                                             vmem_limit_bytes=VMEM_LIMIT),
        name="mixer_ln1_route",
    )(x_prompt, x_prompt, x_prompt, x_sample, x_sample, x_sample, mod, *weights)


def _rank_kernel(rid_ref, u_ref, tri_ref, dest_ref, meta_ref, cnt_ref, base_ref, *, nb_pad):
    phase = pl.program_id(0)
    j = pl.program_id(1)
    cls_iota = lax.broadcasted_iota(I32, (CLS_PAD, SUB), 0)

    @pl.when((phase == 0) & (j == 0))
    def _():
        cnt_ref[...] = jnp.zeros_like(cnt_ref)

    @pl.when(phase == 0)
    def _():
        acc = cnt_ref[...]
        for sb in range(BL // SUB):
            ids = rid_ref[0:1, sb * SUB:(sb + 1) * SUB]
            acc = acc + jnp.sum((cls_iota == ids).astype(F32), axis=1, keepdims=True)
        cnt_ref[...] = acc

    @pl.when((phase == 1) & (j == 0))
    def _():
        cnt = jnp.broadcast_to(cnt_ref[...], (CLS_PAD, CLS_PAD))
        nblk = jnp.floor((cnt + (RB - 1)) * (1.0 / RB))
        cum = jnp.dot(tri_ref[...], nblk, precision=lax.Precision.HIGHEST, preferred_element_type=F32)
        base_ref[...] = (cum[:, 0:1] - nblk[:, 0:1]) * RB
        blk = lax.broadcasted_iota(I32, (CLS_PAD, nb_pad), 1).astype(F32)
        bcls = jnp.sum((cum[:, 0:1] <= blk).astype(F32), axis=0, keepdims=True)
        bcls = jnp.minimum(bcls, N_CLASSES - 1).astype(I32)
        nused = jnp.broadcast_to(cum[CLS_PAD - 1:CLS_PAD, 0:1], (1, nb_pad)).astype(I32)
        cum_l = jnp.transpose(cum)[0:1].astype(I32)
        nblk_l = jnp.transpose(nblk)[0:1].astype(I32)
        pad = jnp.zeros((1, nb_pad - CLS_PAD), I32)
        r8 = lax.broadcasted_iota(I32, (8, nb_pad), 0)
        cum_row = jnp.concatenate([cum_l, pad], axis=1)
        nblk_row = jnp.concatenate([nblk_l, pad], axis=1)
        meta_ref[...] = jnp.where(r8 == 0, bcls, jnp.where(r8 == 1, nused,
                                  jnp.where(r8 == 2, cum_row, jnp.where(r8 == 3, nblk_row, 0))))

    @pl.when(phase == 1)
    def _():
        base = base_ref[...]
        for sb in range(BL // SUB):
            ids = rid_ref[0:1, sb * SUB:(sb + 1) * SUB]
            hit = cls_iota == ids
            incl = _mdot(hit.astype(MXU_DTYPE), u_ref[...])
            slot = jnp.sum(jnp.where(hit, base + incl - 1.0, 0.0), axis=0, keepdims=True)
            dest_ref[0:1, sb * SUB:(sb + 1) * SUB] = slot.astype(I32)
            base = base + incl[:, SUB - 1:SUB]
        base_ref[...] = base


def _rank_call(rid, u_mat, tri, nb_pad):
    n_tok = rid.shape[1]
    nj = n_tok // BL
    return pl.pallas_call(
        functools.partial(_rank_kernel, nb_pad=nb_pad),
        out_shape=(jax.ShapeDtypeStruct((1, n_tok), I32), jax.ShapeDtypeStruct((8, nb_pad), I32)),
        grid=(2, nj),
        in_specs=[pl.BlockSpec((8, BL), lambda p, j: (0, j)),
                  pl.BlockSpec(u_mat.shape, lambda p, j: (0, 0)),
                  pl.BlockSpec(tri.shape, lambda p, j: (0, 0))],
        out_specs=(pl.BlockSpec((1, BL), lambda p, j: (0, p * j)),
                   pl.BlockSpec((8, nb_pad), lambda p, j: (0, 0))),
        scratch_shapes=[pltpu.VMEM((CLS_PAD, 1), F32), pltpu.VMEM((CLS_PAD, 1), F32)],
        compiler_params=pltpu.CompilerParams(dimension_semantics=("arbitrary", "arbitrary"),
                                             vmem_limit_bytes=VMEM_LIMIT),
        name="rank_tokens",
    )(rid, u_mat, tri)


def _dispatch_kernel(cum_ref, nblk_ref, nused_ref, dest_hbm, rows_hbm, out_hbm, idx_smem, rows_buf, zero_ref,
                     sem_idx, sem_in, sem_row, sem_zero):
    i = pl.program_id(0)
    n_steps = pl.num_programs(0)
    slot = i % 2
    tile_rows = TD * SUBLANES
    blk_rows = RB * SUBLANES
    n_blocks = out_hbm.shape[0] // blk_rows

    def fetch(step, s):
        start = pl.multiple_of(step * tile_rows, tile_rows)
        return (pltpu.make_async_copy(dest_hbm.at[step], idx_smem.at[s], sem_idx.at[s]),
                pltpu.make_async_copy(rows_hbm.at[pl.ds(start, tile_rows)], rows_buf.at[s], sem_in.at[s]))

    def scattered(s):
        return pltpu.make_async_copy(rows_buf.at[s], out_hbm.at[pl.ds(0, tile_rows)], sem_row.at[s])

    def zero_block(blk):
        start = pl.multiple_of(blk * blk_rows, blk_rows)
        return pltpu.make_async_copy(zero_ref, out_hbm.at[pl.ds(start, blk_rows)], sem_zero)

    @pl.when(i == 0)
    def _():
        zero_ref[...] = jnp.zeros_like(zero_ref)

        def start(c, carry):
            @pl.when(nblk_ref[c] > 0)
            def _():
                zero_block(cum_ref[c] - 1).start()
            return carry

        def wait(c, carry):
            @pl.when(nblk_ref[c] > 0)
            def _():
                zero_block(cum_ref[c] - 1).wait()
            return carry

        def start_tail(blk, carry):
            zero_block(blk).start()
            return carry

        def wait_tail(blk, carry):
            zero_block(blk).wait()
            return carry

        lax.fori_loop(0, N_CLASSES, start, 0)
        lax.fori_loop(nused_ref[0], n_blocks, start_tail, 0)
        lax.fori_loop(0, N_CLASSES, wait, 0)
        lax.fori_loop(nused_ref[0], n_blocks, wait_tail, 0)
        for cp in fetch(0, 0):
            cp.start()

    for cp in fetch(i, slot):
        cp.wait()

    @pl.when(i > 0)
    def _():
        scattered(1 - slot).wait()

    @pl.when(i + 1 < n_steps)
    def _():
        for cp in fetch(i + 1, 1 - slot):
            cp.start()

    def scatter(g, carry):
        for u in range(ROW_UNROLL):
            t = g * ROW_UNROLL + u
            src = pl.multiple_of(t * SUBLANES, SUBLANES)
            dst = pl.multiple_of(idx_smem[slot, t] * SUBLANES, SUBLANES)
            pltpu.make_async_copy(rows_buf.at[slot, pl.ds(src, SUBLANES)], out_hbm.at[pl.ds(dst, SUBLANES)],
                                  sem_row.at[slot]).start(priority=u % 2)
        return carry

    lax.fori_loop(0, TD // ROW_UNROLL, scatter, 0)

    @pl.when(i + 1 == n_steps)
    def _():
        scattered(slot).wait()


def _dispatch_call(cum_cls, nblk_cls, nused, dest2d, rows, n_rows):
    n_tok = rows.shape[0] // SUBLANES
    grid_spec = pltpu.PrefetchScalarGridSpec(
        num_scalar_prefetch=3,
        grid=(n_tok // TD,),
        in_specs=[pl.BlockSpec(memory_space=pl.ANY), pl.BlockSpec(memory_space=pl.ANY)],
        out_specs=pl.BlockSpec(memory_space=pl.ANY),
        scratch_shapes=[pltpu.SMEM((2, TD), I32), pltpu.VMEM((2, TD * SUBLANES, LANES), U32),
                        pltpu.VMEM((RB * SUBLANES, LANES), U32),
                        pltpu.SemaphoreType.DMA((2,)), pltpu.SemaphoreType.DMA((2,)),
                        pltpu.SemaphoreType.DMA((2,)), pltpu.SemaphoreType.DMA],
    )
    return pl.pallas_call(
        _dispatch_kernel,
        out_shape=jax.ShapeDtypeStruct((n_rows * SUBLANES, LANES), U32),
        grid_spec=grid_spec,
        compiler_params=pltpu.CompilerParams(dimension_semantics=("arbitrary",),
                                             vmem_limit_bytes=VMEM_LIMIT),
        name="dispatch_rows",
    )(cum_cls, nblk_cls, nused, dest2d, rows)


def _expert_kernel(ea_ref, eb_ref, nused_ref, rows_ref, wgu_a_ref, wd_a_ref, wgu_b_ref, wd_b_ref, y_ref):
    @pl.when(pl.program_id(0) < nused_ref[0])
    def _():
        def tile_row(k):
            return rows_ref[pl.ds(k, RB, stride=SUBLANES), :]

        words = [tile_row(k) for k in range(PACK_ROWS)]
        lo = [lax.bitcast_convert_type(jnp.left_shift(w, 16), F32) for w in words]
        hi = [lax.bitcast_convert_type(jnp.bitwise_and(w, jnp.uint32(HI_MASK)), F32) for w in words]
        x = jnp.concatenate(lo + hi, axis=1).astype(MXU_DTYPE)
        gate = lax.bitcast_convert_type(tile_row(PACK_ROWS), F32)

        def expert(wgu_ref, wd_ref):
            gu = _mdot(x, wgu_ref[0])
            act = jax.nn.silu(gu[:, :D_EXPERT]) * gu[:, D_EXPERT:]
            return _mdot(act.astype(MXU_DTYPE), wd_ref[0])

        y = expert(wgu_a_ref, wd_a_ref) * gate[:, 0:1] + expert(wgu_b_ref, wd_b_ref) * gate[:, 1:2]
        for k in range(Y_ROWS):
            y_ref[pl.ds(k, RB, stride=SUBLANES), :] = y[:, k * LANES:(k + 1) * LANES]

    @pl.when(pl.program_id(0) >= nused_ref[0])
    def _():
        y_ref[...] = jnp.zeros_like(y_ref)


def _expert_call(blk_ea, blk_eb, nused, rows, w_gate_up, w_down):
    n_rows = rows.shape[0] // SUBLANES
    nblk = n_rows // RB

    def live(b, nu):
        return jnp.maximum(jnp.minimum(b, nu[0] - 1), 0)

    grid_spec = pltpu.PrefetchScalarGridSpec(
        num_scalar_prefetch=3,
        grid=(nblk,),
        in_specs=[pl.BlockSpec((RB * SUBLANES, LANES), lambda b, ea, eb, nu: (live(b, nu), 0)),
                  pl.BlockSpec((1,) + w_gate_up.shape[1:], lambda b, ea, eb, nu: (ea[live(b, nu)], 0, 0)),
                  pl.BlockSpec((1,) + w_down.shape[1:], lambda b, ea, eb, nu: (ea[live(b, nu)], 0, 0)),
                  pl.BlockSpec((1,) + w_gate_up.shape[1:], lambda b, ea, eb, nu: (eb[live(b, nu)], 0, 0)),
                  pl.BlockSpec((1,) + w_down.shape[1:], lambda b, ea, eb, nu: (eb[live(b, nu)], 0, 0))],
        out_specs=pl.BlockSpec((RB * Y_ROWS, LANES), lambda b, ea, eb, nu: (b, 0)),
    )
    return pl.pallas_call(
        _expert_kernel,
        out_shape=jax.ShapeDtypeStruct((n_rows * Y_ROWS, LANES), F32),
        grid_spec=grid_spec,
        compiler_params=pltpu.CompilerParams(dimension_semantics=("arbitrary",),
                                             vmem_limit_bytes=VMEM_LIMIT),
        name="pair_experts",
    )(blk_ea, blk_eb, nused, rows, w_gate_up, w_down, w_gate_up, w_down)


def _combine_kernel(dest_hbm, y_hbm, x1_ref, mod_ref, g_ref, b_ref, o_ref, idx_smem, ybuf, sem_idx, sem_row,
                    *, tile_off):
    ns = pl.num_programs(1)
    i = pl.program_id(0) * ns + pl.program_id(1)
    n_steps = pl.num_programs(0) * ns
    slot = i % 2

    def idx_copy(step, s):
        return pltpu.make_async_copy(dest_hbm.at[tile_off + step], idx_smem.at[s], sem_idx.at[s])

    def issue_gather(s):
        def gather(g, carry):
            for u in range(ROW_UNROLL):
                t = g * ROW_UNROLL + u
                src = pl.multiple_of(idx_smem[s, t] * Y_ROWS, Y_ROWS)
                dst = pl.multiple_of(t * Y_ROWS, Y_ROWS)
                pltpu.make_async_copy(y_hbm.at[pl.ds(src, Y_ROWS)], ybuf.at[s, pl.ds(dst, Y_ROWS)],
                                      sem_row.at[s]).start(priority=u % 2)
            return carry

        lax.fori_loop(0, TD // ROW_UNROLL, gather, 0)

    @pl.when(i == 0)
    def _():
        idx_copy(0, 0).start()
        idx_copy(0, 0).wait()
        issue_gather(0)

        @pl.when(n_steps > 1)
        def _():
            idx_copy(1, 1).start()

    @pl.when(i + 1 < n_steps)
    def _():
        idx_copy(i + 1, 1 - slot).wait()
        issue_gather(1 - slot)

    @pl.when(i + 2 < n_steps)
    def _():
        idx_copy(i + 2, slot).start()

    pltpu.make_async_copy(y_hbm.at[pl.ds(0, TD * Y_ROWS)], ybuf.at[slot], sem_row.at[slot]).wait()

    y = jnp.concatenate([ybuf[slot, pl.ds(k, TD, stride=Y_ROWS), :] for k in range(Y_ROWS)], axis=1)
    g2 = mod_ref[0][5:6]
    o_ref[0] = _layer_norm(ALPHA * x1_ref[...] + g2 * y, g_ref[...], b_ref[...])


def _combine_call(dest2d, y_rows, x1, mod, mod_off, tile_off, ln2g, ln2b, bsz, seq):
    d = x1.shape[1]
    ns = seq // TD
    return pl.pallas_call(
        functools.partial(_combine_kernel, tile_off=tile_off),
        out_shape=jax.ShapeDtypeStruct((bsz, seq, d), F32),
        grid=(bsz, ns),
        in_specs=[pl.BlockSpec(memory_space=pl.ANY),
                  pl.BlockSpec(memory_space=pl.ANY),
                  pl.BlockSpec((TD, d), lambda b, s: (tile_off + b * ns + s, 0)),
                  pl.BlockSpec((1, N_MOD, d), lambda b, s: (mod_off + b, 0, 0)),
                  pl.BlockSpec((1, d), lambda b, s: (0, 0)),
                  pl.BlockSpec((1, d), lambda b, s: (0, 0))],
        out_specs=pl.BlockSpec((1, TD, d), lambda b, s: (b, s, 0)),
        scratch_shapes=[pltpu.SMEM((2, TD), I32), pltpu.VMEM((2, TD * Y_ROWS, LANES), F32),
                        pltpu.SemaphoreType.DMA((2,)), pltpu.SemaphoreType.DMA((2,))],
        compiler_params=pltpu.CompilerParams(dimension_semantics=("arbitrary", "arbitrary"),
                                             vmem_limit_bytes=VMEM_LIMIT),
        name="combine_ln2",
    )(dest2d, y_rows, x1, mod, ln2g, ln2b)


def _band_matrices():
    rows = np.arange(CHUNK)[:, None]
    cols = np.arange(CHUNK + 2 * POOL_HALO)[None, :] - POOL_HALO
    return np.stack([((cols >= rows - w // 2) & (cols < rows + w // 2)) for w in POOL_WINDOWS]).astype(np.float32)


def _class_tables():
    ea, eb = [], []
    for g in range(N_GROUPS):
        for a in range(EPG):
            for b in range(a + 1, EPG):
                ea.append(g * EPG + a)
                eb.append(g * EPG + b)
    return np.asarray(ea, np.int32), np.asarray(eb, np.int32)


def _split_hi_lo(w):
    hi = w.astype(MXU_DTYPE)
    lo = (w - hi.astype(F32)).astype(MXU_DTYPE)
    return hi, lo


def kernel(x_prompt, x_sample, c_prompt, c_sample, w_ada, b_ada, w_in, v_norm_g, v_norm_b, w_spatial, b_spatial,
           w_pool, pool_scale, w_out, ln1_g, ln1_b, w_route_group, b_route_group, w_route_expert, b_route_expert,
           w_gate_up, w_down, ln2_g, ln2_b):
    assert w_ada.shape[0] == 1, "single-layer kernel"
    bp, sp, d = x_prompt.shape
    bs, ss, _ = x_sample.shape
    assert d == D_MODEL and sp % TS == 0 and ss % TS == 0
    n_prompt, n_sample = bp * sp, bs * ss
    n_tok = n_prompt + n_sample
    assert n_tok % BL == 0 and n_prompt % TD == 0

    c_all = jnp.concatenate([c_prompt, c_sample], axis=0)
    mod = _modulation(c_all, w_ada[0], b_ada).reshape(bp + bs, N_MOD, d)

    wr = jnp.concatenate([w_route_group[0], w_route_expert[0]], axis=1).T
    wr = jnp.pad(wr, ((0, ROUTE_PAD - N_ROUTE), (0, 0)))
    wr_hi, wr_lo = _split_hi_lo(wr)
    rbias = jnp.concatenate([b_route_group[0], b_route_expert[0].reshape(-1),
                             jnp.zeros((ROUTE_PAD - N_ROUTE,), F32)]).reshape(ROUTE_PAD, 1)
    row = lambda a: a.reshape(1, -1)
    weights = (w_in[0].astype(MXU_DTYPE), row(v_norm_g[0]), row(v_norm_b[0]), w_spatial[0].astype(MXU_DTYPE),
               b_spatial[0].T, jnp.asarray(_band_matrices(), MXU_DTYPE), w_pool[0].astype(MXU_DTYPE),
               row(pool_scale[0]), w_out[0].astype(MXU_DTYPE), row(ln1_g[0]), row(ln1_b[0]),
               jnp.concatenate([wr_hi, wr_lo], axis=0), rbias)

    x1, rows, rid = _mixer_call(x_prompt, x_sample, mod, weights)

    nblk = (n_tok + N_CLASSES * (RB - 1)) // RB
    nb_pad = -(-nblk // 128) * 128
    u_mat = jnp.asarray(np.triu(np.ones((SUB, SUB), np.float32)), MXU_DTYPE)
    tri = jnp.asarray(np.tril(np.ones((CLS_PAD, CLS_PAD), np.float32)))
    dest, meta = _rank_call(rid, u_mat, tri, nb_pad)
    dest2d = dest.reshape(n_tok // TD, TD)
    tab_a, tab_b = _class_tables()
    blk_cls = meta[0, :nblk]
    blk_ea = jnp.asarray(tab_a)[blk_cls]
    blk_eb = jnp.asarray(tab_b)[blk_cls]
    nused = meta[1, 0:1]
    cum_cls = meta[2, :CLS_PAD]
    nblk_cls = meta[3, :CLS_PAD]

    sorted_rows = _dispatch_call(cum_cls, nblk_cls, nused, dest2d, rows, nblk * RB)
    y_rows = _expert_call(blk_ea, blk_eb, nused, sorted_rows, w_gate_up[0].astype(MXU_DTYPE),
                          w_down[0].astype(MXU_DTYPE))

    ln2g, ln2b = row(ln2_g[0]), row(ln2_b[0])
    y_prompt = _combine_call(dest2d, y_rows, x1, mod, 0, 0, ln2g, ln2b, bp, sp)
    y_sample = _combine_call(dest2d, y_rows, x1, mod, bp, n_prompt // TD, ln2g, ln2b, bs, ss)
    return (y_prompt, y_sample)
```

```python
import functools

import numpy as np
import jax
import jax.numpy as jnp
from jax import lax
from jax.experimental import pallas as pl
from jax.experimental.pallas import tpu as pltpu

F32 = jnp.float32
I32 = jnp.int32
MXU_DTYPE = jnp.bfloat16

D_MODEL = 1024
A_WIDTH = 512
B_WIDTH = 512
CHUNK = 128
A_HEADS = 4
HEAD_DIM = A_WIDTH // A_HEADS
POOL_WINDOWS = (2, 4, 8, 16)
POOL_HALO = 8
GROUP_DIM = B_WIDTH // len(POOL_WINDOWS)
N_GROUPS = 4
EPG = 8
N_EXPERTS = N_GROUPS * EPG
D_EXPERT = 512
N_MOD = 6
LN_EPS = 1e-5
ALPHA = 2.0 ** 0.25

PAIRS = EPG * (EPG - 1) // 2
N_CLASSES = N_GROUPS * PAIRS
CLS_PAD = 128
N_ROUTE = N_GROUPS + N_EXPERTS
ROUTE_PAD = 40
U32 = jnp.uint32
LANES = 128
SUBLANES = 8
PACK_ROWS = D_MODEL // (2 * LANES)
Y_ROWS = D_MODEL // LANES
assert PACK_ROWS < SUBLANES and Y_ROWS == SUBLANES
HI_MASK = 0xFFFF0000

TS = 512
TD = 512
RB = 256
TI = 2048
IDX_SLOTS = 4
BL = 2048
SUB = 256
VMEM_LIMIT = 56 * 1024 * 1024

_NT = (((1,), (1,)), ((), ()))


def _layer_norm(x, g, b):
    mu = jnp.mean(x, axis=-1, keepdims=True)
    xc = x - mu
    var = jnp.mean(xc * xc, axis=-1, keepdims=True)
    return xc * lax.rsqrt(var + LN_EPS) * g + b


def _mdot(a, b):
    return jnp.dot(a, b, preferred_element_type=F32)


def _mod_kernel(c_ref, w_ref, b_ref, o_ref):
    a = jax.nn.silu(c_ref[...])
    o_ref[...] = jnp.dot(a, w_ref[...], precision=lax.Precision.HIGHEST,
                         preferred_element_type=F32) + b_ref[...]


def _modulation(c_all, w_ada, b_ada):
    nb, d = c_all.shape
    n = w_ada.shape[1]
    bn = 1536
    return pl.pallas_call(
        _mod_kernel,
        out_shape=jax.ShapeDtypeStruct((nb, n), F32),
        grid=(n // bn,),
        in_specs=[pl.BlockSpec((nb, d), lambda j: (0, 0)),
                  pl.BlockSpec((d, bn), lambda j: (0, j)),
                  pl.BlockSpec((1, bn), lambda j: (0, j))],
        out_specs=pl.BlockSpec((nb, bn), lambda j: (0, j)),
        compiler_params=pltpu.CompilerParams(dimension_semantics=("arbitrary",),
                                             vmem_limit_bytes=VMEM_LIMIT),
        name="adaln_mod",
    )(c_all, w_ada, b_ada)


def _mixer_kernel(xp_ref, xpp_ref, xpn_ref, xs_ref, xsp_ref, xsn_ref, mod_ref, win_ref, vng_ref, vnb_ref, ws_ref,
                  bst_ref, band_ref, wpool_ref, pscale_ref, wout_ref, ln1g_ref, ln1b_ref, wr_ref, rb_ref,
                  x1_ref, rows_ref, rid_ref, *, prompt_tiles, prompt_seq, sample_seq):
    ts = xp_ref.shape[1]
    i = pl.program_id(0)
    is_p = i < prompt_tiles
    seq_len = jnp.where(is_p, prompt_seq, sample_seq)
    ns = jnp.where(is_p, prompt_seq // ts, sample_seq // ts)
    s = jnp.where(is_p, i, i - prompt_tiles) % ns
    md = mod_ref[0]
    sh1, sc1, g1, sh2, sc2 = md[0:1], md[1:2], md[2:3], md[3:4], md[4:5]

    xt = jnp.where(is_p, xp_ref[0], xs_ref[0])
    h = xt * (1.0 + sc1) + sh1
    hp = jnp.where(s > 0, jnp.where(is_p, xpp_ref[0], xsp_ref[0]) * (1.0 + sc1) + sh1, 0.0)
    hn = jnp.where(s < ns - 1, jnp.where(is_p, xpn_ref[0], xsn_ref[0]) * (1.0 + sc1) + sh1, 0.0)
    hext = jnp.concatenate([h, hp, hn], axis=0).astype(MXU_DTYPE)
    zext = _mdot(hext, win_ref[...])
    z = zext[:ts]

    za = jax.nn.gelu(z[:, :2 * A_WIDTH])
    u = za[:, :A_WIDTH]
    v = _layer_norm(za[:, A_WIDTH:], vng_ref[...], vnb_ref[...]).astype(MXU_DTYPE)
    bst = bst_ref[...]
    row_blocks = []
    for c in range(ts // CHUNK):
        cols = []
        for hh in range(A_HEADS):
            vv = v[c * CHUNK:(c + 1) * CHUNK, hh * HEAD_DIM:(hh + 1) * HEAD_DIM]
            cols.append(_mdot(ws_ref[hh], vv) + bst[:, hh:hh + 1])
        row_blocks.append(jnp.concatenate(cols, axis=1))
    a_out = u * jnp.concatenate(row_blocks, axis=0)

    p = z[:, 2 * A_WIDTH:]
    pall = jnp.concatenate([zext[ts:ts + POOL_HALO, 2 * A_WIDTH:], p,
                            zext[ts + POOL_HALO:ts + 2 * POOL_HALO, 2 * A_WIDTH:]], axis=0)
    p_hi = pall.astype(MXU_DTYPE)
    p_lo = (pall - p_hi.astype(F32)).astype(MXU_DTYPE)
    pos = s * ts + lax.broadcasted_iota(I32, (ts, 1), 0)
    b_cols = []
    for g, w in enumerate(POOL_WINDOWS):
        wins = []
        for c in range(ts // CHUNK):
            r0 = c * CHUNK
            seg_hi = p_hi[r0:r0 + CHUNK + 2 * POOL_HALO, g * GROUP_DIM:(g + 1) * GROUP_DIM]
            seg_lo = p_lo[r0:r0 + CHUNK + 2 * POOL_HALO, g * GROUP_DIM:(g + 1) * GROUP_DIM]
            wins.append(_mdot(band_ref[g], seg_hi) + _mdot(band_ref[g], seg_lo))
        win = jnp.concatenate(wins, axis=0)
        cnt = (jnp.minimum(pos + w // 2, seq_len) - jnp.maximum(pos - w // 2, 0)).astype(F32)
        pooled = win / cnt - p[:, g * GROUP_DIM:(g + 1) * GROUP_DIM]
        b_cols.append(_mdot(pooled.astype(MXU_DTYPE), wpool_ref[g]))
    b_out = jnp.concatenate(b_cols, axis=1) * pscale_ref[...]

    mix_in = jnp.concatenate([a_out, b_out], axis=1).astype(MXU_DTYPE)
    mix = _mdot(mix_in, wout_ref[...])
    x1 = _layer_norm(ALPHA * xt + g1 * mix, ln1g_ref[...], ln1b_ref[...])
    x1_ref[...] = x1

    h2 = x1 * (1.0 + sc2) + sh2
    h2_hi = h2.astype(MXU_DTYPE)
    h2_lo = (h2 - h2_hi.astype(F32)).astype(MXU_DTYPE)
    l1 = lax.dot_general(wr_ref[...], h2_hi, _NT, preferred_element_type=F32)
    l2 = lax.dot_general(wr_ref[0:ROUTE_PAD], h2_lo, _NT, preferred_element_type=F32)
    lt = l1[:ROUTE_PAD] + l1[ROUTE_PAD:] + l2 + rb_ref[...]

    def row(r):
        return lt[r:r + 1, :]

    gl = [row(r) for r in range(N_GROUPS)]
    gmax = jnp.maximum(jnp.maximum(gl[0], gl[1]), jnp.maximum(gl[2], gl[3]))
    gidx = jnp.where(gl[0] == gmax, 0, jnp.where(gl[1] == gmax, 1, jnp.where(gl[2] == gmax, 2, 3)))
    gsum = (jnp.exp(gl[0] - gmax) + jnp.exp(gl[1] - gmax)) + (jnp.exp(gl[2] - gmax) + jnp.exp(gl[3] - gmax))
    gw = 1.0 / gsum
    ev = [jnp.where(gidx == 0, row(N_GROUPS + j),
                    jnp.where(gidx == 1, row(N_GROUPS + EPG + j),
                              jnp.where(gidx == 2, row(N_GROUPS + 2 * EPG + j), row(N_GROUPS + 3 * EPG + j))))
          for j in range(EPG)]

    def top1(vals):
        m = vals[0]
        for t in vals[1:]:
            m = jnp.maximum(m, t)
        idx = jnp.full(m.shape, EPG - 1, I32)
        for j in range(EPG - 2, -1, -1):
            idx = jnp.where(vals[j] == m, j, idx)
        return m, idx

    v1, j1 = top1(ev)
    v2, j2 = top1([jnp.where(j1 == j, -jnp.inf, ev[j]) for j in range(EPG)])
    t2 = jnp.exp(v2 - v1)
    den = 1.0 + t2
    w1 = (1.0 / den) * gw
    w2 = (t2 / den) * gw
    first = j1 < j2
    ea = jnp.minimum(j1, j2)
    eb = jnp.maximum(j1, j2)
    wa = jnp.where(first, w1, w2)
    wb = jnp.where(first, w2, w1)
    cls = gidx * PAIRS + jnp.right_shift(ea * (2 * EPG - 1 - ea), 1) + (eb - ea - 1)

    r8 = lax.broadcasted_iota(I32, (8, ts), 0)
    rid_ref[...] = jnp.where(r8 == 0, cls, jnp.where(r8 == 1, gidx * EPG + ea,
                                                      jnp.where(r8 == 2, gidx * EPG + eb, 0)))
    w8 = jnp.where(r8 == 0, wa, jnp.where(r8 == 1, wb, 0.0))
    w128 = jnp.concatenate([w8, jnp.zeros((LANES - 8, ts), F32)], axis=0)
    bits = lax.bitcast_convert_type(h2.astype(jnp.bfloat16).astype(F32), U32)
    half = D_MODEL // 2
    for k in range(PACK_ROWS):
        lo = jnp.right_shift(bits[:, k * LANES:(k + 1) * LANES], 16)
        hi = jnp.bitwise_and(bits[:, half + k * LANES:half + (k + 1) * LANES], jnp.uint32(HI_MASK))
        rows_ref[pl.ds(k, ts, stride=SUBLANES), :] = jnp.bitwise_or(lo, hi)
    rows_ref[pl.ds(PACK_ROWS, ts, stride=SUBLANES), :] = lax.bitcast_convert_type(w128.T, U32)
    for k in range(PACK_ROWS + 1, SUBLANES):
        rows_ref[pl.ds(k, ts, stride=SUBLANES), :] = jnp.zeros((ts, LANES), U32)


def _mixer_call(x_prompt, x_sample, mod, weights):
    bp, sp, d = x_prompt.shape
    bs, ss, _ = x_sample.shape
    nsp, nss = sp // TS, ss // TS
    ntp, nts = bp * nsp, bs * nss
    n_tok = bp * sp + bs * ss
    hb = TS // POOL_HALO

    def p_tile(i):
        t = jnp.minimum(i, ntp - 1)
        return t // nsp, t % nsp

    def s_tile(i):
        t = jnp.maximum(i - ntp, 0)
        return t // nss, t % nss

    def specs(tile_fn, seq):
        def cur(i):
            b, s = tile_fn(i)
            return (b, s, 0)

        def prev(i):
            b, s = tile_fn(i)
            return (b, jnp.maximum(s * hb - 1, 0), 0)

        def nxt(i):
            b, s = tile_fn(i)
            return (b, jnp.minimum((s + 1) * hb, seq // POOL_HALO - 1), 0)

        return [pl.BlockSpec((1, TS, d), cur), pl.BlockSpec((1, POOL_HALO, d), prev),
                pl.BlockSpec((1, POOL_HALO, d), nxt)]

    def mod_map(i):
        return (jnp.where(i < ntp, p_tile(i)[0], bp + s_tile(i)[0]), 0, 0)

    def const(w):
        return pl.BlockSpec(w.shape, lambda i, nd=w.ndim: (0,) * nd)

    in_specs = (specs(p_tile, sp) + specs(s_tile, ss) + [pl.BlockSpec((1, N_MOD, d), mod_map)]
                + [const(w) for w in weights])
    out_shape = (jax.ShapeDtypeStruct((n_tok, d), F32),
                 jax.ShapeDtypeStruct((n_tok * SUBLANES, LANES), U32),
                 jax.ShapeDtypeStruct((8, n_tok), I32))
    out_specs = (pl.BlockSpec((TS, d), lambda i: (i, 0)),
                 pl.BlockSpec((TS * SUBLANES, LANES), lambda i: (i, 0)),
                 pl.BlockSpec((8, TS), lambda i: (0, i)))
    return pl.pallas_call(
        functools.partial(_mixer_kernel, prompt_tiles=ntp, prompt_seq=sp, sample_seq=ss),
        out_shape=out_shape,
        grid=(ntp + nts,),
        in_specs=in_specs,
        out_specs=out_specs,
        compiler_params=pltpu.CompilerParams(dimension_semantics=("arbitrary",),
                                             vmem_limit_bytes=VMEM_LIMIT),
        name="mixer_ln1_route",
    )(x_prompt, x_prompt, x_prompt, x_sample, x_sample, x_sample, mod, *weights)


def _rank_kernel(rid_ref, u_ref, tri_ref, dest_ref, meta_ref, cnt_ref, base_ref, *, nb_pad):
    phase = pl.program_id(0)
    j = pl.program_id(1)
    cls_iota = lax.broadcasted_iota(I32, (CLS_PAD, SUB), 0)

    @pl.when((phase == 0) & (j == 0))
    def _():
        cnt_ref[...] = jnp.zeros_like(cnt_ref)

    @pl.when(phase == 0)
    def _():
        acc = cnt_ref[...]
        for sb in range(BL // SUB):
            ids = rid_ref[0:1, sb * SUB:(sb + 1) * SUB]
            acc = acc + jnp.sum((cls_iota == ids).astype(F32), axis=1, keepdims=True)
        cnt_ref[...] = acc

    @pl.when((phase == 1) & (j == 0))
    def _():
        cnt = jnp.broadcast_to(cnt_ref[...], (CLS_PAD, CLS_PAD))
        nblk = jnp.floor((cnt + (RB - 1)) * (1.0 / RB))
        cum = jnp.dot(tri_ref[...], nblk, precision=lax.Precision.HIGHEST, preferred_element_type=F32)
        base_ref[...] = (cum[:, 0:1] - nblk[:, 0:1]) * RB
        blk = lax.broadcasted_iota(I32, (CLS_PAD, nb_pad), 1).astype(F32)
        bcls = jnp.sum((cum[:, 0:1] <= blk).astype(F32), axis=0, keepdims=True)
        bcls = jnp.minimum(bcls, N_CLASSES - 1).astype(I32)
        nused = jnp.broadcast_to(cum[CLS_PAD - 1:CLS_PAD, 0:1], (1, nb_pad)).astype(I32)
        r8 = lax.broadcasted_iota(I32, (8, nb_pad), 0)
        meta_ref[...] = jnp.where(r8 == 0, bcls, jnp.where(r8 == 1, nused, 0))

    @pl.when(phase == 1)
    def _():
        base = base_ref[...]
        for sb in range(BL // SUB):
            ids = rid_ref[0:1, sb * SUB:(sb + 1) * SUB]
            hit = cls_iota == ids
            incl = _mdot(hit.astype(MXU_DTYPE), u_ref[...])
            slot = jnp.sum(jnp.where(hit, base + incl - 1.0, 0.0), axis=0, keepdims=True)
            dest_ref[0:1, sb * SUB:(sb + 1) * SUB] = slot.astype(I32)
            base = base + incl[:, SUB - 1:SUB]
        base_ref[...] = base


def _rank_call(rid, u_mat, tri, nb_pad):
    n_tok = rid.shape[1]
    nj = n_tok // BL
    return pl.pallas_call(
        functools.partial(_rank_kernel, nb_pad=nb_pad),
        out_shape=(jax.ShapeDtypeStruct((1, n_tok), I32), jax.ShapeDtypeStruct((8, nb_pad), I32)),
        grid=(2, nj),
        in_specs=[pl.BlockSpec((8, BL), lambda p, j: (0, j)),
                  pl.BlockSpec(u_mat.shape, lambda p, j: (0, 0)),
                  pl.BlockSpec(tri.shape, lambda p, j: (0, 0))],
        out_specs=(pl.BlockSpec((1, BL), lambda p, j: (0, p * j)),
                   pl.BlockSpec((8, nb_pad), lambda p, j: (0, 0))),
        scratch_shapes=[pltpu.VMEM((CLS_PAD, 1), F32), pltpu.VMEM((CLS_PAD, 1), F32)],
        compiler_params=pltpu.CompilerParams(dimension_semantics=("arbitrary", "arbitrary"),
                                             vmem_limit_bytes=VMEM_LIMIT),
        name="rank_tokens",
    )(rid, u_mat, tri)


def _invert_kernel(dest_hbm, tok_hbm, table, idx_smem, sem_idx, sem_out, *, n_tok):
    i = pl.program_id(0)
    n_steps = pl.num_programs(0)
    slot = i % 2

    def idx_copy(step, s):
        return pltpu.make_async_copy(dest_hbm.at[step], idx_smem.at[s], sem_idx.at[s])

    @pl.when(i == 0)
    def _():
        idx_copy(0, 0).start()

        def init(r, carry):
            table[r] = n_tok + r % (2 * RB)
            return carry

        lax.fori_loop(0, table.shape[0], init, 0, unroll=8)

    idx_copy(i, slot).wait()

    @pl.when(i + 1 < n_steps)
    def _():
        idx_copy(i + 1, 1 - slot).start()

    base = i * TI

    def place(t, carry):
        table[idx_smem[slot, t]] = base + t
        return carry

    lax.fori_loop(0, TI, place, 0, unroll=8)

    @pl.when(i + 1 == n_steps)
    def _():
        out = pltpu.make_async_copy(table, tok_hbm, sem_out)
        out.start()
        out.wait()


def _invert_call(dest, n_rows_pad):
    n_tok = dest.shape[1]
    return pl.pallas_call(
        functools.partial(_invert_kernel, n_tok=n_tok),
        out_shape=jax.ShapeDtypeStruct((n_rows_pad,), I32),
        grid=(n_tok // TI,),
        in_specs=[pl.BlockSpec(memory_space=pl.ANY)],
        out_specs=pl.BlockSpec(memory_space=pl.ANY),
        scratch_shapes=[pltpu.SMEM((n_rows_pad,), I32), pltpu.SMEM((2, TI), I32),
                        pltpu.SemaphoreType.DMA((2,)), pltpu.SemaphoreType.DMA],
        compiler_params=pltpu.CompilerParams(dimension_semantics=("arbitrary",)),
        name="invert_rows",
    )(dest.reshape(n_tok // TI, TI))


def _expert_kernel(ea_ref, eb_ref, nused_ref, tok_hbm, rows_hbm, wgu_a_ref, wd_a_ref, wgu_b_ref, wd_b_ref, y_hbm,
                   idx_smem, xbuf, ybuf, sem_idx, sem_g, sem_s, *, n_tok):
    b = pl.program_id(0)
    nused = nused_ref[0]
    slot = b % 2
    blk_rows = RB * SUBLANES

    def idx_copy(blk, s):
        return pltpu.make_async_copy(tok_hbm.at[blk], idx_smem.at[s], sem_idx.at[s])

    def gathered(s):
        return pltpu.make_async_copy(rows_hbm.at[pl.ds(0, blk_rows)], xbuf.at[s], sem_g.at[s])

    def scattered(s):
        return pltpu.make_async_copy(ybuf.at[s], y_hbm.at[pl.ds(0, blk_rows)], sem_s.at[s])

    def gather_row(r, isl, s, priority):
        tok = jnp.minimum(idx_smem[isl, r], n_tok - 1)
        src = pl.multiple_of(tok * SUBLANES, SUBLANES)
        pltpu.make_async_copy(rows_hbm.at[pl.ds(src, SUBLANES)], xbuf.at[s, pl.ds(r * SUBLANES, SUBLANES)],
                              sem_g.at[s]).start(priority=priority)

    def scatter_row(r, isl, s, priority):
        dst = pl.multiple_of(idx_smem[isl, r] * Y_ROWS, Y_ROWS)
        pltpu.make_async_copy(ybuf.at[s, pl.ds(r * Y_ROWS, Y_ROWS)], y_hbm.at[pl.ds(dst, Y_ROWS)],
                              sem_s.at[s]).start(priority=priority)

    @pl.when(b == 0)
    def _():
        ybuf[...] = jnp.zeros_like(ybuf)
        for s in range(2):
            start = (n_tok + s * RB) * Y_ROWS
            dump = pltpu.make_async_copy(ybuf.at[s], y_hbm.at[pl.ds(start, blk_rows)], sem_s.at[s])
            dump.start()
            dump.wait()

        def clear(r, carry):
            idx_smem[IDX_SLOTS - 1, r] = n_tok + RB + r
            return carry

        lax.fori_loop(0, RB, clear, 0, unroll=8)
        idx_copy(0, 0).start()
        idx_copy(1, 1).start()
        idx_copy(0, 0).wait()

        def first(r, carry):
            gather_row(r, 0, 0, 0)
            return carry

        lax.fori_loop(0, RB, first, 0, unroll=8)

    @pl.when((b >= 1) & (b < nused))
    def _():
        scattered(slot).wait()

    @pl.when(b < nused)
    def _():
        nxt = (b + 1) % IDX_SLOTS
        prv = (b + IDX_SLOTS - 1) % IDX_SLOTS
        idx_copy(b + 1, nxt).wait()
        idx_copy(b + 2, (b + 2) % IDX_SLOTS).start()
        gathered(slot).wait()

        def tile_row(k):
            return xbuf[slot, pl.ds(k, RB, stride=SUBLANES), :]

        words = [tile_row(k) for k in range(PACK_ROWS)]
        lo = [lax.bitcast_convert_type(jnp.left_shift(w, 16), F32) for w in words]
        hi = [lax.bitcast_convert_type(jnp.bitwise_and(w, jnp.uint32(HI_MASK)), F32) for w in words]
        x = jnp.concatenate(lo + hi, axis=1).astype(MXU_DTYPE)
        gate = lax.bitcast_convert_type(tile_row(PACK_ROWS), F32)

        for r in range(RB):
            gather_row(r, nxt, 1 - slot, r % 2)
            scatter_row(r, prv, 1 - slot, r % 2)

        def expert(wgu_ref, wd_ref):
            gu = _mdot(x, wgu_ref[0])
            act = jax.nn.silu(gu[:, :D_EXPERT]) * gu[:, D_EXPERT:]
            return _mdot(act.astype(MXU_DTYPE), wd_ref[0])

        y = expert(wgu_a_ref, wd_a_ref) * gate[:, 0:1] + expert(wgu_b_ref, wd_b_ref) * gate[:, 1:2]
        for k in range(Y_ROWS):
            ybuf[slot, pl.ds(k, RB, stride=SUBLANES), :] = y[:, k * LANES:(k + 1) * LANES]

    @pl.when(b == nused - 1)
    def _():
        cur = b % IDX_SLOTS

        def last(r, carry):
            scatter_row(r, cur, slot, 0)
            return carry

        lax.fori_loop(0, RB, last, 0, unroll=8)
        idx_copy(b + 2, (b + 2) % IDX_SLOTS).wait()
        gathered(1 - slot).wait()
        scattered(1 - slot).wait()
        scattered(slot).wait()


def _expert_call(blk_ea, blk_eb, nused, tok2d, rows, w_gate_up, w_down, n_tok):
    nblk = blk_ea.shape[0]

    def live(b, nu):
        return jnp.maximum(jnp.minimum(b, nu[0] - 1), 0)

    grid_spec = pltpu.PrefetchScalarGridSpec(
        num_scalar_prefetch=3,
        grid=(nblk,),
        in_specs=[pl.BlockSpec(memory_space=pl.ANY), pl.BlockSpec(memory_space=pl.ANY),
                  pl.BlockSpec((1,) + w_gate_up.shape[1:], lambda b, ea, eb, nu: (ea[live(b, nu)], 0, 0)),
                  pl.BlockSpec((1,) + w_down.shape[1:], lambda b, ea, eb, nu: (ea[live(b, nu)], 0, 0)),
                  pl.BlockSpec((1,) + w_gate_up.shape[1:], lambda b, ea, eb, nu: (eb[live(b, nu)], 0, 0)),
                  pl.BlockSpec((1,) + w_down.shape[1:], lambda b, ea, eb, nu: (eb[live(b, nu)], 0, 0))],
        out_specs=pl.BlockSpec(memory_space=pl.ANY),
        scratch_shapes=[pltpu.SMEM((IDX_SLOTS, RB), I32), pltpu.VMEM((2, RB * SUBLANES, LANES), U32),
                        pltpu.VMEM((2, RB * Y_ROWS, LANES), F32), pltpu.SemaphoreType.DMA((IDX_SLOTS,)),
                        pltpu.SemaphoreType.DMA((2,)), pltpu.SemaphoreType.DMA((2,))],
    )
    return pl.pallas_call(
        functools.partial(_expert_kernel, n_tok=n_tok),
        out_shape=jax.ShapeDtypeStruct(((n_tok + 2 * RB) * Y_ROWS, LANES), F32),
        grid_spec=grid_spec,
        compiler_params=pltpu.CompilerParams(dimension_semantics=("arbitrary",),
                                             vmem_limit_bytes=VMEM_LIMIT),
        name="pair_experts",
    )(blk_ea, blk_eb, nused, tok2d, rows, w_gate_up, w_down, w_gate_up, w_down)


def _final_kernel(y_ref, x1_ref, mod_ref, g_ref, b_ref, o_ref):
    y = jnp.concatenate([y_ref[pl.ds(k, TD, stride=Y_ROWS), :] for k in range(Y_ROWS)], axis=1)
    g2 = mod_ref[0][5:6]
    o_ref[0] = _layer_norm(ALPHA * x1_ref[...] + g2 * y, g_ref[...], b_ref[...])


def _final_call(y_tok, x1, mod, mod_off, tile_off, ln2g, ln2b, bsz, seq):
    d = x1.shape[1]
    ns = seq // TD
    return pl.pallas_call(
        _final_kernel,
        out_shape=jax.ShapeDtypeStruct((bsz, seq, d), F32),
        grid=(bsz, ns),
        in_specs=[pl.BlockSpec((TD * Y_ROWS, LANES), lambda b, s: (tile_off + b * ns + s, 0)),
                  pl.BlockSpec((TD, d), lambda b, s: (tile_off + b * ns + s, 0)),
                  pl.BlockSpec((1, N_MOD, d), lambda b, s: (mod_off + b, 0, 0)),
                  pl.BlockSpec((1, d), lambda b, s: (0, 0)),
                  pl.BlockSpec((1, d), lambda b, s: (0, 0))],
        out_specs=pl.BlockSpec((1, TD, d), lambda b, s: (b, s, 0)),
        compiler_params=pltpu.CompilerParams(dimension_semantics=("arbitrary", "arbitrary"),
                                             vmem_limit_bytes=VMEM_LIMIT),
        name="residual_ln2",
    )(y_tok, x1, mod, ln2g, ln2b)


def _band_matrices():
    rows = np.arange(CHUNK)[:, None]
    cols = np.arange(CHUNK + 2 * POOL_HALO)[None, :] - POOL_HALO
    return np.stack([((cols >= rows - w // 2) & (cols < rows + w // 2)) for w in POOL_WINDOWS]).astype(np.float32)


def _class_tables():
    ea, eb = [], []
    for g in range(N_GROUPS):
        for a in range(EPG):
            for b in range(a + 1, EPG):
                ea.append(g * EPG + a)
                eb.append(g * EPG + b)
    return np.asarray(ea, np.int32), np.asarray(eb, np.int32)


def _split_hi_lo(w):
    hi = w.astype(MXU_DTYPE)
    lo = (w - hi.astype(F32)).astype(MXU_DTYPE)
    return hi, lo


def kernel(x_prompt, x_sample, c_prompt, c_sample, w_ada, b_ada, w_in, v_norm_g, v_norm_b, w_spatial, b_spatial,
           w_pool, pool_scale, w_out, ln1_g, ln1_b, w_route_group, b_route_group, w_route_expert, b_route_expert,
           w_gate_up, w_down, ln2_g, ln2_b):
    assert w_ada.shape[0] == 1, "single-layer kernel"
    bp, sp, d = x_prompt.shape
    bs, ss, _ = x_sample.shape
    assert d == D_MODEL and sp % TS == 0 and ss % TS == 0
    n_prompt, n_sample = bp * sp, bs * ss
    n_tok = n_prompt + n_sample
    assert n_tok % BL == 0 and n_prompt % TD == 0 and n_tok % TI == 0

    c_all = jnp.concatenate([c_prompt, c_sample], axis=0)
    mod = _modulation(c_all, w_ada[0], b_ada).reshape(bp + bs, N_MOD, d)

    wr = jnp.concatenate([w_route_group[0], w_route_expert[0]], axis=1).T
    wr = jnp.pad(wr, ((0, ROUTE_PAD - N_ROUTE), (0, 0)))
    wr_hi, wr_lo = _split_hi_lo(wr)
    rbias = jnp.concatenate([b_route_group[0], b_route_expert[0].reshape(-1),
                             jnp.zeros((ROUTE_PAD - N_ROUTE,), F32)]).reshape(ROUTE_PAD, 1)
    row = lambda a: a.reshape(1, -1)
    weights = (w_in[0].astype(MXU_DTYPE), row(v_norm_g[0]), row(v_norm_b[0]), w_spatial[0].astype(MXU_DTYPE),
               b_spatial[0].T, jnp.asarray(_band_matrices(), MXU_DTYPE), w_pool[0].astype(MXU_DTYPE),
               row(pool_scale[0]), w_out[0].astype(MXU_DTYPE), row(ln1_g[0]), row(ln1_b[0]),
               jnp.concatenate([wr_hi, wr_lo], axis=0), rbias)

    x1, rows, rid = _mixer_call(x_prompt, x_sample, mod, weights)

    nblk = (n_tok + N_CLASSES * (RB - 1)) // RB
    nb_pad = -(-nblk // 128) * 128
    u_mat = jnp.asarray(np.triu(np.ones((SUB, SUB), np.float32)), MXU_DTYPE)
    tri = jnp.asarray(np.tril(np.ones((CLS_PAD, CLS_PAD), np.float32)))
    dest, meta = _rank_call(rid, u_mat, tri, nb_pad)
    tab_a, tab_b = _class_tables()
    blk_cls = meta[0, :nblk]
    blk_ea = jnp.asarray(tab_a)[blk_cls]
    blk_eb = jnp.asarray(tab_b)[blk_cls]
    nused = meta[1, 0:1]

    tok_rows = (nblk + 2) * RB
    tok = _invert_call(dest, -(-tok_rows // 1024) * 1024)
    tok2d = tok[:tok_rows].reshape(nblk + 2, RB)
    y_tok = _expert_call(blk_ea, blk_eb, nused, tok2d, rows, w_gate_up[0].astype(MXU_DTYPE),
                         w_down[0].astype(MXU_DTYPE), n_tok)

    ln2g, ln2b = row(ln2_g[0]), row(ln2_b[0])
    y_prompt = _final_call(y_tok, x1, mod, 0, 0, ln2g, ln2b, bp, sp)
    y_sample = _final_call(y_tok, x1, mod, bp, n_prompt // TD, ln2g, ln2b, bs, ss)
    return (y_prompt, y_sample)
```

```python
import functools

import numpy as np
import jax
import jax.numpy as jnp
from jax import lax
from jax.experimental import pallas as pl
from jax.experimental.pallas import tpu as pltpu

F32 = jnp.float32
I32 = jnp.int32
MXU_DTYPE = jnp.bfloat16

D_MODEL = 1024
A_WIDTH = 512
B_WIDTH = 512
CHUNK = 128
A_HEADS = 4
HEAD_DIM = A_WIDTH // A_HEADS
POOL_WINDOWS = (2, 4, 8, 16)
POOL_HALO = 8
GROUP_DIM = B_WIDTH // len(POOL_WINDOWS)
N_GROUPS = 4
EPG = 8
N_EXPERTS = N_GROUPS * EPG
D_EXPERT = 512
N_MOD = 6
LN_EPS = 1e-5
ALPHA = 2.0 ** 0.25

PAIRS = EPG * (EPG - 1) // 2
N_CLASSES = N_GROUPS * PAIRS
CLS_PAD = 128
N_ROUTE = N_GROUPS + N_EXPERTS
ROUTE_PAD = 40
U32 = jnp.uint32
LANES = 128
SUBLANES = 8
PACK_ROWS = D_MODEL // (2 * LANES)
Y_ROWS = D_MODEL // LANES
assert PACK_ROWS < SUBLANES and Y_ROWS == SUBLANES
HI_MASK = 0xFFFF0000

TS = 512
TD = 512
RB = 256
TI = 2048
IDX_SLOTS = 4
BL = 2048
SUB = 256
VMEM_LIMIT = 56 * 1024 * 1024

_NT = (((1,), (1,)), ((), ()))


def _layer_norm(x, g, b):
    mu = jnp.mean(x, axis=-1, keepdims=True)
    xc = x - mu
    var = jnp.mean(xc * xc, axis=-1, keepdims=True)
    return xc * lax.rsqrt(var + LN_EPS) * g + b


def _mdot(a, b):
    return jnp.dot(a, b, preferred_element_type=F32)


def _mod_kernel(c_ref, w_ref, b_ref, o_ref):
    a = jax.nn.silu(c_ref[...])
    o_ref[...] = jnp.dot(a, w_ref[...], precision=lax.Precision.HIGHEST,
                         preferred_element_type=F32) + b_ref[...]


def _modulation(c_all, w_ada, b_ada):
    nb, d = c_all.shape
    n = w_ada.shape[1]
    bn = 1536
    return pl.pallas_call(
        _mod_kernel,
        out_shape=jax.ShapeDtypeStruct((nb, n), F32),
        grid=(n // bn,),
        in_specs=[pl.BlockSpec((nb, d), lambda j: (0, 0)),
                  pl.BlockSpec((d, bn), lambda j: (0, j)),
                  pl.BlockSpec((1, bn), lambda j: (0, j))],
        out_specs=pl.BlockSpec((nb, bn), lambda j: (0, j)),
        compiler_params=pltpu.CompilerParams(dimension_semantics=("arbitrary",),
                                             vmem_limit_bytes=VMEM_LIMIT),
        name="adaln_mod",
    )(c_all, w_ada, b_ada)


def _mixer_kernel(xp_ref, xpp_ref, xpn_ref, xs_ref, xsp_ref, xsn_ref, mod_ref, win_ref, vng_ref, vnb_ref, ws_ref,
                  bst_ref, band_ref, wpool_ref, pscale_ref, wout_ref, ln1g_ref, ln1b_ref, wr_ref, rb_ref,
                  x1_ref, rows_ref, rid_ref, *, prompt_tiles, prompt_seq, sample_seq):
    ts = xp_ref.shape[1]
    i = pl.program_id(0)
    is_p = i < prompt_tiles
    seq_len = jnp.where(is_p, prompt_seq, sample_seq)
    ns = jnp.where(is_p, prompt_seq // ts, sample_seq // ts)
    s = jnp.where(is_p, i, i - prompt_tiles) % ns
    md = mod_ref[0]
    sh1, sc1, g1, sh2, sc2 = md[0:1], md[1:2], md[2:3], md[3:4], md[4:5]

    xt = jnp.where(is_p, xp_ref[0], xs_ref[0])
    h = xt * (1.0 + sc1) + sh1
    hp = jnp.where(s > 0, jnp.where(is_p, xpp_ref[0], xsp_ref[0]) * (1.0 + sc1) + sh1, 0.0)
    hn = jnp.where(s < ns - 1, jnp.where(is_p, xpn_ref[0], xsn_ref[0]) * (1.0 + sc1) + sh1, 0.0)
    hext = jnp.concatenate([h, hp, hn], axis=0).astype(MXU_DTYPE)
    zext = _mdot(hext, win_ref[...])
    z = zext[:ts]

    za = jax.nn.gelu(z[:, :2 * A_WIDTH])
    u = za[:, :A_WIDTH]
    v = _layer_norm(za[:, A_WIDTH:], vng_ref[...], vnb_ref[...]).astype(MXU_DTYPE)
    bst = bst_ref[...]
    row_blocks = []
    for c in range(ts // CHUNK):
        cols = []
        for hh in range(A_HEADS):
            vv = v[c * CHUNK:(c + 1) * CHUNK, hh * HEAD_DIM:(hh + 1) * HEAD_DIM]
            cols.append(_mdot(ws_ref[hh], vv) + bst[:, hh:hh + 1])
        row_blocks.append(jnp.concatenate(cols, axis=1))
    a_out = u * jnp.concatenate(row_blocks, axis=0)

    p = z[:, 2 * A_WIDTH:]
    pall = jnp.concatenate([zext[ts:ts + POOL_HALO, 2 * A_WIDTH:], p,
                            zext[ts + POOL_HALO:ts + 2 * POOL_HALO, 2 * A_WIDTH:]], axis=0)
    p_hi = pall.astype(MXU_DTYPE)
    p_lo = (pall - p_hi.astype(F32)).astype(MXU_DTYPE)
    pos = s * ts + lax.broadcasted_iota(I32, (ts, 1), 0)
    b_cols = []
    for g, w in enumerate(POOL_WINDOWS):
        wins = []
        for c in range(ts // CHUNK):
            r0 = c * CHUNK
            seg_hi = p_hi[r0:r0 + CHUNK + 2 * POOL_HALO, g * GROUP_DIM:(g + 1) * GROUP_DIM]
            seg_lo = p_lo[r0:r0 + CHUNK + 2 * POOL_HALO, g * GROUP_DIM:(g + 1) * GROUP_DIM]
            wins.append(_mdot(band_ref[g], seg_hi) + _mdot(band_ref[g], seg_lo))
        win = jnp.concatenate(wins, axis=0)
        cnt = (jnp.minimum(pos + w // 2, seq_len) - jnp.maximum(pos - w // 2, 0)).astype(F32)
        pooled = win / cnt - p[:, g * GROUP_DIM:(g + 1) * GROUP_DIM]
        b_cols.append(_mdot(pooled.astype(MXU_DTYPE), wpool_ref[g]))
    b_out = jnp.concatenate(b_cols, axis=1) * pscale_ref[...]

    mix_in = jnp.concatenate([a_out, b_out], axis=1).astype(MXU_DTYPE)
    mix = _mdot(mix_in, wout_ref[...])
    x1 = _layer_norm(ALPHA * xt + g1 * mix, ln1g_ref[...], ln1b_ref[...])
    x1_ref[...] = x1

    h2 = x1 * (1.0 + sc2) + sh2
    h2_hi = h2.astype(MXU_DTYPE)
    h2_lo = (h2 - h2_hi.astype(F32)).astype(MXU_DTYPE)
    l1 = lax.dot_general(wr_ref[...], h2_hi, _NT, preferred_element_type=F32)
    l2 = lax.dot_general(wr_ref[0:ROUTE_PAD], h2_lo, _NT, preferred_element_type=F32)
    lt = l1[:ROUTE_PAD] + l1[ROUTE_PAD:] + l2 + rb_ref[...]

    def row(r):
        return lt[r:r + 1, :]

    gl = [row(r) for r in range(N_GROUPS)]
    gmax = jnp.maximum(jnp.maximum(gl[0], gl[1]), jnp.maximum(gl[2], gl[3]))
    gidx = jnp.where(gl[0] == gmax, 0, jnp.where(gl[1] == gmax, 1, jnp.where(gl[2] == gmax, 2, 3)))
    gsum = (jnp.exp(gl[0] - gmax) + jnp.exp(gl[1] - gmax)) + (jnp.exp(gl[2] - gmax) + jnp.exp(gl[3] - gmax))
    gw = 1.0 / gsum
    ev = [jnp.where(gidx == 0, row(N_GROUPS + j),
                    jnp.where(gidx == 1, row(N_GROUPS + EPG + j),
                              jnp.where(gidx == 2, row(N_GROUPS + 2 * EPG + j), row(N_GROUPS + 3 * EPG + j))))
          for j in range(EPG)]

    def top1(vals):
        m = vals[0]
        for t in vals[1:]:
            m = jnp.maximum(m, t)
        idx = jnp.full(m.shape, EPG - 1, I32)
        for j in range(EPG - 2, -1, -1):
            idx = jnp.where(vals[j] == m, j, idx)
        return m, idx

    v1, j1 = top1(ev)
    v2, j2 = top1([jnp.where(j1 == j, -jnp.inf, ev[j]) for j in range(EPG)])
    t2 = jnp.exp(v2 - v1)
    den = 1.0 + t2
    w1 = (1.0 / den) * gw
    w2 = (t2 / den) * gw
    first = j1 < j2
    ea = jnp.minimum(j1, j2)
    eb = jnp.maximum(j1, j2)
    wa = jnp.where(first, w1, w2)
    wb = jnp.where(first, w2, w1)
    cls = gidx * PAIRS + jnp.right_shift(ea * (2 * EPG - 1 - ea), 1) + (eb - ea - 1)

    r8 = lax.broadcasted_iota(I32, (8, ts), 0)
    rid_ref[...] = jnp.where(r8 == 0, cls, jnp.where(r8 == 1, gidx * EPG + ea,
                                                      jnp.where(r8 == 2, gidx * EPG + eb, 0)))
    w8 = jnp.where(r8 == 0, wa, jnp.where(r8 == 1, wb, 0.0))
    w128 = jnp.concatenate([w8, jnp.zeros((LANES - 8, ts), F32)], axis=0)
    bits = lax.bitcast_convert_type(h2.astype(jnp.bfloat16).astype(F32), U32)
    half = D_MODEL // 2
    for k in range(PACK_ROWS):
        lo = jnp.right_shift(bits[:, k * LANES:(k + 1) * LANES], 16)
        hi = jnp.bitwise_and(bits[:, half + k * LANES:half + (k + 1) * LANES], jnp.uint32(HI_MASK))
        rows_ref[pl.ds(k, ts, stride=SUBLANES), :] = jnp.bitwise_or(lo, hi)
    rows_ref[pl.ds(PACK_ROWS, ts, stride=SUBLANES), :] = lax.bitcast_convert_type(w128.T, U32)
    for k in range(PACK_ROWS + 1, SUBLANES):
        rows_ref[pl.ds(k, ts, stride=SUBLANES), :] = jnp.zeros((ts, LANES), U32)


def _mixer_call(x_prompt, x_sample, mod, weights):
    bp, sp, d = x_prompt.shape
    bs, ss, _ = x_sample.shape
    nsp, nss = sp // TS, ss // TS
    ntp, nts = bp * nsp, bs * nss
    n_tok = bp * sp + bs * ss
    hb = TS // POOL_HALO

    def p_tile(i):
        t = jnp.minimum(i, ntp - 1)
        return t // nsp, t % nsp

    def s_tile(i):
        t = jnp.maximum(i - ntp, 0)
        return t // nss, t % nss

    def specs(tile_fn, seq):
        def cur(i):
            b, s = tile_fn(i)
            return (b, s, 0)

        def prev(i):
            b, s = tile_fn(i)
            return (b, jnp.maximum(s * hb - 1, 0), 0)

        def nxt(i):
            b, s = tile_fn(i)
            return (b, jnp.minimum((s + 1) * hb, seq // POOL_HALO - 1), 0)

        return [pl.BlockSpec((1, TS, d), cur), pl.BlockSpec((1, POOL_HALO, d), prev),
                pl.BlockSpec((1, POOL_HALO, d), nxt)]

    def mod_map(i):
        return (jnp.where(i < ntp, p_tile(i)[0], bp + s_tile(i)[0]), 0, 0)

    def const(w):
        return pl.BlockSpec(w.shape, lambda i, nd=w.ndim: (0,) * nd)

    in_specs = (specs(p_tile, sp) + specs(s_tile, ss) + [pl.BlockSpec((1, N_MOD, d), mod_map)]
                + [const(w) for w in weights])
    out_shape = (jax.ShapeDtypeStruct((n_tok, d), F32),
                 jax.ShapeDtypeStruct((n_tok * SUBLANES, LANES), U32),
                 jax.ShapeDtypeStruct((8, n_tok), I32))
    out_specs = (pl.BlockSpec((TS, d), lambda i: (i, 0)),
                 pl.BlockSpec((TS * SUBLANES, LANES), lambda i: (i, 0)),
                 pl.BlockSpec((8, TS), lambda i: (0, i)))
    return pl.pallas_call(
        functools.partial(_mixer_kernel, prompt_tiles=ntp, prompt_seq=sp, sample_seq=ss),
        out_shape=out_shape,
        grid=(ntp + nts,),
        in_specs=in_specs,
        out_specs=out_specs,
        compiler_params=pltpu.CompilerParams(dimension_semantics=("arbitrary",),
                                             vmem_limit_bytes=VMEM_LIMIT),
        name="mixer_ln1_route",
    )(x_prompt, x_prompt, x_prompt, x_sample, x_sample, x_sample, mod, *weights)


def _rank_kernel(rid_ref, u_ref, tri_ref, dest_ref, meta_ref, cnt_ref, base_ref, *, nb_pad):
    phase = pl.program_id(0)
    j = pl.program_id(1)
    cls_iota = lax.broadcasted_iota(I32, (CLS_PAD, SUB), 0)

    @pl.when((phase == 0) & (j == 0))
    def _():
        cnt_ref[...] = jnp.zeros_like(cnt_ref)

    @pl.when(phase == 0)
    def _():
        acc = cnt_ref[...]
        for sb in range(BL // SUB):
            ids = rid_ref[0:1, sb * SUB:(sb + 1) * SUB]
            acc = acc + jnp.sum((cls_iota == ids).astype(F32), axis=1, keepdims=True)
        cnt_ref[...] = acc

    @pl.when((phase == 1) & (j == 0))
    def _():
        cnt = jnp.broadcast_to(cnt_ref[...], (CLS_PAD, CLS_PAD))
        nblk = jnp.floor((cnt + (RB - 1)) * (1.0 / RB))
        cum = jnp.dot(tri_ref[...], nblk, precision=lax.Precision.HIGHEST, preferred_element_type=F32)
        base_ref[...] = (cum[:, 0:1] - nblk[:, 0:1]) * RB
        blk = lax.broadcasted_iota(I32, (CLS_PAD, nb_pad), 1).astype(F32)
        bcls = jnp.sum((cum[:, 0:1] <= blk).astype(F32), axis=0, keepdims=True)
        bcls = jnp.minimum(bcls, N_CLASSES - 1).astype(I32)
        nused = jnp.broadcast_to(cum[CLS_PAD - 1:CLS_PAD, 0:1], (1, nb_pad)).astype(I32)
        r8 = lax.broadcasted_iota(I32, (8, nb_pad), 0)
        meta_ref[...] = jnp.where(r8 == 0, bcls, jnp.where(r8 == 1, nused, 0))

    @pl.when(phase == 1)
    def _():
        base = base_ref[...]
        for sb in range(BL // SUB):
            ids = rid_ref[0:1, sb * SUB:(sb + 1) * SUB]
            hit = cls_iota == ids
            incl = _mdot(hit.astype(MXU_DTYPE), u_ref[...])
            slot = jnp.sum(jnp.where(hit, base + incl - 1.0, 0.0), axis=0, keepdims=True)
            dest_ref[0:1, sb * SUB:(sb + 1) * SUB] = slot.astype(I32)
            base = base + incl[:, SUB - 1:SUB]
        base_ref[...] = base


def _rank_call(rid, u_mat, tri, nb_pad):
    n_tok = rid.shape[1]
    nj = n_tok // BL
    return pl.pallas_call(
        functools.partial(_rank_kernel, nb_pad=nb_pad),
        out_shape=(jax.ShapeDtypeStruct((1, n_tok), I32), jax.ShapeDtypeStruct((8, nb_pad), I32)),
        grid=(2, nj),
        in_specs=[pl.BlockSpec((8, BL), lambda p, j: (0, j)),
                  pl.BlockSpec(u_mat.shape, lambda p, j: (0, 0)),
                  pl.BlockSpec(tri.shape, lambda p, j: (0, 0))],
        out_specs=(pl.BlockSpec((1, BL), lambda p, j: (0, p * j)),
                   pl.BlockSpec((8, nb_pad), lambda p, j: (0, 0))),
        scratch_shapes=[pltpu.VMEM((CLS_PAD, 1), F32), pltpu.VMEM((CLS_PAD, 1), F32)],
        compiler_params=pltpu.CompilerParams(dimension_semantics=("arbitrary", "arbitrary"),
                                             vmem_limit_bytes=VMEM_LIMIT),
        name="rank_tokens",
    )(rid, u_mat, tri)


def _invert_kernel(dest_hbm, init_hbm, tok_hbm, table, idx_smem, sem_idx, sem_tab):
    i = pl.program_id(0)
    n_steps = pl.num_programs(0)
    slot = i % 2

    def idx_copy(step, s):
        return pltpu.make_async_copy(dest_hbm.at[step], idx_smem.at[pl.ds(pl.multiple_of(s * TI, TI), TI)],
                                     sem_idx.at[s])

    @pl.when(i == 0)
    def _():
        idx_copy(0, 0).start()
        fill = pltpu.make_async_copy(init_hbm, table, sem_tab)
        fill.start()
        fill.wait()

    idx_copy(i, slot).wait()

    @pl.when(i + 1 < n_steps)
    def _():
        idx_copy(i + 1, 1 - slot).start()

    tok0 = i * TI
    off = slot * TI

    def place(t, carry):
        table[idx_smem[off + t]] = tok0 + t
        return carry

    lax.fori_loop(0, TI, place, 0, unroll=8)

    @pl.when(i + 1 == n_steps)
    def _():
        out = pltpu.make_async_copy(table, tok_hbm, sem_tab)
        out.start()
        out.wait()


def _invert_call(dest, n_rows_pad):
    n_tok = dest.shape[1]
    init = jnp.asarray(n_tok + np.arange(n_rows_pad, dtype=np.int32) % (2 * RB))
    return pl.pallas_call(
        _invert_kernel,
        out_shape=jax.ShapeDtypeStruct((n_rows_pad,), I32),
        grid=(n_tok // TI,),
        in_specs=[pl.BlockSpec(memory_space=pl.ANY), pl.BlockSpec(memory_space=pl.ANY)],
        out_specs=pl.BlockSpec(memory_space=pl.ANY),
        scratch_shapes=[pltpu.SMEM((n_rows_pad,), I32), pltpu.SMEM((2 * TI,), I32),
                        pltpu.SemaphoreType.DMA((2,)), pltpu.SemaphoreType.DMA],
        compiler_params=pltpu.CompilerParams(dimension_semantics=("arbitrary",)),
        name="invert_rows",
    )(dest.reshape(n_tok // TI, TI), init)


def _expert_kernel(ea_ref, eb_ref, nused_ref, tok_hbm, rows_hbm, wgu_a_ref, wd_a_ref, wgu_b_ref, wd_b_ref, y_hbm,
                   idx_smem, xbuf, ybuf, xs_ref, yacc_ref, sem_idx, sem_g, sem_s, *, n_tok):
    b = pl.program_id(0)
    nused = nused_ref[0]
    slot = b % 2
    blk_rows = RB * SUBLANES

    def idx_copy(blk, s):
        return pltpu.make_async_copy(tok_hbm.at[blk], idx_smem.at[s], sem_idx.at[s])

    def gathered(s):
        return pltpu.make_async_copy(rows_hbm.at[pl.ds(0, blk_rows)], xbuf.at[s], sem_g.at[s])

    def scattered(s):
        return pltpu.make_async_copy(ybuf.at[s], y_hbm.at[pl.ds(0, blk_rows)], sem_s.at[s])

    def gather_row(r, isl, s, priority):
        tok = jnp.minimum(idx_smem[isl, r], n_tok - 1)
        src = pl.multiple_of(tok * SUBLANES, SUBLANES)
        pltpu.make_async_copy(rows_hbm.at[pl.ds(src, SUBLANES)], xbuf.at[s, pl.ds(r * SUBLANES, SUBLANES)],
                              sem_g.at[s]).start(priority=priority)

    def scatter_row(r, isl, s, priority):
        dst = pl.multiple_of(idx_smem[isl, r] * Y_ROWS, Y_ROWS)
        pltpu.make_async_copy(ybuf.at[s, pl.ds(r * Y_ROWS, Y_ROWS)], y_hbm.at[pl.ds(dst, Y_ROWS)],
                              sem_s.at[s]).start(priority=priority)

    @pl.when(b == 0)
    def _():
        ybuf[...] = jnp.zeros_like(ybuf)
        for s in range(2):
            start = (n_tok + s * RB) * Y_ROWS
            dump = pltpu.make_async_copy(ybuf.at[s], y_hbm.at[pl.ds(start, blk_rows)], sem_s.at[s])
            dump.start()
            dump.wait()

        def clear(r, carry):
            idx_smem[IDX_SLOTS - 1, r] = n_tok + RB + r
            return carry

        lax.fori_loop(0, RB, clear, 0, unroll=8)
        idx_copy(0, 0).start()
        idx_copy(1, 1).start()
        idx_copy(0, 0).wait()

        def first(r, carry):
            gather_row(r, 0, 0, 0)
            return carry

        lax.fori_loop(0, RB, first, 0, unroll=8)

    @pl.when((b >= 1) & (b < nused))
    def _():
        scattered(slot).wait()

    nxt = (b + 1) % IDX_SLOTS
    prv = (b + IDX_SLOTS - 1) % IDX_SLOTS

    def tile_row(k):
        return xbuf[slot, pl.ds(k, RB, stride=SUBLANES), :]

    @pl.when(b < nused)
    def _():
        idx_copy(b + 1, nxt).wait()
        idx_copy(b + 2, (b + 2) % IDX_SLOTS).start()
        gathered(slot).wait()
        words = [tile_row(k) for k in range(PACK_ROWS)]
        lo = [lax.bitcast_convert_type(jnp.left_shift(w, 16), F32) for w in words]
        hi = [lax.bitcast_convert_type(jnp.bitwise_and(w, jnp.uint32(HI_MASK)), F32) for w in words]
        xs_ref[...] = jnp.concatenate(lo + hi, axis=1).astype(MXU_DTYPE)

    half = D_EXPERT // 2
    rq = RB // 4
    parts = ((wgu_a_ref, wd_a_ref, 0, 0, b + 1 <= nused), (wgu_a_ref, wd_a_ref, 0, 1, nused > b),
             (wgu_b_ref, wd_b_ref, 1, 0, nused - b >= 1), (wgu_b_ref, wd_b_ref, 1, 1, b - nused < 0))
    for q, (wgu_ref, wd_ref, gcol, h, live) in enumerate(parts):
        @pl.when(live)
        def _(q=q, wgu_ref=wgu_ref, wd_ref=wd_ref, gcol=gcol, h=h):
            for r in range(q * rq, (q + 1) * rq):
                gather_row(r, nxt, 1 - slot, r % 2)
                scatter_row(r, prv, 1 - slot, r % 2)
            x = xs_ref[...]
            c0 = h * half
            gpart = _mdot(x, wgu_ref[0, :, c0:c0 + half])
            upart = _mdot(x, wgu_ref[0, :, D_EXPERT + c0:D_EXPERT + c0 + half])
            act = (jax.nn.silu(gpart) * upart).astype(MXU_DTYPE)
            gate = lax.bitcast_convert_type(tile_row(PACK_ROWS), F32)[:, gcol:gcol + 1]
            part = _mdot(act, wd_ref[0, c0:c0 + half, :]) * gate
            if q == 0:
                yacc_ref[...] = part
            elif q < len(parts) - 1:
                yacc_ref[...] += part
            else:
                y = yacc_ref[...] + part
                for k in range(Y_ROWS):
                    ybuf[slot, pl.ds(k, RB, stride=SUBLANES), :] = y[:, k * LANES:(k + 1) * LANES]

    @pl.when(b == nused - 1)
    def _():
        cur = b % IDX_SLOTS

        def last(r, carry):
            scatter_row(r, cur, slot, 0)
            return carry

        lax.fori_loop(0, RB, last, 0, unroll=8)
        idx_copy(b + 2, (b + 2) % IDX_SLOTS).wait()
        gathered(1 - slot).wait()
        scattered(1 - slot).wait()
        scattered(slot).wait()


def _expert_call(blk_ea, blk_eb, nused, tok2d, rows, w_gate_up, w_down, n_tok):
    nblk = blk_ea.shape[0]

    def live(b, nu):
        return jnp.maximum(jnp.minimum(b, nu[0] - 1), 0)

    grid_spec = pltpu.PrefetchScalarGridSpec(
        num_scalar_prefetch=3,
        grid=(nblk,),
        in_specs=[pl.BlockSpec(memory_space=pl.ANY), pl.BlockSpec(memory_space=pl.ANY),
                  pl.BlockSpec((1,) + w_gate_up.shape[1:], lambda b, ea, eb, nu: (ea[live(b, nu)], 0, 0)),
                  pl.BlockSpec((1,) + w_down.shape[1:], lambda b, ea, eb, nu: (ea[live(b, nu)], 0, 0)),
                  pl.BlockSpec((1,) + w_gate_up.shape[1:], lambda b, ea, eb, nu: (eb[live(b, nu)], 0, 0)),
                  pl.BlockSpec((1,) + w_down.shape[1:], lambda b, ea, eb, nu: (eb[live(b, nu)], 0, 0))],
        out_specs=pl.BlockSpec(memory_space=pl.ANY),
        scratch_shapes=[pltpu.SMEM((IDX_SLOTS, RB), I32), pltpu.VMEM((2, RB * SUBLANES, LANES), U32),
                        pltpu.VMEM((2, RB * Y_ROWS, LANES), F32), pltpu.VMEM((RB, D_MODEL), MXU_DTYPE),
                        pltpu.VMEM((RB, D_MODEL), F32), pltpu.SemaphoreType.DMA((IDX_SLOTS,)),
                        pltpu.SemaphoreType.DMA((2,)), pltpu.SemaphoreType.DMA((2,))],
    )
    return pl.pallas_call(
        functools.partial(_expert_kernel, n_tok=n_tok),
        out_shape=jax.ShapeDtypeStruct(((n_tok + 2 * RB) * Y_ROWS, LANES), F32),
        grid_spec=grid_spec,
        compiler_params=pltpu.CompilerParams(dimension_semantics=("arbitrary",),
                                             vmem_limit_bytes=VMEM_LIMIT),
        name="pair_experts",
    )(blk_ea, blk_eb, nused, tok2d, rows, w_gate_up, w_down, w_gate_up, w_down)


def _final_kernel(y_ref, x1_ref, mod_ref, g_ref, b_ref, o_ref):
    y = jnp.concatenate([y_ref[pl.ds(k, TD, stride=Y_ROWS), :] for k in range(Y_ROWS)], axis=1)
    g2 = mod_ref[0][5:6]
    o_ref[0] = _layer_norm(ALPHA * x1_ref[...] + g2 * y, g_ref[...], b_ref[...])


def _final_call(y_tok, x1, mod, mod_off, tile_off, ln2g, ln2b, bsz, seq):
    d = x1.shape[1]
    ns = seq // TD
    return pl.pallas_call(
        _final_kernel,
        out_shape=jax.ShapeDtypeStruct((bsz, seq, d), F32),
        grid=(bsz, ns),
        in_specs=[pl.BlockSpec((TD * Y_ROWS, LANES), lambda b, s: (tile_off + b * ns + s, 0)),
                  pl.BlockSpec((TD, d), lambda b, s: (tile_off + b * ns + s, 0)),
                  pl.BlockSpec((1, N_MOD, d), lambda b, s: (mod_off + b, 0, 0)),
                  pl.BlockSpec((1, d), lambda b, s: (0, 0)),
                  pl.BlockSpec((1, d), lambda b, s: (0, 0))],
        out_specs=pl.BlockSpec((1, TD, d), lambda b, s: (b, s, 0)),
        compiler_params=pltpu.CompilerParams(dimension_semantics=("arbitrary", "arbitrary"),
                                             vmem_limit_bytes=VMEM_LIMIT),
        name="residual_ln2",
    )(y_tok, x1, mod, ln2g, ln2b)


def _band_matrices():
    rows = np.arange(CHUNK)[:, None]
    cols = np.arange(CHUNK + 2 * POOL_HALO)[None, :] - POOL_HALO
    return np.stack([((cols >= rows - w // 2) & (cols < rows + w // 2)) for w in POOL_WINDOWS]).astype(np.float32)


def _class_tables():
    ea, eb = [], []
    for g in range(N_GROUPS):
        for a in range(EPG):
            for b in range(a + 1, EPG):
                ea.append(g * EPG + a)
                eb.append(g * EPG + b)
    return np.asarray(ea, np.int32), np.asarray(eb, np.int32)


def _split_hi_lo(w):
    hi = w.astype(MXU_DTYPE)
    lo = (w - hi.astype(F32)).astype(MXU_DTYPE)
    return hi, lo


def kernel(x_prompt, x_sample, c_prompt, c_sample, w_ada, b_ada, w_in, v_norm_g, v_norm_b, w_spatial, b_spatial,
           w_pool, pool_scale, w_out, ln1_g, ln1_b, w_route_group, b_route_group, w_route_expert, b_route_expert,
           w_gate_up, w_down, ln2_g, ln2_b):
    assert w_ada.shape[0] == 1, "single-layer kernel"
    bp, sp, d = x_prompt.shape
    bs, ss, _ = x_sample.shape
    assert d == D_MODEL and sp % TS == 0 and ss % TS == 0
    n_prompt, n_sample = bp * sp, bs * ss
    n_tok = n_prompt + n_sample
    assert n_tok % BL == 0 and n_prompt % TD == 0 and n_tok % TI == 0

    c_all = jnp.concatenate([c_prompt, c_sample], axis=0)
    mod = _modulation(c_all, w_ada[0], b_ada).reshape(bp + bs, N_MOD, d)

    wr = jnp.concatenate([w_route_group[0], w_route_expert[0]], axis=1).T
    wr = jnp.pad(wr, ((0, ROUTE_PAD - N_ROUTE), (0, 0)))
    wr_hi, wr_lo = _split_hi_lo(wr)
    rbias = jnp.concatenate([b_route_group[0], b_route_expert[0].reshape(-1),
                             jnp.zeros((ROUTE_PAD - N_ROUTE,), F32)]).reshape(ROUTE_PAD, 1)
    row = lambda a: a.reshape(1, -1)
    weights = (w_in[0].astype(MXU_DTYPE), row(v_norm_g[0]), row(v_norm_b[0]), w_spatial[0].astype(MXU_DTYPE),
               b_spatial[0].T, jnp.asarray(_band_matrices(), MXU_DTYPE), w_pool[0].astype(MXU_DTYPE),
               row(pool_scale[0]), w_out[0].astype(MXU_DTYPE), row(ln1_g[0]), row(ln1_b[0]),
               jnp.concatenate([wr_hi, wr_lo], axis=0), rbias)

    x1, rows, rid = _mixer_call(x_prompt, x_sample, mod, weights)

    nblk = (n_tok + N_CLASSES * (RB - 1)) // RB
    nb_pad = -(-nblk // 128) * 128
    u_mat = jnp.asarray(np.triu(np.ones((SUB, SUB), np.float32)), MXU_DTYPE)
    tri = jnp.asarray(np.tril(np.ones((CLS_PAD, CLS_PAD), np.float32)))
    dest, meta = _rank_call(rid, u_mat, tri, nb_pad)
    tab_a, tab_b = _class_tables()
    blk_cls = meta[0, :nblk]
    blk_ea = jnp.asarray(tab_a)[blk_cls]
    blk_eb = jnp.asarray(tab_b)[blk_cls]
    nused = meta[1, 0:1]

    tok_rows = (nblk + 2) * RB
    tok = _invert_call(dest, -(-tok_rows // 1024) * 1024)
    tok2d = tok[:tok_rows].reshape(nblk + 2, RB)
    y_tok = _expert_call(blk_ea, blk_eb, nused, tok2d, rows, w_gate_up[0].astype(MXU_DTYPE),
                         w_down[0].astype(MXU_DTYPE), n_tok)

    ln2g, ln2b = row(ln2_g[0]), row(ln2_b[0])
    y_prompt = _final_call(y_tok, x1, mod, 0, 0, ln2g, ln2b, bp, sp)
    y_sample = _final_call(y_tok, x1, mod, bp, n_prompt // TD, ln2g, ln2b, bs, ss)
    return (y_prompt, y_sample)
```

```python
import functools

import numpy as np
import jax
import jax.numpy as jnp
from jax import lax
from jax.experimental import pallas as pl
from jax.experimental.pallas import tpu as pltpu

F32 = jnp.float32
I32 = jnp.int32
MXU_DTYPE = jnp.bfloat16

D_MODEL = 1024
A_WIDTH = 512
B_WIDTH = 512
CHUNK = 128
A_HEADS = 4
HEAD_DIM = A_WIDTH // A_HEADS
POOL_WINDOWS = (2, 4, 8, 16)
POOL_HALO = 8
GROUP_DIM = B_WIDTH // len(POOL_WINDOWS)
N_GROUPS = 4
EPG = 8
N_EXPERTS = N_GROUPS * EPG
D_EXPERT = 512
N_MOD = 6
LN_EPS = 1e-5
ALPHA = 2.0 ** 0.25

PAIRS = EPG * (EPG - 1) // 2
N_CLASSES = N_GROUPS * PAIRS
CLS_PAD = 128
N_ROUTE = N_GROUPS + N_EXPERTS
ROUTE_PAD = 40
U32 = jnp.uint32
LANES = 128
SUBLANES = 8
PACK_ROWS = D_MODEL // (2 * LANES)
Y_ROWS = D_MODEL // LANES
assert PACK_ROWS < SUBLANES and Y_ROWS == SUBLANES
HI_MASK = 0xFFFF0000

TS = 512
TD = 512
RB = 256
ROW_UNROLL = 8
BL = 2048
SUB = 256
VMEM_LIMIT = 56 * 1024 * 1024

_NT = (((1,), (1,)), ((), ()))

_GELU_K1 = -2.0 * (2.0 / np.pi) ** 0.5
_GELU_K3 = _GELU_K1 * 0.044715


def _gelu_tanh(x):
    return x / (1.0 + jnp.exp(x * (_GELU_K1 + _GELU_K3 * (x * x))))


def _layer_norm(x, g, b):
    mu = jnp.mean(x, axis=-1, keepdims=True)
    xc = x - mu
    var = jnp.mean(xc * xc, axis=-1, keepdims=True)
    return xc * lax.rsqrt(var + LN_EPS) * g + b


def _mdot(a, b):
    return jnp.dot(a, b, preferred_element_type=F32)


def _mod_kernel(c_ref, w_ref, b_ref, o_ref):
    a = jax.nn.silu(c_ref[...])
    o_ref[...] = jnp.dot(a, w_ref[...], precision=lax.Precision.HIGHEST,
                         preferred_element_type=F32) + b_ref[...]


def _modulation(c_all, w_ada, b_ada):
    nb, d = c_all.shape
    n = w_ada.shape[1]
    bn = 1536
    return pl.pallas_call(
        _mod_kernel,
        out_shape=jax.ShapeDtypeStruct((nb, n), F32),
        grid=(n // bn,),
        in_specs=[pl.BlockSpec((nb, d), lambda j: (0, 0)),
                  pl.BlockSpec((d, bn), lambda j: (0, j)),
                  pl.BlockSpec((1, bn), lambda j: (0, j))],
        out_specs=pl.BlockSpec((nb, bn), lambda j: (0, j)),
        compiler_params=pltpu.CompilerParams(dimension_semantics=("arbitrary",),
                                             vmem_limit_bytes=VMEM_LIMIT),
        name="adaln_mod",
    )(c_all, w_ada, b_ada)


def _mixer_kernel(xp_ref, xpp_ref, xpn_ref, xs_ref, xsp_ref, xsn_ref, mod_ref, modp_ref, inv_ref, win_ref, vng_ref,
                  vnb_ref, ws_ref, bst_ref, band_ref, wpool_ref, pscale_ref, wout_ref, ln1g_ref, ln1b_ref, wr_ref,
                  rb_ref, x1_ref, rows_ref, rid_ref, res_ref, *, prompt_tiles, prompt_seq, sample_seq):
    ts = xp_ref.shape[1]
    i = pl.program_id(0)
    slot = i % 2

    @pl.when(i == 0)
    def _():
        res_ref[1] = jnp.zeros(res_ref.shape[1:], F32)

    is_p = i < prompt_tiles
    ns = jnp.where(is_p, prompt_seq // ts, sample_seq // ts)
    s = jnp.where(is_p, i, i - prompt_tiles) % ns
    md = mod_ref[0]
    sh1, sc1, g1 = md[0:1], md[1:2], md[2:3]

    xt = jnp.where(is_p, xp_ref[0], xs_ref[0])
    h = xt * (1.0 + sc1) + sh1
    hp = jnp.where(s > 0, jnp.where(is_p, xpp_ref[0], xsp_ref[0]) * (1.0 + sc1) + sh1, 0.0)
    hn = jnp.where(s < ns - 1, jnp.where(is_p, xpn_ref[0], xsn_ref[0]) * (1.0 + sc1) + sh1, 0.0)
    hext = jnp.concatenate([h, hp, hn], axis=0).astype(MXU_DTYPE)
    zext = _mdot(hext, win_ref[...])
    z = zext[:ts]

    za = _gelu_tanh(z[:, :2 * A_WIDTH])
    u = za[:, :A_WIDTH]
    v = _layer_norm(za[:, A_WIDTH:], vng_ref[...], vnb_ref[...]).astype(MXU_DTYPE)
    bst = bst_ref[...]
    row_blocks = []
    for c in range(ts // CHUNK):
        cols = []
        for hh in range(A_HEADS):
            vv = v[c * CHUNK:(c + 1) * CHUNK, hh * HEAD_DIM:(hh + 1) * HEAD_DIM]
            cols.append(_mdot(ws_ref[hh], vv) + bst[:, hh:hh + 1])
        row_blocks.append(jnp.concatenate(cols, axis=1))
    a_out = u * jnp.concatenate(row_blocks, axis=0)

    p = z[:, 2 * A_WIDTH:]
    pall = jnp.concatenate([zext[ts:ts + POOL_HALO, 2 * A_WIDTH:], p,
                            zext[ts + POOL_HALO:ts + 2 * POOL_HALO, 2 * A_WIDTH:]], axis=0)
    p_hi = pall.astype(MXU_DTYPE)
    p_lo = (pall - p_hi.astype(F32)).astype(MXU_DTYPE)
    b_cols = []
    for g in range(len(POOL_WINDOWS)):
        wins = []
        for c in range(ts // CHUNK):
            r0 = c * CHUNK
            seg_hi = p_hi[r0:r0 + CHUNK + 2 * POOL_HALO, g * GROUP_DIM:(g + 1) * GROUP_DIM]
            seg_lo = p_lo[r0:r0 + CHUNK + 2 * POOL_HALO, g * GROUP_DIM:(g + 1) * GROUP_DIM]
            wins.append(_mdot(band_ref[g], seg_hi) + _mdot(band_ref[g], seg_lo))
        pooled = jnp.concatenate(wins, axis=0) * inv_ref[0, g] - p[:, g * GROUP_DIM:(g + 1) * GROUP_DIM]
        b_cols.append(_mdot(pooled.astype(MXU_DTYPE), wpool_ref[g]))
    b_out = jnp.concatenate(b_cols, axis=1) * pscale_ref[...]

    mix_in = jnp.concatenate([a_out, b_out], axis=1).astype(MXU_DTYPE)
    mdp = modp_ref[0]
    h2_prev, lt_prev = _ln1_router(res_ref[1 - slot], mdp[3:4], mdp[4:5], ln1g_ref, ln1b_ref, wr_ref, rb_ref,
                                   x1_ref)
    _route_and_pack(h2_prev, lt_prev, rows_ref, rid_ref)

    mix = _mdot(mix_in, wout_ref[...])
    res_ref[slot] = ALPHA * xt + g1 * mix


def _ln1_router(res, sh2, sc2, ln1g_ref, ln1b_ref, wr_ref, rb_ref, x1_ref):
    x1 = _layer_norm(res, ln1g_ref[...], ln1b_ref[...])
    x1_ref[...] = x1
    h2 = x1 * (1.0 + sc2) + sh2
    h2_hi = h2.astype(MXU_DTYPE)
    h2_lo = (h2 - h2_hi.astype(F32)).astype(MXU_DTYPE)
    l1 = lax.dot_general(wr_ref[...], h2_hi, _NT, preferred_element_type=F32)
    l2 = lax.dot_general(wr_ref[0:ROUTE_PAD], h2_lo, _NT, preferred_element_type=F32)
    return h2, l1[:ROUTE_PAD] + l1[ROUTE_PAD:] + l2 + rb_ref[...]


def _route_and_pack(h2, lt, rows_ref, rid_ref):
    ts = h2.shape[0]

    def row(r):
        return lt[r:r + 1, :]

    gl = [row(r) for r in range(N_GROUPS)]
    gmax = jnp.maximum(jnp.maximum(gl[0], gl[1]), jnp.maximum(gl[2], gl[3]))
    gidx = jnp.where(gl[0] == gmax, 0, jnp.where(gl[1] == gmax, 1, jnp.where(gl[2] == gmax, 2, 3)))
    gsum = (jnp.exp(gl[0] - gmax) + jnp.exp(gl[1] - gmax)) + (jnp.exp(gl[2] - gmax) + jnp.exp(gl[3] - gmax))
    gw = 1.0 / gsum
    ev = [jnp.where(gidx == 0, row(N_GROUPS + j),
                    jnp.where(gidx == 1, row(N_GROUPS + EPG + j),
                              jnp.where(gidx == 2, row(N_GROUPS + 2 * EPG + j), row(N_GROUPS + 3 * EPG + j))))
          for j in range(EPG)]

    def top1(vals):
        m = vals[0]
        for t in vals[1:]:
            m = jnp.maximum(m, t)
        idx = jnp.full(m.shape, EPG - 1, I32)
        for j in range(EPG - 2, -1, -1):
            idx = jnp.where(vals[j] == m, j, idx)
        return m, idx

    v1, j1 = top1(ev)
    v2, j2 = top1([jnp.where(j1 == j, -jnp.inf, ev[j]) for j in range(EPG)])
    t2 = jnp.exp(v2 - v1)
    den = 1.0 + t2
    w1 = (1.0 / den) * gw
    w2 = (t2 / den) * gw
    first = j1 < j2
    ea = jnp.minimum(j1, j2)
    eb = jnp.maximum(j1, j2)
    wa = jnp.where(first, w1, w2)
    wb = jnp.where(first, w2, w1)
    cls = gidx * PAIRS + jnp.right_shift(ea * (2 * EPG - 1 - ea), 1) + (eb - ea - 1)

    r8 = lax.broadcasted_iota(I32, (8, ts), 0)
    rid_ref[...] = jnp.where(r8 == 0, cls, jnp.where(r8 == 1, gidx * EPG + ea,
                                                      jnp.where(r8 == 2, gidx * EPG + eb, 0)))
    w8 = jnp.where(r8 == 0, wa, jnp.where(r8 == 1, wb, 0.0))
    w128 = jnp.concatenate([w8, jnp.zeros((LANES - 8, ts), F32)], axis=0)
    bits = lax.bitcast_convert_type(h2.astype(jnp.bfloat16).astype(F32), U32)
    half = D_MODEL // 2
    for k in range(PACK_ROWS):
        lo = jnp.right_shift(bits[:, k * LANES:(k + 1) * LANES], 16)
        hi = jnp.bitwise_and(bits[:, half + k * LANES:half + (k + 1) * LANES], jnp.uint32(HI_MASK))
        rows_ref[pl.ds(k, ts, stride=SUBLANES), :] = jnp.bitwise_or(lo, hi)
    rows_ref[pl.ds(PACK_ROWS, ts, stride=SUBLANES), :] = lax.bitcast_convert_type(w128.T, U32)
    for k in range(PACK_ROWS + 1, SUBLANES):
        rows_ref[pl.ds(k, ts, stride=SUBLANES), :] = jnp.zeros((ts, LANES), U32)


def _inv_population_table():
    r = np.arange(TS)
    out = np.empty((3, len(POOL_WINDOWS), TS, LANES), np.float32)
    for case in range(3):
        for g, w in enumerate(POOL_WINDOWS):
            lo = np.maximum(r - w // 2, 0) if case == 0 else r - w // 2
            hi = np.minimum(r + w // 2, TS) if case == 2 else r + w // 2
            out[case, g] = (1.0 / (hi - lo).astype(np.float64)).astype(np.float32)[:, None]
    return out


def _mixer_call(x_prompt, x_sample, mod, weights):
    bp, sp, d = x_prompt.shape
    bs, ss, _ = x_sample.shape
    nsp, nss = sp // TS, ss // TS
    assert nsp >= 2 and nss >= 2, "a tile is the first or the last of its sequence, not both"
    ntp, nts = bp * nsp, bs * nss
    n_tok = bp * sp + bs * ss
    hb = TS // POOL_HALO

    def p_tile(i):
        t = jnp.minimum(i, ntp - 1)
        return t // nsp, t % nsp

    def s_tile(i):
        t = jnp.clip(i - ntp, 0, nts - 1)
        return t // nss, t % nss

    def specs(tile_fn, seq):
        def cur(i):
            b, s = tile_fn(i)
            return (b, s, 0)

        def prev(i):
            b, s = tile_fn(i)
            return (b, jnp.maximum(s * hb - 1, 0), 0)

        def nxt(i):
            b, s = tile_fn(i)
            return (b, jnp.minimum((s + 1) * hb, seq // POOL_HALO - 1), 0)

        return [pl.BlockSpec((1, TS, d), cur), pl.BlockSpec((1, POOL_HALO, d), prev),
                pl.BlockSpec((1, POOL_HALO, d), nxt)]

    def mod_map(i):
        return (jnp.where(i < ntp, p_tile(i)[0], bp + s_tile(i)[0]), 0, 0)

    def inv_map(i):
        s = jnp.where(i < ntp, p_tile(i)[1], s_tile(i)[1])
        last = jnp.where(i < ntp, nsp - 1, nss - 1)
        return (jnp.where(s == 0, 0, jnp.where(s == last, 2, 1)), 0, 0, 0)

    def const(w):
        return pl.BlockSpec(w.shape, lambda i, nd=w.ndim: (0,) * nd)

    def prev_tile(i):
        return jnp.maximum(i - 1, 0)

    inv_tab = jnp.asarray(_inv_population_table())
    in_specs = (specs(p_tile, sp) + specs(s_tile, ss)
                + [pl.BlockSpec((1, N_MOD, d), mod_map),
                   pl.BlockSpec((1, N_MOD, d), lambda i: mod_map(prev_tile(i))),
                   pl.BlockSpec((1,) + inv_tab.shape[1:], inv_map)]
                + [const(w) for w in weights])
    out_shape = (jax.ShapeDtypeStruct((n_tok, d), F32),
                 jax.ShapeDtypeStruct((n_tok * SUBLANES, LANES), U32),
                 jax.ShapeDtypeStruct((8, n_tok), I32))
    out_specs = (pl.BlockSpec((TS, d), lambda i: (prev_tile(i), 0)),
                 pl.BlockSpec((TS * SUBLANES, LANES), lambda i: (prev_tile(i), 0)),
                 pl.BlockSpec((8, TS), lambda i: (0, prev_tile(i))))
    return pl.pallas_call(
        functools.partial(_mixer_kernel, prompt_tiles=ntp, prompt_seq=sp, sample_seq=ss),
        out_shape=out_shape,
        grid=(ntp + nts + 1,),
        in_specs=in_specs,
        out_specs=out_specs,
        scratch_shapes=[pltpu.VMEM((2, TS, d), F32)],
        compiler_params=pltpu.CompilerParams(dimension_semantics=("arbitrary",),
                                             vmem_limit_bytes=VMEM_LIMIT),
        name="mixer_ln1_route",
    )(x_prompt, x_prompt, x_prompt, x_sample, x_sample, x_sample, mod, mod, inv_tab, *weights)


def _rank_kernel(rid_ref, u_ref, tri_ref, dest_ref, meta_ref, cnt_ref, base_ref, *, nb_pad):
    phase = pl.program_id(0)
    j = pl.program_id(1)
    cls_iota = lax.broadcasted_iota(I32, (CLS_PAD, SUB), 0)

    @pl.when((phase == 0) & (j == 0))
    def _():
        cnt_ref[...] = jnp.zeros_like(cnt_ref)

    @pl.when(phase == 0)
    def _():
        acc = cnt_ref[...]
        for sb in range(BL // SUB):
            ids = rid_ref[0:1, sb * SUB:(sb + 1) * SUB]
            acc = acc + jnp.sum((cls_iota == ids).astype(F32), axis=1, keepdims=True)
        cnt_ref[...] = acc

    @pl.when((phase == 1) & (j == 0))
    def _():
        cnt = jnp.broadcast_to(cnt_ref[...], (CLS_PAD, CLS_PAD))
        nblk = jnp.floor((cnt + (RB - 1)) * (1.0 / RB))
        cum = jnp.dot(tri_ref[...], nblk, precision=lax.Precision.HIGHEST, preferred_element_type=F32)
        base_ref[...] = (cum[:, 0:1] - nblk[:, 0:1]) * RB
        blk = lax.broadcasted_iota(I32, (CLS_PAD, nb_pad), 1).astype(F32)
        bcls = jnp.sum((cum[:, 0:1] <= blk).astype(F32), axis=0, keepdims=True)
        bcls = jnp.minimum(bcls, N_CLASSES - 1).astype(I32)
        nused = jnp.broadcast_to(cum[CLS_PAD - 1:CLS_PAD, 0:1], (1, nb_pad)).astype(I32)
        cum_l = jnp.transpose(cum)[0:1].astype(I32)
        nblk_l = jnp.transpose(nblk)[0:1].astype(I32)
        pad = jnp.zeros((1, nb_pad - CLS_PAD), I32)
        r8 = lax.broadcasted_iota(I32, (8, nb_pad), 0)
        cum_row = jnp.concatenate([cum_l, pad], axis=1)
        nblk_row = jnp.concatenate([nblk_l, pad], axis=1)
        meta_ref[...] = jnp.where(r8 == 0, bcls, jnp.where(r8 == 1, nused,
                                  jnp.where(r8 == 2, cum_row, jnp.where(r8 == 3, nblk_row, 0))))

    @pl.when(phase == 1)
    def _():
        base = base_ref[...]
        for sb in range(BL // SUB):
            ids = rid_ref[0:1, sb * SUB:(sb + 1) * SUB]
            hit = cls_iota == ids
            incl = _mdot(hit.astype(MXU_DTYPE), u_ref[...])
            slot = jnp.sum(jnp.where(hit, base + incl - 1.0, 0.0), axis=0, keepdims=True)
            dest_ref[0:1, sb * SUB:(sb + 1) * SUB] = slot.astype(I32)
            base = base + incl[:, SUB - 1:SUB]
        base_ref[...] = base


def _rank_call(rid, u_mat, tri, nb_pad):
    n_tok = rid.shape[1]
    nj = n_tok // BL
    return pl.pallas_call(
        functools.partial(_rank_kernel, nb_pad=nb_pad),
        out_shape=(jax.ShapeDtypeStruct((1, n_tok), I32), jax.ShapeDtypeStruct((8, nb_pad), I32)),
        grid=(2, nj),
        in_specs=[pl.BlockSpec((8, BL), lambda p, j: (0, j)),
                  pl.BlockSpec(u_mat.shape, lambda p, j: (0, 0)),
                  pl.BlockSpec(tri.shape, lambda p, j: (0, 0))],
        out_specs=(pl.BlockSpec((1, BL), lambda p, j: (0, p * j)),
                   pl.BlockSpec((8, nb_pad), lambda p, j: (0, 0))),
        scratch_shapes=[pltpu.VMEM((CLS_PAD, 1), F32), pltpu.VMEM((CLS_PAD, 1), F32)],
        compiler_params=pltpu.CompilerParams(dimension_semantics=("arbitrary", "arbitrary"),
                                             vmem_limit_bytes=VMEM_LIMIT),
        name="rank_tokens",
    )(rid, u_mat, tri)


def _dispatch_kernel(cum_ref, nblk_ref, nused_ref, dest_hbm, rows_hbm, out_hbm, idx_smem, rows_buf, zero_ref,
                     sem_idx, sem_in, sem_row, sem_zero):
    i = pl.program_id(0)
    n_steps = pl.num_programs(0)
    slot = i % 2
    tile_rows = TD * SUBLANES
    blk_rows = RB * SUBLANES
    n_blocks = out_hbm.shape[0] // blk_rows

    def fetch(step, s):
        start = pl.multiple_of(step * tile_rows, tile_rows)
        return (pltpu.make_async_copy(dest_hbm.at[step], idx_smem.at[s], sem_idx.at[s]),
                pltpu.make_async_copy(rows_hbm.at[pl.ds(start, tile_rows)], rows_buf.at[s], sem_in.at[s]))

    def scattered(s):
        return pltpu.make_async_copy(rows_buf.at[s], out_hbm.at[pl.ds(0, tile_rows)], sem_row.at[s])

    def zero_block(blk):
        start = pl.multiple_of(blk * blk_rows, blk_rows)
        return pltpu.make_async_copy(zero_ref, out_hbm.at[pl.ds(start, blk_rows)], sem_zero)

    @pl.when(i == 0)
    def _():
        zero_ref[...] = jnp.zeros_like(zero_ref)

        def start(c, carry):
            @pl.when(nblk_ref[c] > 0)
            def _():
                zero_block(cum_ref[c] - 1).start()
            return carry

        def wait(c, carry):
            @pl.when(nblk_ref[c] > 0)
            def _():
                zero_block(cum_ref[c] - 1).wait()
            return carry

        def start_tail(blk, carry):
            zero_block(blk).start()
            return carry

        def wait_tail(blk, carry):
            zero_block(blk).wait()
            return carry

        lax.fori_loop(0, N_CLASSES, start, 0)
        lax.fori_loop(nused_ref[0], n_blocks, start_tail, 0)
        lax.fori_loop(0, N_CLASSES, wait, 0)
        lax.fori_loop(nused_ref[0], n_blocks, wait_tail, 0)
        for cp in fetch(0, 0):
            cp.start()

    for cp in fetch(i, slot):
        cp.wait()

    @pl.when(i > 0)
    def _():
        scattered(1 - slot).wait()

    @pl.when(i + 1 < n_steps)
    def _():
        for cp in fetch(i + 1, 1 - slot):
            cp.start()

    def scatter(g, carry):
        for u in range(ROW_UNROLL):
            t = g * ROW_UNROLL + u
            src = pl.multiple_of(t * SUBLANES, SUBLANES)
            dst = pl.multiple_of(idx_smem[slot, t] * SUBLANES, SUBLANES)
            pltpu.make_async_copy(rows_buf.at[slot, pl.ds(src, SUBLANES)], out_hbm.at[pl.ds(dst, SUBLANES)],
                                  sem_row.at[slot]).start(priority=u % 2)
        return carry

    lax.fori_loop(0, TD // ROW_UNROLL, scatter, 0)

    @pl.when(i + 1 == n_steps)
    def _():
        scattered(slot).wait()


def _dispatch_call(cum_cls, nblk_cls, nused, dest2d, rows, n_rows):
    n_tok = rows.shape[0] // SUBLANES
    grid_spec = pltpu.PrefetchScalarGridSpec(
        num_scalar_prefetch=3,
        grid=(n_tok // TD,),
        in_specs=[pl.BlockSpec(memory_space=pl.ANY), pl.BlockSpec(memory_space=pl.ANY)],
        out_specs=pl.BlockSpec(memory_space=pl.ANY),
        scratch_shapes=[pltpu.SMEM((2, TD), I32), pltpu.VMEM((2, TD * SUBLANES, LANES), U32),
                        pltpu.VMEM((RB * SUBLANES, LANES), U32),
                        pltpu.SemaphoreType.DMA((2,)), pltpu.SemaphoreType.DMA((2,)),
                        pltpu.SemaphoreType.DMA((2,)), pltpu.SemaphoreType.DMA],
    )
    return pl.pallas_call(
        _dispatch_kernel,
        out_shape=jax.ShapeDtypeStruct((n_rows * SUBLANES, LANES), U32),
        grid_spec=grid_spec,
        compiler_params=pltpu.CompilerParams(dimension_semantics=("arbitrary",),
                                             vmem_limit_bytes=VMEM_LIMIT),
        name="dispatch_rows",
    )(cum_cls, nblk_cls, nused, dest2d, rows)


def _expert_kernel(ea_ref, eb_ref, nused_ref, rows_ref, wgu_a_ref, wd_a_ref, wgu_b_ref, wd_b_ref, y_ref):
    @pl.when(pl.program_id(0) < nused_ref[0])
    def _():
        def tile_row(k):
            return rows_ref[pl.ds(k, RB, stride=SUBLANES), :]

        words = [tile_row(k) for k in range(PACK_ROWS)]
        lo = [lax.bitcast_convert_type(jnp.left_shift(w, 16), F32) for w in words]
        hi = [lax.bitcast_convert_type(jnp.bitwise_and(w, jnp.uint32(HI_MASK)), F32) for w in words]
        x = jnp.concatenate(lo + hi, axis=1).astype(MXU_DTYPE)
        gate = lax.bitcast_convert_type(tile_row(PACK_ROWS), F32)

        def expert(wgu_ref, wd_ref):
            gu = _mdot(x, wgu_ref[0])
            act = jax.nn.silu(gu[:, :D_EXPERT]) * gu[:, D_EXPERT:]
            return _mdot(act.astype(MXU_DTYPE), wd_ref[0])

        y = expert(wgu_a_ref, wd_a_ref) * gate[:, 0:1] + expert(wgu_b_ref, wd_b_ref) * gate[:, 1:2]
        for k in range(Y_ROWS):
            y_ref[pl.ds(k, RB, stride=SUBLANES), :] = y[:, k * LANES:(k + 1) * LANES]

    @pl.when(pl.program_id(0) >= nused_ref[0])
    def _():
        y_ref[...] = jnp.zeros_like(y_ref)


def _expert_call(blk_ea, blk_eb, nused, rows, w_gate_up, w_down):
    n_rows = rows.shape[0] // SUBLANES
    nblk = n_rows // RB

    def live(b, nu):
        return jnp.maximum(jnp.minimum(b, nu[0] - 1), 0)

    grid_spec = pltpu.PrefetchScalarGridSpec(
        num_scalar_prefetch=3,
        grid=(nblk,),
        in_specs=[pl.BlockSpec((RB * SUBLANES, LANES), lambda b, ea, eb, nu: (live(b, nu), 0)),
                  pl.BlockSpec((1,) + w_gate_up.shape[1:], lambda b, ea, eb, nu: (ea[live(b, nu)], 0, 0)),
                  pl.BlockSpec((1,) + w_down.shape[1:], lambda b, ea, eb, nu: (ea[live(b, nu)], 0, 0)),
                  pl.BlockSpec((1,) + w_gate_up.shape[1:], lambda b, ea, eb, nu: (eb[live(b, nu)], 0, 0)),
                  pl.BlockSpec((1,) + w_down.shape[1:], lambda b, ea, eb, nu: (eb[live(b, nu)], 0, 0))],
        out_specs=pl.BlockSpec((RB * Y_ROWS, LANES), lambda b, ea, eb, nu: (b, 0)),
    )
    return pl.pallas_call(
        _expert_kernel,
        out_shape=jax.ShapeDtypeStruct((n_rows * Y_ROWS, LANES), F32),
        grid_spec=grid_spec,
        compiler_params=pltpu.CompilerParams(dimension_semantics=("arbitrary",),
                                             vmem_limit_bytes=VMEM_LIMIT),
        name="pair_experts",
    )(blk_ea, blk_eb, nused, rows, w_gate_up, w_down, w_gate_up, w_down)


def _combine_kernel(dest_hbm, y_hbm, x1_ref, mod_ref, g_ref, b_ref, o_ref, idx_smem, ybuf, sem_idx, sem_row,
                    *, tile_off):
    ns = pl.num_programs(1)
    i = pl.program_id(0) * ns + pl.program_id(1)
    n_steps = pl.num_programs(0) * ns
    slot = i % 2

    def idx_copy(step, s):
        return pltpu.make_async_copy(dest_hbm.at[tile_off + step], idx_smem.at[s], sem_idx.at[s])

    def issue_gather(s):
        def gather(g, carry):
            for u in range(ROW_UNROLL):
                t = g * ROW_UNROLL + u
                src = pl.multiple_of(idx_smem[s, t] * Y_ROWS, Y_ROWS)
                dst = pl.multiple_of(t * Y_ROWS, Y_ROWS)
                pltpu.make_async_copy(y_hbm.at[pl.ds(src, Y_ROWS)], ybuf.at[s, pl.ds(dst, Y_ROWS)],
                                      sem_row.at[s]).start(priority=u % 2)
            return carry

        lax.fori_loop(0, TD // ROW_UNROLL, gather, 0)

    @pl.when(i == 0)
    def _():
        idx_copy(0, 0).start()
        idx_copy(0, 0).wait()
        issue_gather(0)

        @pl.when(n_steps > 1)
        def _():
            idx_copy(1, 1).start()

    @pl.when(i + 1 < n_steps)
    def _():
        idx_copy(i + 1, 1 - slot).wait()
        issue_gather(1 - slot)

    @pl.when(i + 2 < n_steps)
    def _():
        idx_copy(i + 2, slot).start()

    pltpu.make_async_copy(y_hbm.at[pl.ds(0, TD * Y_ROWS)], ybuf.at[slot], sem_row.at[slot]).wait()

    y = jnp.concatenate([ybuf[slot, pl.ds(k, TD, stride=Y_ROWS), :] for k in range(Y_ROWS)], axis=1)
    g2 = mod_ref[0][5:6]
    o_ref[0] = _layer_norm(ALPHA * x1_ref[...] + g2 * y, g_ref[...], b_ref[...])


def _combine_call(dest2d, y_rows, x1, mod, mod_off, tile_off, ln2g, ln2b, bsz, seq):
    d = x1.shape[1]
    ns = seq // TD
    return pl.pallas_call(
        functools.partial(_combine_kernel, tile_off=tile_off),
        out_shape=jax.ShapeDtypeStruct((bsz, seq, d), F32),
        grid=(bsz, ns),
        in_specs=[pl.BlockSpec(memory_space=pl.ANY),
                  pl.BlockSpec(memory_space=pl.ANY),
                  pl.BlockSpec((TD, d), lambda b, s: (tile_off + b * ns + s, 0)),
                  pl.BlockSpec((1, N_MOD, d), lambda b, s: (mod_off + b, 0, 0)),
                  pl.BlockSpec((1, d), lambda b, s: (0, 0)),
                  pl.BlockSpec((1, d), lambda b, s: (0, 0))],
        out_specs=pl.BlockSpec((1, TD, d), lambda b, s: (b, s, 0)),
        scratch_shapes=[pltpu.SMEM((2, TD), I32), pltpu.VMEM((2, TD * Y_ROWS, LANES), F32),
                        pltpu.SemaphoreType.DMA((2,)), pltpu.SemaphoreType.DMA((2,))],
        compiler_params=pltpu.CompilerParams(dimension_semantics=("arbitrary", "arbitrary"),
                                             vmem_limit_bytes=VMEM_LIMIT),
        name="combine_ln2",
    )(dest2d, y_rows, x1, mod, ln2g, ln2b)


def _band_matrices():
    rows = np.arange(CHUNK)[:, None]
    cols = np.arange(CHUNK + 2 * POOL_HALO)[None, :] - POOL_HALO
    return np.stack([((cols >= rows - w // 2) & (cols < rows + w // 2)) for w in POOL_WINDOWS]).astype(np.float32)


def _class_tables():
    ea, eb = [], []
    for g in range(N_GROUPS):
        for a in range(EPG):
            for b in range(a + 1, EPG):
                ea.append(g * EPG + a)
                eb.append(g * EPG + b)
    return np.asarray(ea, np.int32), np.asarray(eb, np.int32)


def _split_hi_lo(w):
    hi = w.astype(MXU_DTYPE)
    lo = (w - hi.astype(F32)).astype(MXU_DTYPE)
    return hi, lo


def kernel(x_prompt, x_sample, c_prompt, c_sample, w_ada, b_ada, w_in, v_norm_g, v_norm_b, w_spatial, b_spatial,
           w_pool, pool_scale, w_out, ln1_g, ln1_b, w_route_group, b_route_group, w_route_expert, b_route_expert,
           w_gate_up, w_down, ln2_g, ln2_b):
    assert w_ada.shape[0] == 1, "single-layer kernel"
    bp, sp, d = x_prompt.shape
    bs, ss, _ = x_sample.shape
    assert d == D_MODEL and sp % TS == 0 and ss % TS == 0
    n_prompt, n_sample = bp * sp, bs * ss
    n_tok = n_prompt + n_sample
    assert n_tok % BL == 0 and n_prompt % TD == 0

    c_all = jnp.concatenate([c_prompt, c_sample], axis=0)
    mod = _modulation(c_all, w_ada[0], b_ada).reshape(bp + bs, N_MOD, d)

    wr = jnp.concatenate([w_route_group[0], w_route_expert[0]], axis=1).T
    wr = jnp.pad(wr, ((0, ROUTE_PAD - N_ROUTE), (0, 0)))
    wr_hi, wr_lo = _split_hi_lo(wr)
    rbias = jnp.concatenate([b_route_group[0], b_route_expert[0].reshape(-1),
                             jnp.zeros((ROUTE_PAD - N_ROUTE,), F32)]).reshape(ROUTE_PAD, 1)
    row = lambda a: a.reshape(1, -1)
    weights = (w_in[0].astype(MXU_DTYPE), row(v_norm_g[0]), row(v_norm_b[0]), w_spatial[0].astype(MXU_DTYPE),
               b_spatial[0].T, jnp.asarray(_band_matrices(), MXU_DTYPE), w_pool[0].astype(MXU_DTYPE),
               row(pool_scale[0]), w_out[0].astype(MXU_DTYPE), row(ln1_g[0]), row(ln1_b[0]),
               jnp.concatenate([wr_hi, wr_lo], axis=0), rbias)

    x1, rows, rid = _mixer_call(x_prompt, x_sample, mod, weights)

    nblk = (n_tok + N_CLASSES * (RB - 1)) // RB
    nb_pad = -(-nblk // 128) * 128
    u_mat = jnp.asarray(np.triu(np.ones((SUB, SUB), np.float32)), MXU_DTYPE)
    tri = jnp.asarray(np.tril(np.ones((CLS_PAD, CLS_PAD), np.float32)))
    dest, meta = _rank_call(rid, u_mat, tri, nb_pad)
    dest2d = dest.reshape(n_tok // TD, TD)
    tab_a, tab_b = _class_tables()
    blk_cls = meta[0, :nblk]
    blk_ea = jnp.asarray(tab_a)[blk_cls]
    blk_eb = jnp.asarray(tab_b)[blk_cls]
    nused = meta[1, 0:1]
    cum_cls = meta[2, :CLS_PAD]
    nblk_cls = meta[3, :CLS_PAD]

    sorted_rows = _dispatch_call(cum_cls, nblk_cls, nused, dest2d, rows, nblk * RB)
    y_rows = _expert_call(blk_ea, blk_eb, nused, sorted_rows, w_gate_up[0].astype(MXU_DTYPE),
                          w_down[0].astype(MXU_DTYPE))

    ln2g, ln2b = row(ln2_g[0]), row(ln2_b[0])
    y_prompt = _combine_call(dest2d, y_rows, x1, mod, 0, 0, ln2g, ln2b, bp, sp)
    y_sample = _combine_call(dest2d, y_rows, x1, mod, bp, n_prompt // TD, ln2g, ln2b, bs, ss)
    return (y_prompt, y_sample)
```

```python
import functools

import numpy as np
import jax
import jax.numpy as jnp
from jax import lax
from jax.experimental import pallas as pl
from jax.experimental.pallas import tpu as pltpu

F32 = jnp.float32
I32 = jnp.int32
MXU_DTYPE = jnp.bfloat16

D_MODEL = 1024
A_WIDTH = 512
B_WIDTH = 512
CHUNK = 128
A_HEADS = 4
HEAD_DIM = A_WIDTH // A_HEADS
POOL_WINDOWS = (2, 4, 8, 16)
POOL_HALO = 8
GROUP_DIM = B_WIDTH // len(POOL_WINDOWS)
N_GROUPS = 4
EPG = 8
N_EXPERTS = N_GROUPS * EPG
D_EXPERT = 512
N_MOD = 6
LN_EPS = 1e-5
ALPHA = 2.0 ** 0.25

PAIRS = EPG * (EPG - 1) // 2
N_CLASSES = N_GROUPS * PAIRS
CLS_PAD = 128
N_ROUTE = N_GROUPS + N_EXPERTS
ROUTE_PAD = 40
U32 = jnp.uint32
LANES = 128
SUBLANES = 8
PACK_ROWS = D_MODEL // (2 * LANES)
Y_ROWS = D_MODEL // LANES
assert PACK_ROWS < SUBLANES and Y_ROWS == SUBLANES
HI_MASK = 0xFFFF0000

TS = 512
TD = 512
RB = 256
ROW_UNROLL = 8
BL = 2048
SUB = 256
VMEM_LIMIT = 56 * 1024 * 1024

_NT = (((1,), (1,)), ((), ()))

_GELU_K1 = -2.0 * (2.0 / np.pi) ** 0.5
_GELU_K3 = _GELU_K1 * 0.044715


def _gelu_tanh(x):
    return x / (1.0 + jnp.exp(x * (_GELU_K1 + _GELU_K3 * (x * x))))


def _layer_norm(x, g, b):
    mu = jnp.mean(x, axis=-1, keepdims=True)
    xc = x - mu
    var = jnp.mean(xc * xc, axis=-1, keepdims=True)
    return xc * lax.rsqrt(var + LN_EPS) * g + b


def _mdot(a, b):
    return jnp.dot(a, b, preferred_element_type=F32)


def _mod_kernel(c_ref, w_ref, b_ref, o_ref):
    a = jax.nn.silu(c_ref[...])
    o_ref[...] = jnp.dot(a, w_ref[...], precision=lax.Precision.HIGHEST,
                         preferred_element_type=F32) + b_ref[...]


def _modulation(c_all, w_ada, b_ada):
    nb, d = c_all.shape
    n = w_ada.shape[1]
    bn = 1536
    return pl.pallas_call(
        _mod_kernel,
        out_shape=jax.ShapeDtypeStruct((nb, n), F32),
        grid=(n // bn,),
        in_specs=[pl.BlockSpec((nb, d), lambda j: (0, 0)),
                  pl.BlockSpec((d, bn), lambda j: (0, j)),
                  pl.BlockSpec((1, bn), lambda j: (0, j))],
        out_specs=pl.BlockSpec((nb, bn), lambda j: (0, j)),
        compiler_params=pltpu.CompilerParams(dimension_semantics=("arbitrary",),
                                             vmem_limit_bytes=VMEM_LIMIT),
        name="adaln_mod",
    )(c_all, w_ada, b_ada)


def _mixer_kernel(xp_ref, xpp_ref, xpn_ref, xs_ref, xsp_ref, xsn_ref, mod_ref, modp_ref, inv_ref, win_ref, vng_ref,
                  vnb_ref, ws_ref, bst_ref, band_ref, wpool_ref, pscale_ref, wout_ref, ln1g_ref, ln1b_ref, wr_ref,
                  rb_ref, x1_ref, rows_ref, rid_ref, res_ref, *, prompt_tiles, prompt_seq, sample_seq):
    ts = xp_ref.shape[1]
    i = pl.program_id(0)
    slot = i % 2

    @pl.when(i == 0)
    def _():
        res_ref[1] = jnp.zeros(res_ref.shape[1:], F32)

    is_p = i < prompt_tiles
    ns = jnp.where(is_p, prompt_seq // ts, sample_seq // ts)
    s = jnp.where(is_p, i, i - prompt_tiles) % ns
    md = mod_ref[0]
    sh1, sc1, g1 = md[0:1], md[1:2], md[2:3]

    xt = jnp.where(is_p, xp_ref[0], xs_ref[0])
    h = xt * (1.0 + sc1) + sh1
    hp = jnp.where(s > 0, jnp.where(is_p, xpp_ref[0], xsp_ref[0]) * (1.0 + sc1) + sh1, 0.0)
    hn = jnp.where(s < ns - 1, jnp.where(is_p, xpn_ref[0], xsn_ref[0]) * (1.0 + sc1) + sh1, 0.0)
    hext = jnp.concatenate([h, hp, hn], axis=0).astype(MXU_DTYPE)
    zext = _mdot(hext, win_ref[...])
    z = zext[:ts]

    za = _gelu_tanh(z[:, :2 * A_WIDTH])
    u = za[:, :A_WIDTH]
    v = _layer_norm(za[:, A_WIDTH:], vng_ref[...], vnb_ref[...]).astype(MXU_DTYPE)
    bst = bst_ref[...]
    n_chunks = ts // CHUNK

    def chunks_on_lanes(a, rows, col0, width):
        return jnp.concatenate([a[c * CHUNK:c * CHUNK + rows, col0:col0 + width] for c in range(n_chunks)], axis=1)

    def chunks_on_rows(a, width):
        return jnp.concatenate([a[:, c * width:(c + 1) * width] for c in range(n_chunks)], axis=0)

    heads = [_mdot(ws_ref[hh], chunks_on_lanes(v, CHUNK, hh * HEAD_DIM, HEAD_DIM)) + bst[:, hh:hh + 1]
             for hh in range(A_HEADS)]
    a_out = u * jnp.concatenate([chunks_on_rows(o, HEAD_DIM) for o in heads], axis=1)

    p = z[:, 2 * A_WIDTH:]
    pall = jnp.concatenate([zext[ts:ts + POOL_HALO, 2 * A_WIDTH:], p,
                            zext[ts + POOL_HALO:ts + 2 * POOL_HALO, 2 * A_WIDTH:]], axis=0)
    p_hi = pall.astype(MXU_DTYPE)
    p_lo = (pall - p_hi.astype(F32)).astype(MXU_DTYPE)
    b_cols = []
    for g in range(len(POOL_WINDOWS)):
        seg_rows = CHUNK + 2 * POOL_HALO
        win = (_mdot(band_ref[g], chunks_on_lanes(p_hi, seg_rows, g * GROUP_DIM, GROUP_DIM))
               + _mdot(band_ref[g], chunks_on_lanes(p_lo, seg_rows, g * GROUP_DIM, GROUP_DIM)))
        pooled = chunks_on_rows(win, GROUP_DIM) * inv_ref[0, g] - p[:, g * GROUP_DIM:(g + 1) * GROUP_DIM]
        b_cols.append(_mdot(pooled.astype(MXU_DTYPE), wpool_ref[g]))
    b_out = jnp.concatenate(b_cols, axis=1) * pscale_ref[...]

    mix_in = jnp.concatenate([a_out, b_out], axis=1).astype(MXU_DTYPE)
    mdp = modp_ref[0]
    h2_prev, lt_prev = _ln1_router(res_ref[1 - slot], mdp[3:4], mdp[4:5], ln1g_ref, ln1b_ref, wr_ref, rb_ref,
                                   x1_ref)
    _route_and_pack(h2_prev, lt_prev, rows_ref, rid_ref)

    mix = _mdot(mix_in, wout_ref[...])
    res_ref[slot] = ALPHA * xt + g1 * mix


def _ln1_router(res, sh2, sc2, ln1g_ref, ln1b_ref, wr_ref, rb_ref, x1_ref):
    x1 = _layer_norm(res, ln1g_ref[...], ln1b_ref[...])
    x1_ref[...] = x1
    h2 = x1 * (1.0 + sc2) + sh2
    h2_hi = h2.astype(MXU_DTYPE)
    h2_lo = (h2 - h2_hi.astype(F32)).astype(MXU_DTYPE)
    l1 = lax.dot_general(wr_ref[...], h2_hi, _NT, preferred_element_type=F32)
    l2 = lax.dot_general(wr_ref[0:ROUTE_PAD], h2_lo, _NT, preferred_element_type=F32)
    return h2, l1[:ROUTE_PAD] + l1[ROUTE_PAD:] + l2 + rb_ref[...]


def _route_and_pack(h2, lt, rows_ref, rid_ref):
    ts = h2.shape[0]

    def row(r):
        return lt[r:r + 1, :]

    gl = [row(r) for r in range(N_GROUPS)]
    gmax = jnp.maximum(jnp.maximum(gl[0], gl[1]), jnp.maximum(gl[2], gl[3]))
    gidx = jnp.where(gl[0] == gmax, 0, jnp.where(gl[1] == gmax, 1, jnp.where(gl[2] == gmax, 2, 3)))
    gsum = (jnp.exp(gl[0] - gmax) + jnp.exp(gl[1] - gmax)) + (jnp.exp(gl[2] - gmax) + jnp.exp(gl[3] - gmax))
    gw = 1.0 / gsum
    ev = [jnp.where(gidx == 0, row(N_GROUPS + j),
                    jnp.where(gidx == 1, row(N_GROUPS + EPG + j),
                              jnp.where(gidx == 2, row(N_GROUPS + 2 * EPG + j), row(N_GROUPS + 3 * EPG + j))))
          for j in range(EPG)]

    def top1(vals):
        m = vals[0]
        for t in vals[1:]:
            m = jnp.maximum(m, t)
        idx = jnp.full(m.shape, EPG - 1, I32)
        for j in range(EPG - 2, -1, -1):
            idx = jnp.where(vals[j] == m, j, idx)
        return m, idx

    v1, j1 = top1(ev)
    v2, j2 = top1([jnp.where(j1 == j, -jnp.inf, ev[j]) for j in range(EPG)])
    t2 = jnp.exp(v2 - v1)
    den = 1.0 + t2
    w1 = (1.0 / den) * gw
    w2 = (t2 / den) * gw
    first = j1 < j2
    ea = jnp.minimum(j1, j2)
    eb = jnp.maximum(j1, j2)
    wa = jnp.where(first, w1, w2)
    wb = jnp.where(first, w2, w1)
    cls = gidx * PAIRS + jnp.right_shift(ea * (2 * EPG - 1 - ea), 1) + (eb - ea - 1)

    r8 = lax.broadcasted_iota(I32, (8, ts), 0)
    rid_ref[...] = jnp.where(r8 == 0, cls, jnp.where(r8 == 1, gidx * EPG + ea,
                                                      jnp.where(r8 == 2, gidx * EPG + eb, 0)))
    w8 = jnp.where(r8 == 0, wa, jnp.where(r8 == 1, wb, 0.0))
    w128 = jnp.concatenate([w8, jnp.zeros((LANES - 8, ts), F32)], axis=0)
    bits = lax.bitcast_convert_type(h2.astype(jnp.bfloat16).astype(F32), U32)
    half = D_MODEL // 2
    for k in range(PACK_ROWS):
        lo = jnp.right_shift(bits[:, k * LANES:(k + 1) * LANES], 16)
        hi = jnp.bitwise_and(bits[:, half + k * LANES:half + (k + 1) * LANES], jnp.uint32(HI_MASK))
        rows_ref[pl.ds(k, ts, stride=SUBLANES), :] = jnp.bitwise_or(lo, hi)
    rows_ref[pl.ds(PACK_ROWS, ts, stride=SUBLANES), :] = lax.bitcast_convert_type(w128.T, U32)
    for k in range(PACK_ROWS + 1, SUBLANES):
        rows_ref[pl.ds(k, ts, stride=SUBLANES), :] = jnp.zeros((ts, LANES), U32)


def _inv_population_table():
    r = np.arange(TS)
    out = np.empty((3, len(POOL_WINDOWS), TS, LANES), np.float32)
    for case in range(3):
        for g, w in enumerate(POOL_WINDOWS):
            lo = np.maximum(r - w // 2, 0) if case == 0 else r - w // 2
            hi = np.minimum(r + w // 2, TS) if case == 2 else r + w // 2
            out[case, g] = (1.0 / (hi - lo).astype(np.float64)).astype(np.float32)[:, None]
    return out


def _mixer_call(x_prompt, x_sample, mod, weights):
    bp, sp, d = x_prompt.shape
    bs, ss, _ = x_sample.shape
    nsp, nss = sp // TS, ss // TS
    assert nsp >= 2 and nss >= 2, "a tile is the first or the last of its sequence, not both"
    ntp, nts = bp * nsp, bs * nss
    n_tok = bp * sp + bs * ss
    hb = TS // POOL_HALO

    def p_tile(i):
        t = jnp.minimum(i, ntp - 1)
        return t // nsp, t % nsp

    def s_tile(i):
        t = jnp.clip(i - ntp, 0, nts - 1)
        return t // nss, t % nss

    def specs(tile_fn, seq):
        def cur(i):
            b, s = tile_fn(i)
            return (b, s, 0)

        def prev(i):
            b, s = tile_fn(i)
            return (b, jnp.maximum(s * hb - 1, 0), 0)

        def nxt(i):
            b, s = tile_fn(i)
            return (b, jnp.minimum((s + 1) * hb, seq // POOL_HALO - 1), 0)

        return [pl.BlockSpec((1, TS, d), cur), pl.BlockSpec((1, POOL_HALO, d), prev),
                pl.BlockSpec((1, POOL_HALO, d), nxt)]

    def mod_map(i):
        return (jnp.where(i < ntp, p_tile(i)[0], bp + s_tile(i)[0]), 0, 0)

    def inv_map(i):
        s = jnp.where(i < ntp, p_tile(i)[1], s_tile(i)[1])
        last = jnp.where(i < ntp, nsp - 1, nss - 1)
        return (jnp.where(s == 0, 0, jnp.where(s == last, 2, 1)), 0, 0, 0)

    def const(w):
        return pl.BlockSpec(w.shape, lambda i, nd=w.ndim: (0,) * nd)

    def prev_tile(i):
        return jnp.maximum(i - 1, 0)

    inv_tab = jnp.asarray(_inv_population_table())
    in_specs = (specs(p_tile, sp) + specs(s_tile, ss)
                + [pl.BlockSpec((1, N_MOD, d), mod_map),
                   pl.BlockSpec((1, N_MOD, d), lambda i: mod_map(prev_tile(i))),
                   pl.BlockSpec((1,) + inv_tab.shape[1:], inv_map)]
                + [const(w) for w in weights])
    out_shape = (jax.ShapeDtypeStruct((n_tok, d), F32),
                 jax.ShapeDtypeStruct((n_tok * SUBLANES, LANES), U32),
                 jax.ShapeDtypeStruct((8, n_tok), I32))
    out_specs = (pl.BlockSpec((TS, d), lambda i: (prev_tile(i), 0)),
                 pl.BlockSpec((TS * SUBLANES, LANES), lambda i: (prev_tile(i), 0)),
                 pl.BlockSpec((8, TS), lambda i: (0, prev_tile(i))))
    return pl.pallas_call(
        functools.partial(_mixer_kernel, prompt_tiles=ntp, prompt_seq=sp, sample_seq=ss),
        out_shape=out_shape,
        grid=(ntp + nts + 1,),
        in_specs=in_specs,
        out_specs=out_specs,
        scratch_shapes=[pltpu.VMEM((2, TS, d), F32)],
        compiler_params=pltpu.CompilerParams(dimension_semantics=("arbitrary",),
                                             vmem_limit_bytes=VMEM_LIMIT),
        name="mixer_ln1_route",
    )(x_prompt, x_prompt, x_prompt, x_sample, x_sample, x_sample, mod, mod, inv_tab, *weights)


def _rank_kernel(rid_ref, u_ref, tri_ref, dest_ref, meta_ref, cnt_ref, base_ref, *, nb_pad):
    phase = pl.program_id(0)
    j = pl.program_id(1)
    cls_iota = lax.broadcasted_iota(I32, (CLS_PAD, SUB), 0)

    @pl.when((phase == 0) & (j == 0))
    def _():
        cnt_ref[...] = jnp.zeros_like(cnt_ref)

    @pl.when(phase == 0)
    def _():
        acc = cnt_ref[...]
        for sb in range(BL // SUB):
            ids = rid_ref[0:1, sb * SUB:(sb + 1) * SUB]
            acc = acc + jnp.sum((cls_iota == ids).astype(F32), axis=1, keepdims=True)
        cnt_ref[...] = acc

    @pl.when((phase == 1) & (j == 0))
    def _():
        cnt = jnp.broadcast_to(cnt_ref[...], (CLS_PAD, CLS_PAD))
        nblk = jnp.floor((cnt + (RB - 1)) * (1.0 / RB))
        cum = jnp.dot(tri_ref[...], nblk, precision=lax.Precision.HIGHEST, preferred_element_type=F32)
        base_ref[...] = (cum[:, 0:1] - nblk[:, 0:1]) * RB
        blk = lax.broadcasted_iota(I32, (CLS_PAD, nb_pad), 1).astype(F32)
        bcls = jnp.sum((cum[:, 0:1] <= blk).astype(F32), axis=0, keepdims=True)
        bcls = jnp.minimum(bcls, N_CLASSES - 1).astype(I32)
        nused = jnp.broadcast_to(cum[CLS_PAD - 1:CLS_PAD, 0:1], (1, nb_pad)).astype(I32)
        cum_l = jnp.transpose(cum)[0:1].astype(I32)
        nblk_l = jnp.transpose(nblk)[0:1].astype(I32)
        pad = jnp.zeros((1, nb_pad - CLS_PAD), I32)
        r8 = lax.broadcasted_iota(I32, (8, nb_pad), 0)
        cum_row = jnp.concatenate([cum_l, pad], axis=1)
        nblk_row = jnp.concatenate([nblk_l, pad], axis=1)
        meta_ref[...] = jnp.where(r8 == 0, bcls, jnp.where(r8 == 1, nused,
                                  jnp.where(r8 == 2, cum_row, jnp.where(r8 == 3, nblk_row, 0))))

    @pl.when(phase == 1)
    def _():
        base = base_ref[...]
        for sb in range(BL // SUB):
            ids = rid_ref[0:1, sb * SUB:(sb + 1) * SUB]
            hit = cls_iota == ids
            incl = _mdot(hit.astype(MXU_DTYPE), u_ref[...])
            slot = jnp.sum(jnp.where(hit, base + incl - 1.0, 0.0), axis=0, keepdims=True)
            dest_ref[0:1, sb * SUB:(sb + 1) * SUB] = slot.astype(I32)
            base = base + incl[:, SUB - 1:SUB]
        base_ref[...] = base


def _rank_call(rid, u_mat, tri, nb_pad):
    n_tok = rid.shape[1]
    nj = n_tok // BL
    return pl.pallas_call(
        functools.partial(_rank_kernel, nb_pad=nb_pad),
        out_shape=(jax.ShapeDtypeStruct((1, n_tok), I32), jax.ShapeDtypeStruct((8, nb_pad), I32)),
        grid=(2, nj),
        in_specs=[pl.BlockSpec((8, BL), lambda p, j: (0, j)),
                  pl.BlockSpec(u_mat.shape, lambda p, j: (0, 0)),
                  pl.BlockSpec(tri.shape, lambda p, j: (0, 0))],
        out_specs=(pl.BlockSpec((1, BL), lambda p, j: (0, p * j)),
                   pl.BlockSpec((8, nb_pad), lambda p, j: (0, 0))),
        scratch_shapes=[pltpu.VMEM((CLS_PAD, 1), F32), pltpu.VMEM((CLS_PAD, 1), F32)],
        compiler_params=pltpu.CompilerParams(dimension_semantics=("arbitrary", "arbitrary"),
                                             vmem_limit_bytes=VMEM_LIMIT),
        name="rank_tokens",
    )(rid, u_mat, tri)


def _dispatch_kernel(cum_ref, nblk_ref, nused_ref, dest_hbm, rows_hbm, out_hbm, idx_smem, rows_buf, zero_ref,
                     sem_idx, sem_in, sem_row, sem_zero):
    i = pl.program_id(0)
    n_steps = pl.num_programs(0)
    slot = i % 2
    tile_rows = TD * SUBLANES
    blk_rows = RB * SUBLANES
    n_blocks = out_hbm.shape[0] // blk_rows

    def fetch(step, s):
        start = pl.multiple_of(step * tile_rows, tile_rows)
        return (pltpu.make_async_copy(dest_hbm.at[step], idx_smem.at[s], sem_idx.at[s]),
                pltpu.make_async_copy(rows_hbm.at[pl.ds(start, tile_rows)], rows_buf.at[s], sem_in.at[s]))

    def scattered(s):
        return pltpu.make_async_copy(rows_buf.at[s], out_hbm.at[pl.ds(0, tile_rows)], sem_row.at[s])

    def zero_block(blk):
        start = pl.multiple_of(blk * blk_rows, blk_rows)
        return pltpu.make_async_copy(zero_ref, out_hbm.at[pl.ds(start, blk_rows)], sem_zero)

    @pl.when(i == 0)
    def _():
        zero_ref[...] = jnp.zeros_like(zero_ref)

        def start(c, carry):
            @pl.when(nblk_ref[c] > 0)
            def _():
                zero_block(cum_ref[c] - 1).start()
            return carry

        def wait(c, carry):
            @pl.when(nblk_ref[c] > 0)
            def _():
                zero_block(cum_ref[c] - 1).wait()
            return carry

        def start_tail(blk, carry):
            zero_block(blk).start()
            return carry

        def wait_tail(blk, carry):
            zero_block(blk).wait()
            return carry

        lax.fori_loop(0, N_CLASSES, start, 0)
        lax.fori_loop(nused_ref[0], n_blocks, start_tail, 0)
        lax.fori_loop(0, N_CLASSES, wait, 0)
        lax.fori_loop(nused_ref[0], n_blocks, wait_tail, 0)
        for cp in fetch(0, 0):
            cp.start()

    for cp in fetch(i, slot):
        cp.wait()

    @pl.when(i > 0)
    def _():
        scattered(1 - slot).wait()

    @pl.when(i + 1 < n_steps)
    def _():
        for cp in fetch(i + 1, 1 - slot):
            cp.start()

    def scatter(g, carry):
        for u in range(ROW_UNROLL):
            t = g * ROW_UNROLL + u
            src = pl.multiple_of(t * SUBLANES, SUBLANES)
            dst = pl.multiple_of(idx_smem[slot, t] * SUBLANES, SUBLANES)
            pltpu.make_async_copy(rows_buf.at[slot, pl.ds(src, SUBLANES)], out_hbm.at[pl.ds(dst, SUBLANES)],
                                  sem_row.at[slot]).start(priority=u % 2)
        return carry

    lax.fori_loop(0, TD // ROW_UNROLL, scatter, 0)

    @pl.when(i + 1 == n_steps)
    def _():
        scattered(slot).wait()


def _dispatch_call(cum_cls, nblk_cls, nused, dest2d, rows, n_rows):
    n_tok = rows.shape[0] // SUBLANES
    grid_spec = pltpu.PrefetchScalarGridSpec(
        num_scalar_prefetch=3,
        grid=(n_tok // TD,),
        in_specs=[pl.BlockSpec(memory_space=pl.ANY), pl.BlockSpec(memory_space=pl.ANY)],
        out_specs=pl.BlockSpec(memory_space=pl.ANY),
        scratch_shapes=[pltpu.SMEM((2, TD), I32), pltpu.VMEM((2, TD * SUBLANES, LANES), U32),
                        pltpu.VMEM((RB * SUBLANES, LANES), U32),
                        pltpu.SemaphoreType.DMA((2,)), pltpu.SemaphoreType.DMA((2,)),
                        pltpu.SemaphoreType.DMA((2,)), pltpu.SemaphoreType.DMA],
    )
    return pl.pallas_call(
        _dispatch_kernel,
        out_shape=jax.ShapeDtypeStruct((n_rows * SUBLANES, LANES), U32),
        grid_spec=grid_spec,
        compiler_params=pltpu.CompilerParams(dimension_semantics=("arbitrary",),
                                             vmem_limit_bytes=VMEM_LIMIT),
        name="dispatch_rows",
    )(cum_cls, nblk_cls, nused, dest2d, rows)


def _expert_kernel(ea_ref, eb_ref, nused_ref, rows_ref, wgu_a_ref, wd_a_ref, wgu_b_ref, wd_b_ref, y_ref,
                   xs0_ref, xs1_ref, gs0_ref, gs1_ref):
    b = pl.program_id(0)
    nused = nused_ref[0]
    xs = (xs0_ref, xs1_ref)
    gs = (gs0_ref, gs1_ref)

    @pl.when(b == 0)
    def _():
        xs1_ref[...] = jnp.zeros_like(xs1_ref)
        gs1_ref[...] = jnp.zeros_like(gs1_ref)

    def step(slot):
        def tile_row(k):
            return rows_ref[pl.ds(k, RB, stride=SUBLANES), :]

        x = xs[1 - slot][...]
        gate = gs[1 - slot][...]
        gu_a = _mdot(x, wgu_a_ref[0])

        words = [tile_row(k) for k in range(PACK_ROWS)]
        lo = [lax.bitcast_convert_type(jnp.left_shift(w, 16), F32) for w in words]
        hi = [lax.bitcast_convert_type(jnp.bitwise_and(w, jnp.uint32(HI_MASK)), F32) for w in words]
        xs[slot][...] = jnp.concatenate(lo + hi, axis=1).astype(MXU_DTYPE)
        gs[slot][...] = lax.bitcast_convert_type(tile_row(PACK_ROWS), F32)

        def down(gu, wd_ref):
            act = jax.nn.silu(gu[:, :D_EXPERT]) * gu[:, D_EXPERT:]
            return _mdot(act.astype(MXU_DTYPE), wd_ref[0])

        y = down(gu_a, wd_a_ref) * gate[:, 0:1] + down(_mdot(x, wgu_b_ref[0]), wd_b_ref) * gate[:, 1:2]
        for k in range(Y_ROWS):
            y_ref[pl.ds(k, RB, stride=SUBLANES), :] = y[:, k * LANES:(k + 1) * LANES]

    for slot in range(2):
        @pl.when((b <= nused) & (b % 2 == slot))
        def _(slot=slot):
            step(slot)

    @pl.when(b > nused)
    def _():
        y_ref[...] = jnp.zeros_like(y_ref)


def _expert_call(blk_ea, blk_eb, nused, rows, w_gate_up, w_down):
    n_rows = rows.shape[0] // SUBLANES
    nblk = n_rows // RB

    def live(b, nu):
        return jnp.maximum(jnp.minimum(b, nu[0] - 1), 0)

    def prev(b):
        return jnp.maximum(b - 1, 0)

    grid_spec = pltpu.PrefetchScalarGridSpec(
        num_scalar_prefetch=3,
        grid=(nblk + 1,),
        in_specs=[pl.BlockSpec((RB * SUBLANES, LANES), lambda b, ea, eb, nu: (live(b, nu), 0)),
                  pl.BlockSpec((1,) + w_gate_up.shape[1:], lambda b, ea, eb, nu: (ea[live(prev(b), nu)], 0, 0)),
                  pl.BlockSpec((1,) + w_down.shape[1:], lambda b, ea, eb, nu: (ea[live(prev(b), nu)], 0, 0)),
                  pl.BlockSpec((1,) + w_gate_up.shape[1:], lambda b, ea, eb, nu: (eb[live(prev(b), nu)], 0, 0)),
                  pl.BlockSpec((1,) + w_down.shape[1:], lambda b, ea, eb, nu: (eb[live(prev(b), nu)], 0, 0))],
        out_specs=pl.BlockSpec((RB * Y_ROWS, LANES), lambda b, ea, eb, nu: (prev(b), 0)),
        scratch_shapes=[pltpu.VMEM((RB, D_MODEL), MXU_DTYPE), pltpu.VMEM((RB, D_MODEL), MXU_DTYPE),
                        pltpu.VMEM((RB, LANES), F32), pltpu.VMEM((RB, LANES), F32)],
    )
    return pl.pallas_call(
        _expert_kernel,
        out_shape=jax.ShapeDtypeStruct((n_rows * Y_ROWS, LANES), F32),
        grid_spec=grid_spec,
        compiler_params=pltpu.CompilerParams(dimension_semantics=("arbitrary",),
                                             vmem_limit_bytes=VMEM_LIMIT),
        name="pair_experts",
    )(blk_ea, blk_eb, nused, rows, w_gate_up, w_down, w_gate_up, w_down)


def _combine_kernel(dest_hbm, y_hbm, x1_ref, mod_ref, g_ref, b_ref, o_ref, idx_smem, ybuf, sem_idx, sem_row,
                    *, tile_off):
    ns = pl.num_programs(1)
    i = pl.program_id(0) * ns + pl.program_id(1)
    n_steps = pl.num_programs(0) * ns
    slot = i % 2

    def idx_copy(step, s):
        return pltpu.make_async_copy(dest_hbm.at[tile_off + step], idx_smem.at[s], sem_idx.at[s])

    def issue_gather(s):
        def gather(g, carry):
            for u in range(ROW_UNROLL):
                t = g * ROW_UNROLL + u
                src = pl.multiple_of(idx_smem[s, t] * Y_ROWS, Y_ROWS)
                dst = pl.multiple_of(t * Y_ROWS, Y_ROWS)
                pltpu.make_async_copy(y_hbm.at[pl.ds(src, Y_ROWS)], ybuf.at[s, pl.ds(dst, Y_ROWS)],
                                      sem_row.at[s]).start(priority=u % 2)
            return carry

        lax.fori_loop(0, TD // ROW_UNROLL, gather, 0)

    @pl.when(i == 0)
    def _():
        idx_copy(0, 0).start()
        idx_copy(0, 0).wait()
        issue_gather(0)

        @pl.when(n_steps > 1)
        def _():
            idx_copy(1, 1).start()

    @pl.when(i + 1 < n_steps)
    def _():
        idx_copy(i + 1, 1 - slot).wait()
        issue_gather(1 - slot)

    @pl.when(i + 2 < n_steps)
    def _():
        idx_copy(i + 2, slot).start()

    pltpu.make_async_copy(y_hbm.at[pl.ds(0, TD * Y_ROWS)], ybuf.at[slot], sem_row.at[slot]).wait()

    y = jnp.concatenate([ybuf[slot, pl.ds(k, TD, stride=Y_ROWS), :] for k in range(Y_ROWS)], axis=1)
    g2 = mod_ref[0][5:6]
    o_ref[0] = _layer_norm(ALPHA * x1_ref[...] + g2 * y, g_ref[...], b_ref[...])


def _combine_call(dest2d, y_rows, x1, mod, mod_off, tile_off, ln2g, ln2b, bsz, seq):
    d = x1.shape[1]
    ns = seq // TD
    return pl.pallas_call(
        functools.partial(_combine_kernel, tile_off=tile_off),
        out_shape=jax.ShapeDtypeStruct((bsz, seq, d), F32),
        grid=(bsz, ns),
        in_specs=[pl.BlockSpec(memory_space=pl.ANY),
                  pl.BlockSpec(memory_space=pl.ANY),
                  pl.BlockSpec((TD, d), lambda b, s: (tile_off + b * ns + s, 0)),
                  pl.BlockSpec((1, N_MOD, d), lambda b, s: (mod_off + b, 0, 0)),
                  pl.BlockSpec((1, d), lambda b, s: (0, 0)),
                  pl.BlockSpec((1, d), lambda b, s: (0, 0))],
        out_specs=pl.BlockSpec((1, TD, d), lambda b, s: (b, s, 0)),
        scratch_shapes=[pltpu.SMEM((2, TD), I32), pltpu.VMEM((2, TD * Y_ROWS, LANES), F32),
                        pltpu.SemaphoreType.DMA((2,)), pltpu.SemaphoreType.DMA((2,))],
        compiler_params=pltpu.CompilerParams(dimension_semantics=("arbitrary", "arbitrary"),
                                             vmem_limit_bytes=VMEM_LIMIT),
        name="combine_ln2",
    )(dest2d, y_rows, x1, mod, ln2g, ln2b)


def _band_matrices():
    rows = np.arange(CHUNK)[:, None]
    cols = np.arange(CHUNK + 2 * POOL_HALO)[None, :] - POOL_HALO
    return np.stack([((cols >= rows - w // 2) & (cols < rows + w // 2)) for w in POOL_WINDOWS]).astype(np.float32)


def _class_tables():
    ea, eb = [], []
    for g in range(N_GROUPS):
        for a in range(EPG):
            for b in range(a + 1, EPG):
                ea.append(g * EPG + a)
                eb.append(g * EPG + b)
    return np.asarray(ea, np.int32), np.asarray(eb, np.int32)


def _split_hi_lo(w):
    hi = w.astype(MXU_DTYPE)
    lo = (w - hi.astype(F32)).astype(MXU_DTYPE)
    return hi, lo


def kernel(x_prompt, x_sample, c_prompt, c_sample, w_ada, b_ada, w_in, v_norm_g, v_norm_b, w_spatial, b_spatial,
           w_pool, pool_scale, w_out, ln1_g, ln1_b, w_route_group, b_route_group, w_route_expert, b_route_expert,
           w_gate_up, w_down, ln2_g, ln2_b):
    assert w_ada.shape[0] == 1, "single-layer kernel"
    bp, sp, d = x_prompt.shape
    bs, ss, _ = x_sample.shape
    assert d == D_MODEL and sp % TS == 0 and ss % TS == 0
    n_prompt, n_sample = bp * sp, bs * ss
    n_tok = n_prompt + n_sample
    assert n_tok % BL == 0 and n_prompt % TD == 0

    c_all = jnp.concatenate([c_prompt, c_sample], axis=0)
    mod = _modulation(c_all, w_ada[0], b_ada).reshape(bp + bs, N_MOD, d)

    wr = jnp.concatenate([w_route_group[0], w_route_expert[0]], axis=1).T
    wr = jnp.pad(wr, ((0, ROUTE_PAD - N_ROUTE), (0, 0)))
    wr_hi, wr_lo = _split_hi_lo(wr)
    rbias = jnp.concatenate([b_route_group[0], b_route_expert[0].reshape(-1),
                             jnp.zeros((ROUTE_PAD - N_ROUTE,), F32)]).reshape(ROUTE_PAD, 1)
    row = lambda a: a.reshape(1, -1)
    weights = (w_in[0].astype(MXU_DTYPE), row(v_norm_g[0]), row(v_norm_b[0]), w_spatial[0].astype(MXU_DTYPE),
               b_spatial[0].T, jnp.asarray(_band_matrices(), MXU_DTYPE), w_pool[0].astype(MXU_DTYPE),
               row(pool_scale[0]), w_out[0].astype(MXU_DTYPE), row(ln1_g[0]), row(ln1_b[0]),
               jnp.concatenate([wr_hi, wr_lo], axis=0), rbias)

    x1, rows, rid = _mixer_call(x_prompt, x_sample, mod, weights)

    nblk = (n_tok + N_CLASSES * (RB - 1)) // RB
    nb_pad = -(-nblk // 128) * 128
    u_mat = jnp.asarray(np.triu(np.ones((SUB, SUB), np.float32)), MXU_DTYPE)
    tri = jnp.asarray(np.tril(np.ones((CLS_PAD, CLS_PAD), np.float32)))
    dest, meta = _rank_call(rid, u_mat, tri, nb_pad)
    dest2d = dest.reshape(n_tok // TD, TD)
    tab_a, tab_b = _class_tables()
    blk_cls = meta[0, :nblk]
    blk_ea = jnp.asarray(tab_a)[blk_cls]
    blk_eb = jnp.asarray(tab_b)[blk_cls]
    nused = meta[1, 0:1]
    cum_cls = meta[2, :CLS_PAD]
    nblk_cls = meta[3, :CLS_PAD]

    sorted_rows = _dispatch_call(cum_cls, nblk_cls, nused, dest2d, rows, nblk * RB)
    y_rows = _expert_call(blk_ea, blk_eb, nused, sorted_rows, w_gate_up[0].astype(MXU_DTYPE),
                          w_down[0].astype(MXU_DTYPE))

    ln2g, ln2b = row(ln2_g[0]), row(ln2_b[0])
    y_prompt = _combine_call(dest2d, y_rows, x1, mod, 0, 0, ln2g, ln2b, bp, sp)
    y_sample = _combine_call(dest2d, y_rows, x1, mod, bp, n_prompt // TD, ln2g, ln2b, bs, ss)
    return (y_prompt, y_sample)
```

```python
import functools

import numpy as np
import jax
import jax.numpy as jnp
from jax import lax
from jax.experimental import pallas as pl
from jax.experimental.pallas import tpu as pltpu

F32 = jnp.float32
I32 = jnp.int32
MXU_DTYPE = jnp.bfloat16

D_MODEL = 1024
A_WIDTH = 512
B_WIDTH = 512
CHUNK = 128
A_HEADS = 4
HEAD_DIM = A_WIDTH // A_HEADS
POOL_WINDOWS = (2, 4, 8, 16)
POOL_HALO = 8
GROUP_DIM = B_WIDTH // len(POOL_WINDOWS)
N_GROUPS = 4
EPG = 8
N_EXPERTS = N_GROUPS * EPG
D_EXPERT = 512
N_MOD = 6
LN_EPS = 1e-5
ALPHA = 2.0 ** 0.25

PAIRS = EPG * (EPG - 1) // 2
N_CLASSES = N_GROUPS * PAIRS
CLS_PAD = 128
N_ROUTE = N_GROUPS + N_EXPERTS
ROUTE_PAD = 40
U32 = jnp.uint32
LANES = 128
SUBLANES = 8
PACK_ROWS = D_MODEL // (2 * LANES)
Y_ROWS = D_MODEL // LANES
assert PACK_ROWS < SUBLANES and Y_ROWS == SUBLANES
HI_MASK = 0xFFFF0000

TS = 512
TD = 512
RB = 256
ROW_UNROLL = 8
BL = 2048
SUB = 256
VMEM_LIMIT = 56 * 1024 * 1024

_NT = (((1,), (1,)), ((), ()))

_GELU_K1 = -2.0 * (2.0 / np.pi) ** 0.5
_GELU_K3 = _GELU_K1 * 0.044715


def _gelu_tanh(x):
    return x / (1.0 + jnp.exp(x * (_GELU_K1 + _GELU_K3 * (x * x))))


def _layer_norm(x, g, b):
    mu = jnp.mean(x, axis=-1, keepdims=True)
    xc = x - mu
    var = jnp.mean(xc * xc, axis=-1, keepdims=True)
    return xc * lax.rsqrt(var + LN_EPS) * g + b


def _mdot(a, b):
    return jnp.dot(a, b, preferred_element_type=F32)


def _mod_kernel(c_ref, w_ref, b_ref, o_ref):
    a = jax.nn.silu(c_ref[...])
    o_ref[...] = jnp.dot(a, w_ref[...], precision=lax.Precision.HIGHEST,
                         preferred_element_type=F32) + b_ref[...]


def _modulation(c_all, w_ada, b_ada):
    nb, d = c_all.shape
    n = w_ada.shape[1]
    bn = 1536
    return pl.pallas_call(
        _mod_kernel,
        out_shape=jax.ShapeDtypeStruct((nb, n), F32),
        grid=(n // bn,),
        in_specs=[pl.BlockSpec((nb, d), lambda j: (0, 0)),
                  pl.BlockSpec((d, bn), lambda j: (0, j)),
                  pl.BlockSpec((1, bn), lambda j: (0, j))],
        out_specs=pl.BlockSpec((nb, bn), lambda j: (0, j)),
        compiler_params=pltpu.CompilerParams(dimension_semantics=("arbitrary",),
                                             vmem_limit_bytes=VMEM_LIMIT),
        name="adaln_mod",
    )(c_all, w_ada, b_ada)


def _mixer_kernel(xp_ref, xpp_ref, xpn_ref, xs_ref, xsp_ref, xsn_ref, mod_ref, modp_ref, inv_ref, win_ref, vng_ref,
                  vnb_ref, ws_ref, bst_ref, band_ref, wpool_ref, pscale_ref, wout_ref, ln1g_ref, ln1b_ref, wr_ref,
                  rb_ref, x1_ref, rows_ref, rid_ref, res_ref, *, prompt_tiles, prompt_seq, sample_seq):
    ts = xp_ref.shape[1]
    i = pl.program_id(0)
    slot = i % 2

    @pl.when(i == 0)
    def _():
        res_ref[1] = jnp.zeros(res_ref.shape[1:], F32)

    is_p = i < prompt_tiles
    ns = jnp.where(is_p, prompt_seq // ts, sample_seq // ts)
    s = jnp.where(is_p, i, i - prompt_tiles) % ns
    md = mod_ref[0]
    sh1, sc1, g1 = md[0:1], md[1:2], md[2:3]

    xt = jnp.where(is_p, xp_ref[0], xs_ref[0])
    h = xt * (1.0 + sc1) + sh1
    hp = jnp.where(s > 0, jnp.where(is_p, xpp_ref[0], xsp_ref[0]) * (1.0 + sc1) + sh1, 0.0)
    hn = jnp.where(s < ns - 1, jnp.where(is_p, xpn_ref[0], xsn_ref[0]) * (1.0 + sc1) + sh1, 0.0)
    hext = jnp.concatenate([h, hp, hn], axis=0).astype(MXU_DTYPE)
    zext = _mdot(hext, win_ref[...])
    z = zext[:ts]

    za = _gelu_tanh(z[:, :2 * A_WIDTH])
    u = za[:, :A_WIDTH]
    v = _layer_norm(za[:, A_WIDTH:], vng_ref[...], vnb_ref[...]).astype(MXU_DTYPE)
    bst = bst_ref[...]
    n_chunks = ts // CHUNK

    def chunks_on_lanes(a, rows, col0, width):
        return jnp.concatenate([a[c * CHUNK:c * CHUNK + rows, col0:col0 + width] for c in range(n_chunks)], axis=1)

    def chunks_on_rows(a, width):
        return jnp.concatenate([a[:, c * width:(c + 1) * width] for c in range(n_chunks)], axis=0)

    heads = [_mdot(ws_ref[hh], chunks_on_lanes(v, CHUNK, hh * HEAD_DIM, HEAD_DIM)) + bst[:, hh:hh + 1]
             for hh in range(A_HEADS)]
    a_out = u * jnp.concatenate([chunks_on_rows(o, HEAD_DIM) for o in heads], axis=1)

    p = z[:, 2 * A_WIDTH:]
    pall = jnp.concatenate([zext[ts:ts + POOL_HALO, 2 * A_WIDTH:], p,
                            zext[ts + POOL_HALO:ts + 2 * POOL_HALO, 2 * A_WIDTH:]], axis=0)
    p_hi = pall.astype(MXU_DTYPE)
    p_lo = (pall - p_hi.astype(F32)).astype(MXU_DTYPE)
    b_cols = []
    for g in range(len(POOL_WINDOWS)):
        seg_rows = CHUNK + 2 * POOL_HALO
        win = (_mdot(band_ref[g], chunks_on_lanes(p_hi, seg_rows, g * GROUP_DIM, GROUP_DIM))
               + _mdot(band_ref[g], chunks_on_lanes(p_lo, seg_rows, g * GROUP_DIM, GROUP_DIM)))
        pooled = chunks_on_rows(win, GROUP_DIM) * inv_ref[0, g] - p[:, g * GROUP_DIM:(g + 1) * GROUP_DIM]
        b_cols.append(_mdot(pooled.astype(MXU_DTYPE), wpool_ref[g]))
    b_out = jnp.concatenate(b_cols, axis=1) * pscale_ref[...]

    mix_in = jnp.concatenate([a_out, b_out], axis=1).astype(MXU_DTYPE)
    mdp = modp_ref[0]
    h2_prev, lt_prev = _ln1_router(res_ref[1 - slot], mdp[3:4], mdp[4:5], ln1g_ref, ln1b_ref, wr_ref, rb_ref,
                                   x1_ref)
    _route_and_pack(h2_prev, lt_prev, rows_ref, rid_ref)

    mix = _mdot(mix_in, wout_ref[...])
    res_ref[slot] = ALPHA * xt + g1 * mix


def _ln1_router(res, sh2, sc2, ln1g_ref, ln1b_ref, wr_ref, rb_ref, x1_ref):
    x1 = _layer_norm(res, ln1g_ref[...], ln1b_ref[...])
    x1_ref[...] = x1
    h2 = x1 * (1.0 + sc2) + sh2
    h2_hi = h2.astype(MXU_DTYPE)
    h2_lo = (h2 - h2_hi.astype(F32)).astype(MXU_DTYPE)
    l1 = lax.dot_general(wr_ref[...], h2_hi, _NT, preferred_element_type=F32)
    l2 = lax.dot_general(wr_ref[0:ROUTE_PAD], h2_lo, _NT, preferred_element_type=F32)
    return h2, l1[:ROUTE_PAD] + l1[ROUTE_PAD:] + l2 + rb_ref[...]


def _route_and_pack(h2, lt, rows_ref, rid_ref):
    ts = h2.shape[0]

    def row(r):
        return lt[r:r + 1, :]

    gl = [row(r) for r in range(N_GROUPS)]
    gmax = jnp.maximum(jnp.maximum(gl[0], gl[1]), jnp.maximum(gl[2], gl[3]))
    gidx = jnp.where(gl[0] == gmax, 0, jnp.where(gl[1] == gmax, 1, jnp.where(gl[2] == gmax, 2, 3)))
    gsum = (jnp.exp(gl[0] - gmax) + jnp.exp(gl[1] - gmax)) + (jnp.exp(gl[2] - gmax) + jnp.exp(gl[3] - gmax))
    gw = 1.0 / gsum
    ev = [jnp.where(gidx == 0, row(N_GROUPS + j),
                    jnp.where(gidx == 1, row(N_GROUPS + EPG + j),
                              jnp.where(gidx == 2, row(N_GROUPS + 2 * EPG + j), row(N_GROUPS + 3 * EPG + j))))
          for j in range(EPG)]

    def top1(vals):
        m = vals[0]
        for t in vals[1:]:
            m = jnp.maximum(m, t)
        idx = jnp.full(m.shape, EPG - 1, I32)
        for j in range(EPG - 2, -1, -1):
            idx = jnp.where(vals[j] == m, j, idx)
        return m, idx

    v1, j1 = top1(ev)
    v2, j2 = top1([jnp.where(j1 == j, -jnp.inf, ev[j]) for j in range(EPG)])
    t2 = jnp.exp(v2 - v1)
    den = 1.0 + t2
    w1 = (1.0 / den) * gw
    w2 = (t2 / den) * gw
    first = j1 < j2
    ea = jnp.minimum(j1, j2)
    eb = jnp.maximum(j1, j2)
    wa = jnp.where(first, w1, w2)
    wb = jnp.where(first, w2, w1)
    cls = gidx * PAIRS + jnp.right_shift(ea * (2 * EPG - 1 - ea), 1) + (eb - ea - 1)

    r8 = lax.broadcasted_iota(I32, (8, ts), 0)
    rid_ref[...] = jnp.where(r8 == 0, cls, jnp.where(r8 == 1, gidx * EPG + ea,
                                                      jnp.where(r8 == 2, gidx * EPG + eb, 0)))
    w8 = jnp.where(r8 == 0, wa, jnp.where(r8 == 1, wb, 0.0))
    w128 = jnp.concatenate([w8, jnp.zeros((LANES - 8, ts), F32)], axis=0)
    bits = lax.bitcast_convert_type(h2.astype(jnp.bfloat16).astype(F32), U32)
    half = D_MODEL // 2
    for k in range(PACK_ROWS):
        lo = jnp.right_shift(bits[:, k * LANES:(k + 1) * LANES], 16)
        hi = jnp.bitwise_and(bits[:, half + k * LANES:half + (k + 1) * LANES], jnp.uint32(HI_MASK))
        rows_ref[pl.ds(k, ts, stride=SUBLANES), :] = jnp.bitwise_or(lo, hi)
    rows_ref[pl.ds(PACK_ROWS, ts, stride=SUBLANES), :] = lax.bitcast_convert_type(w128.T, U32)
    for k in range(PACK_ROWS + 1, SUBLANES):
        rows_ref[pl.ds(k, ts, stride=SUBLANES), :] = jnp.zeros((ts, LANES), U32)


def _inv_population_table():
    r = np.arange(TS)
    out = np.empty((3, len(POOL_WINDOWS), TS, LANES), np.float32)
    for case in range(3):
        for g, w in enumerate(POOL_WINDOWS):
            lo = np.maximum(r - w // 2, 0) if case == 0 else r - w // 2
            hi = np.minimum(r + w // 2, TS) if case == 2 else r + w // 2
            out[case, g] = (1.0 / (hi - lo).astype(np.float64)).astype(np.float32)[:, None]
    return out


def _mixer_call(x_prompt, x_sample, mod, weights):
    bp, sp, d = x_prompt.shape
    bs, ss, _ = x_sample.shape
    nsp, nss = sp // TS, ss // TS
    assert nsp >= 2 and nss >= 2, "a tile is the first or the last of its sequence, not both"
    ntp, nts = bp * nsp, bs * nss
    n_tok = bp * sp + bs * ss
    hb = TS // POOL_HALO

    def p_tile(i):
        t = jnp.minimum(i, ntp - 1)
        return t // nsp, t % nsp

    def s_tile(i):
        t = jnp.clip(i - ntp, 0, nts - 1)
        return t // nss, t % nss

    def specs(tile_fn, seq):
        def cur(i):
            b, s = tile_fn(i)
            return (b, s, 0)

        def prev(i):
            b, s = tile_fn(i)
            return (b, jnp.maximum(s * hb - 1, 0), 0)

        def nxt(i):
            b, s = tile_fn(i)
            return (b, jnp.minimum((s + 1) * hb, seq // POOL_HALO - 1), 0)

        return [pl.BlockSpec((1, TS, d), cur), pl.BlockSpec((1, POOL_HALO, d), prev),
                pl.BlockSpec((1, POOL_HALO, d), nxt)]

    def mod_map(i):
        return (jnp.where(i < ntp, p_tile(i)[0], bp + s_tile(i)[0]), 0, 0)

    def inv_map(i):
        s = jnp.where(i < ntp, p_tile(i)[1], s_tile(i)[1])
        last = jnp.where(i < ntp, nsp - 1, nss - 1)
        return (jnp.where(s == 0, 0, jnp.where(s == last, 2, 1)), 0, 0, 0)

    def const(w):
        return pl.BlockSpec(w.shape, lambda i, nd=w.ndim: (0,) * nd)

    def prev_tile(i):
        return jnp.maximum(i - 1, 0)

    inv_tab = jnp.asarray(_inv_population_table())
    in_specs = (specs(p_tile, sp) + specs(s_tile, ss)
                + [pl.BlockSpec((1, N_MOD, d), mod_map),
                   pl.BlockSpec((1, N_MOD, d), lambda i: mod_map(prev_tile(i))),
                   pl.BlockSpec((1,) + inv_tab.shape[1:], inv_map)]
                + [const(w) for w in weights])
    out_shape = (jax.ShapeDtypeStruct((n_tok, d), F32),
                 jax.ShapeDtypeStruct((n_tok * SUBLANES, LANES), U32),
                 jax.ShapeDtypeStruct((8, n_tok), I32))
    out_specs = (pl.BlockSpec((TS, d), lambda i: (prev_tile(i), 0)),
                 pl.BlockSpec((TS * SUBLANES, LANES), lambda i: (prev_tile(i), 0)),
                 pl.BlockSpec((8, TS), lambda i: (0, prev_tile(i))))
    return pl.pallas_call(
        functools.partial(_mixer_kernel, prompt_tiles=ntp, prompt_seq=sp, sample_seq=ss),
        out_shape=out_shape,
        grid=(ntp + nts + 1,),
        in_specs=in_specs,
        out_specs=out_specs,
        scratch_shapes=[pltpu.VMEM((2, TS, d), F32)],
        compiler_params=pltpu.CompilerParams(dimension_semantics=("arbitrary",),
                                             vmem_limit_bytes=VMEM_LIMIT),
        name="mixer_ln1_route",
    )(x_prompt, x_prompt, x_prompt, x_sample, x_sample, x_sample, mod, mod, inv_tab, *weights)


def _rank_kernel(rid_ref, u_ref, tri_ref, dest_ref, meta_ref, cnt_ref, base_ref, *, nb_pad):
    phase = pl.program_id(0)
    j = pl.program_id(1)
    cls_iota = lax.broadcasted_iota(I32, (CLS_PAD, SUB), 0)

    @pl.when((phase == 0) & (j == 0))
    def _():
        cnt_ref[...] = jnp.zeros_like(cnt_ref)

    @pl.when(phase == 0)
    def _():
        acc = cnt_ref[...]
        for sb in range(BL // SUB):
            ids = rid_ref[0:1, sb * SUB:(sb + 1) * SUB]
            acc = acc + jnp.sum((cls_iota == ids).astype(F32), axis=1, keepdims=True)
        cnt_ref[...] = acc

    @pl.when((phase == 1) & (j == 0))
    def _():
        cnt = jnp.broadcast_to(cnt_ref[...], (CLS_PAD, CLS_PAD))
        nblk = jnp.floor((cnt + (RB - 1)) * (1.0 / RB))
        cum = jnp.dot(tri_ref[...], nblk, precision=lax.Precision.HIGHEST, preferred_element_type=F32)
        base_ref[...] = (cum[:, 0:1] - nblk[:, 0:1]) * RB
        blk = lax.broadcasted_iota(I32, (CLS_PAD, nb_pad), 1).astype(F32)
        bcls = jnp.sum((cum[:, 0:1] <= blk).astype(F32), axis=0, keepdims=True)
        bcls = jnp.minimum(bcls, N_CLASSES - 1).astype(I32)
        nused = jnp.broadcast_to(cum[CLS_PAD - 1:CLS_PAD, 0:1], (1, nb_pad)).astype(I32)
        first = cum[:, 0:1] - nblk[:, 0:1]
        inside = (first <= blk) & (blk < cum[:, 0:1])
        fill = jnp.clip(cnt[:, 0:1] - (blk - first) * RB, 0.0, float(RB))
        valid = jnp.sum(jnp.where(inside, fill, 0.0), axis=0, keepdims=True).astype(I32)
        cum_l = jnp.transpose(cum)[0:1].astype(I32)
        nblk_l = jnp.transpose(nblk)[0:1].astype(I32)
        pad = jnp.zeros((1, nb_pad - CLS_PAD), I32)
        r8 = lax.broadcasted_iota(I32, (8, nb_pad), 0)
        cum_row = jnp.concatenate([cum_l, pad], axis=1)
        nblk_row = jnp.concatenate([nblk_l, pad], axis=1)
        meta_ref[...] = jnp.where(r8 == 0, bcls, jnp.where(r8 == 1, nused,
                                  jnp.where(r8 == 2, cum_row, jnp.where(r8 == 3, nblk_row,
                                                                        jnp.where(r8 == 4, valid, 0)))))

    @pl.when(phase == 1)
    def _():
        base = base_ref[...]
        for sb in range(BL // SUB):
            ids = rid_ref[0:1, sb * SUB:(sb + 1) * SUB]
            hit = cls_iota == ids
            incl = _mdot(hit.astype(MXU_DTYPE), u_ref[...])
            slot = jnp.sum(jnp.where(hit, base + incl - 1.0, 0.0), axis=0, keepdims=True)
            dest_ref[0:1, sb * SUB:(sb + 1) * SUB] = slot.astype(I32)
            base = base + incl[:, SUB - 1:SUB]
        base_ref[...] = base


def _rank_call(rid, u_mat, tri, nb_pad):
    n_tok = rid.shape[1]
    nj = n_tok // BL
    return pl.pallas_call(
        functools.partial(_rank_kernel, nb_pad=nb_pad),
        out_shape=(jax.ShapeDtypeStruct((1, n_tok), I32), jax.ShapeDtypeStruct((8, nb_pad), I32)),
        grid=(2, nj),
        in_specs=[pl.BlockSpec((8, BL), lambda p, j: (0, j)),
                  pl.BlockSpec(u_mat.shape, lambda p, j: (0, 0)),
                  pl.BlockSpec(tri.shape, lambda p, j: (0, 0))],
        out_specs=(pl.BlockSpec((1, BL), lambda p, j: (0, p * j)),
                   pl.BlockSpec((8, nb_pad), lambda p, j: (0, 0))),
        scratch_shapes=[pltpu.VMEM((CLS_PAD, 1), F32), pltpu.VMEM((CLS_PAD, 1), F32)],
        compiler_params=pltpu.CompilerParams(dimension_semantics=("arbitrary", "arbitrary"),
                                             vmem_limit_bytes=VMEM_LIMIT),
        name="rank_tokens",
    )(rid, u_mat, tri)


def _dispatch_kernel(cum_ref, nblk_ref, nused_ref, dest_hbm, rows_hbm, out_hbm, idx_smem, rows_buf, zero_ref,
                     sem_idx, sem_in, sem_row, sem_zero):
    i = pl.program_id(0)
    n_steps = pl.num_programs(0)
    slot = i % 2
    tile_rows = TD * SUBLANES
    blk_rows = RB * SUBLANES
    n_blocks = out_hbm.shape[0] // blk_rows

    def fetch(step, s):
        start = pl.multiple_of(step * tile_rows, tile_rows)
        return (pltpu.make_async_copy(dest_hbm.at[step], idx_smem.at[s], sem_idx.at[s]),
                pltpu.make_async_copy(rows_hbm.at[pl.ds(start, tile_rows)], rows_buf.at[s], sem_in.at[s]))

    def scattered(s):
        return pltpu.make_async_copy(rows_buf.at[s], out_hbm.at[pl.ds(0, tile_rows)], sem_row.at[s])

    def zero_block(blk):
        start = pl.multiple_of(blk * blk_rows, blk_rows)
        return pltpu.make_async_copy(zero_ref, out_hbm.at[pl.ds(start, blk_rows)], sem_zero)

    @pl.when(i == 0)
    def _():
        zero_ref[...] = jnp.zeros_like(zero_ref)

        def start(c, carry):
            @pl.when(nblk_ref[c] > 0)
            def _():
                zero_block(cum_ref[c] - 1).start()
            return carry

        def wait(c, carry):
            @pl.when(nblk_ref[c] > 0)
            def _():
                zero_block(cum_ref[c] - 1).wait()
            return carry

        def start_tail(blk, carry):
            zero_block(blk).start()
            return carry

        def wait_tail(blk, carry):
            zero_block(blk).wait()
            return carry

        lax.fori_loop(0, N_CLASSES, start, 0)
        lax.fori_loop(nused_ref[0], n_blocks, start_tail, 0)
        lax.fori_loop(0, N_CLASSES, wait, 0)
        lax.fori_loop(nused_ref[0], n_blocks, wait_tail, 0)
        for cp in fetch(0, 0):
            cp.start()

    for cp in fetch(i, slot):
        cp.wait()

    @pl.when(i > 0)
    def _():
        scattered(1 - slot).wait()

    @pl.when(i + 1 < n_steps)
    def _():
        for cp in fetch(i + 1, 1 - slot):
            cp.start()

    def scatter(g, carry):
        for u in range(ROW_UNROLL):
            t = g * ROW_UNROLL + u
            src = pl.multiple_of(t * SUBLANES, SUBLANES)
            dst = pl.multiple_of(idx_smem[slot, t] * SUBLANES, SUBLANES)
            pltpu.make_async_copy(rows_buf.at[slot, pl.ds(src, SUBLANES)], out_hbm.at[pl.ds(dst, SUBLANES)],
                                  sem_row.at[slot]).start(priority=u % 2)
        return carry

    lax.fori_loop(0, TD // ROW_UNROLL, scatter, 0)

    @pl.when(i + 1 == n_steps)
    def _():
        scattered(slot).wait()


def _dispatch_call(cum_cls, nblk_cls, nused, dest2d, rows, n_rows):
    n_tok = rows.shape[0] // SUBLANES
    grid_spec = pltpu.PrefetchScalarGridSpec(
        num_scalar_prefetch=3,
        grid=(n_tok // TD,),
        in_specs=[pl.BlockSpec(memory_space=pl.ANY), pl.BlockSpec(memory_space=pl.ANY)],
        out_specs=pl.BlockSpec(memory_space=pl.ANY),
        scratch_shapes=[pltpu.SMEM((2, TD), I32), pltpu.VMEM((2, TD * SUBLANES, LANES), U32),
                        pltpu.VMEM((RB * SUBLANES, LANES), U32),
                        pltpu.SemaphoreType.DMA((2,)), pltpu.SemaphoreType.DMA((2,)),
                        pltpu.SemaphoreType.DMA((2,)), pltpu.SemaphoreType.DMA],
    )
    return pl.pallas_call(
        _dispatch_kernel,
        out_shape=jax.ShapeDtypeStruct((n_rows * SUBLANES, LANES), U32),
        grid_spec=grid_spec,
        compiler_params=pltpu.CompilerParams(dimension_semantics=("arbitrary",),
                                             vmem_limit_bytes=VMEM_LIMIT),
        name="dispatch_rows",
    )(cum_cls, nblk_cls, nused, dest2d, rows)


def _expert_kernel(ea_ref, eb_ref, nused_ref, valid_ref, rows_ref, wgu_a_ref, wd_a_ref, wgu_b_ref, wd_b_ref, y_ref):
    b = pl.program_id(0)
    live = b < nused_ref[0]
    half_rows = RB // 2

    def evaluate(m):
        def tile_row(k):
            return rows_ref[pl.ds(k, m, stride=SUBLANES), :]

        words = [tile_row(k) for k in range(PACK_ROWS)]
        lo = [lax.bitcast_convert_type(jnp.left_shift(w, 16), F32) for w in words]
        hi = [lax.bitcast_convert_type(jnp.bitwise_and(w, jnp.uint32(HI_MASK)), F32) for w in words]
        x = jnp.concatenate(lo + hi, axis=1).astype(MXU_DTYPE)
        gate = lax.bitcast_convert_type(tile_row(PACK_ROWS), F32)

        def expert(wgu_ref, wd_ref):
            gu = _mdot(x, wgu_ref[0])
            act = jax.nn.silu(gu[:, :D_EXPERT]) * gu[:, D_EXPERT:]
            return _mdot(act.astype(MXU_DTYPE), wd_ref[0])

        y = expert(wgu_a_ref, wd_a_ref) * gate[:, 0:1] + expert(wgu_b_ref, wd_b_ref) * gate[:, 1:2]
        for k in range(Y_ROWS):
            y_ref[pl.ds(k, m, stride=SUBLANES), :] = y[:, k * LANES:(k + 1) * LANES]

    @pl.when(live & (valid_ref[b] > half_rows))
    def _():
        evaluate(RB)

    @pl.when(live & (valid_ref[b] <= half_rows))
    def _():
        evaluate(half_rows)
        y_ref[half_rows * Y_ROWS:, :] = jnp.zeros((half_rows * Y_ROWS, LANES), F32)

    @pl.when(jnp.logical_not(live))
    def _():
        y_ref[...] = jnp.zeros_like(y_ref)


def _expert_call(blk_ea, blk_eb, nused, blk_valid, rows, w_gate_up, w_down):
    n_rows = rows.shape[0] // SUBLANES
    nblk = n_rows // RB

    def live(b, nu):
        return jnp.maximum(jnp.minimum(b, nu[0] - 1), 0)

    grid_spec = pltpu.PrefetchScalarGridSpec(
        num_scalar_prefetch=4,
        grid=(nblk,),
        in_specs=[pl.BlockSpec((RB * SUBLANES, LANES), lambda b, ea, eb, nu, nv: (live(b, nu), 0)),
                  pl.BlockSpec((1,) + w_gate_up.shape[1:], lambda b, ea, eb, nu, nv: (ea[live(b, nu)], 0, 0)),
                  pl.BlockSpec((1,) + w_down.shape[1:], lambda b, ea, eb, nu, nv: (ea[live(b, nu)], 0, 0)),
                  pl.BlockSpec((1,) + w_gate_up.shape[1:], lambda b, ea, eb, nu, nv: (eb[live(b, nu)], 0, 0)),
                  pl.BlockSpec((1,) + w_down.shape[1:], lambda b, ea, eb, nu, nv: (eb[live(b, nu)], 0, 0))],
        out_specs=pl.BlockSpec((RB * Y_ROWS, LANES), lambda b, ea, eb, nu, nv: (b, 0)),
    )
    return pl.pallas_call(
        _expert_kernel,
        out_shape=jax.ShapeDtypeStruct((n_rows * Y_ROWS, LANES), F32),
        grid_spec=grid_spec,
        compiler_params=pltpu.CompilerParams(dimension_semantics=("arbitrary",),
                                             vmem_limit_bytes=VMEM_LIMIT),
        name="pair_experts",
    )(blk_ea, blk_eb, nused, blk_valid, rows, w_gate_up, w_down, w_gate_up, w_down)


def _combine_kernel(dest_hbm, y_hbm, x1_ref, mod_ref, g_ref, b_ref, o_ref, idx_smem, ybuf, sem_idx, sem_row,
                    *, tile_off):
    ns = pl.num_programs(1)
    i = pl.program_id(0) * ns + pl.program_id(1)
    n_steps = pl.num_programs(0) * ns
    slot = i % 2

    def idx_copy(step, s):
        return pltpu.make_async_copy(dest_hbm.at[tile_off + step], idx_smem.at[s], sem_idx.at[s])

    def issue_gather(s):
        def gather(g, carry):
            for u in range(ROW_UNROLL):
                t = g * ROW_UNROLL + u
                src = pl.multiple_of(idx_smem[s, t] * Y_ROWS, Y_ROWS)
                dst = pl.multiple_of(t * Y_ROWS, Y_ROWS)
                pltpu.make_async_copy(y_hbm.at[pl.ds(src, Y_ROWS)], ybuf.at[s, pl.ds(dst, Y_ROWS)],
                                      sem_row.at[s]).start(priority=u % 2)
            return carry

        lax.fori_loop(0, TD // ROW_UNROLL, gather, 0)

    @pl.when(i == 0)
    def _():
        idx_copy(0, 0).start()
        idx_copy(0, 0).wait()
        issue_gather(0)

        @pl.when(n_steps > 1)
        def _():
            idx_copy(1, 1).start()

    @pl.when(i + 1 < n_steps)
    def _():
        idx_copy(i + 1, 1 - slot).wait()
        issue_gather(1 - slot)

    @pl.when(i + 2 < n_steps)
    def _():
        idx_copy(i + 2, slot).start()

    pltpu.make_async_copy(y_hbm.at[pl.ds(0, TD * Y_ROWS)], ybuf.at[slot], sem_row.at[slot]).wait()

    y = jnp.concatenate([ybuf[slot, pl.ds(k, TD, stride=Y_ROWS), :] for k in range(Y_ROWS)], axis=1)
    g2 = mod_ref[0][5:6]
    o_ref[0] = _layer_norm(ALPHA * x1_ref[...] + g2 * y, g_ref[...], b_ref[...])


def _combine_call(dest2d, y_rows, x1, mod, mod_off, tile_off, ln2g, ln2b, bsz, seq):
    d = x1.shape[1]
    ns = seq // TD
    return pl.pallas_call(
        functools.partial(_combine_kernel, tile_off=tile_off),
        out_shape=jax.ShapeDtypeStruct((bsz, seq, d), F32),
        grid=(bsz, ns),
        in_specs=[pl.BlockSpec(memory_space=pl.ANY),
                  pl.BlockSpec(memory_space=pl.ANY),
                  pl.BlockSpec((TD, d), lambda b, s: (tile_off + b * ns + s, 0)),
                  pl.BlockSpec((1, N_MOD, d), lambda b, s: (mod_off + b, 0, 0)),
                  pl.BlockSpec((1, d), lambda b, s: (0, 0)),
                  pl.BlockSpec((1, d), lambda b, s: (0, 0))],
        out_specs=pl.BlockSpec((1, TD, d), lambda b, s: (b, s, 0)),
        scratch_shapes=[pltpu.SMEM((2, TD), I32), pltpu.VMEM((2, TD * Y_ROWS, LANES), F32),
                        pltpu.SemaphoreType.DMA((2,)), pltpu.SemaphoreType.DMA((2,))],
        compiler_params=pltpu.CompilerParams(dimension_semantics=("arbitrary", "arbitrary"),
                                             vmem_limit_bytes=VMEM_LIMIT),
        name="combine_ln2",
    )(dest2d, y_rows, x1, mod, ln2g, ln2b)


def _band_matrices():
    rows = np.arange(CHUNK)[:, None]
    cols = np.arange(CHUNK + 2 * POOL_HALO)[None, :] - POOL_HALO
    return np.stack([((cols >= rows - w // 2) & (cols < rows + w // 2)) for w in POOL_WINDOWS]).astype(np.float32)


def _class_tables():
    ea, eb = [], []
    for g in range(N_GROUPS):
        for a in range(EPG):
            for b in range(a + 1, EPG):
                ea.append(g * EPG + a)
                eb.append(g * EPG + b)
    return np.asarray(ea, np.int32), np.asarray(eb, np.int32)


def _split_hi_lo(w):
    hi = w.astype(MXU_DTYPE)
    lo = (w - hi.astype(F32)).astype(MXU_DTYPE)
    return hi, lo


def kernel(x_prompt, x_sample, c_prompt, c_sample, w_ada, b_ada, w_in, v_norm_g, v_norm_b, w_spatial, b_spatial,
           w_pool, pool_scale, w_out, ln1_g, ln1_b, w_route_group, b_route_group, w_route_expert, b_route_expert,
           w_gate_up, w_down, ln2_g, ln2_b):
    assert w_ada.shape[0] == 1, "single-layer kernel"
    bp, sp, d = x_prompt.shape
    bs, ss, _ = x_sample.shape
    assert d == D_MODEL and sp % TS == 0 and ss % TS == 0
    n_prompt, n_sample = bp * sp, bs * ss
    n_tok = n_prompt + n_sample
    assert n_tok % BL == 0 and n_prompt % TD == 0

    c_all = jnp.concatenate([c_prompt, c_sample], axis=0)
    mod = _modulation(c_all, w_ada[0], b_ada).reshape(bp + bs, N_MOD, d)

    wr = jnp.concatenate([w_route_group[0], w_route_expert[0]], axis=1).T
    wr = jnp.pad(wr, ((0, ROUTE_PAD - N_ROUTE), (0, 0)))
    wr_hi, wr_lo = _split_hi_lo(wr)
    rbias = jnp.concatenate([b_route_group[0], b_route_expert[0].reshape(-1),
                             jnp.zeros((ROUTE_PAD - N_ROUTE,), F32)]).reshape(ROUTE_PAD, 1)
    row = lambda a: a.reshape(1, -1)
    weights = (w_in[0].astype(MXU_DTYPE), row(v_norm_g[0]), row(v_norm_b[0]), w_spatial[0].astype(MXU_DTYPE),
               b_spatial[0].T, jnp.asarray(_band_matrices(), MXU_DTYPE), w_pool[0].astype(MXU_DTYPE),
               row(pool_scale[0]), w_out[0].astype(MXU_DTYPE), row(ln1_g[0]), row(ln1_b[0]),
               jnp.concatenate([wr_hi, wr_lo], axis=0), rbias)

    x1, rows, rid = _mixer_call(x_prompt, x_sample, mod, weights)

    nblk = (n_tok + N_CLASSES * (RB - 1)) // RB
    nb_pad = -(-nblk // 128) * 128
    u_mat = jnp.asarray(np.triu(np.ones((SUB, SUB), np.float32)), MXU_DTYPE)
    tri = jnp.asarray(np.tril(np.ones((CLS_PAD, CLS_PAD), np.float32)))
    dest, meta = _rank_call(rid, u_mat, tri, nb_pad)
    dest2d = dest.reshape(n_tok // TD, TD)
    tab_a, tab_b = _class_tables()
    blk_cls = meta[0, :nblk]
    blk_ea = jnp.asarray(tab_a)[blk_cls]
    blk_eb = jnp.asarray(tab_b)[blk_cls]
    nused = meta[1, 0:1]
    cum_cls = meta[2, :CLS_PAD]
    nblk_cls = meta[3, :CLS_PAD]

    sorted_rows = _dispatch_call(cum_cls, nblk_cls, nused, dest2d, rows, nblk * RB)
    y_rows = _expert_call(blk_ea, blk_eb, nused, meta[4, :nblk], sorted_rows, w_gate_up[0].astype(MXU_DTYPE),
                          w_down[0].astype(MXU_DTYPE))

    ln2g, ln2b = row(ln2_g[0]), row(ln2_b[0])
    y_prompt = _combine_call(dest2d, y_rows, x1, mod, 0, 0, ln2g, ln2b, bp, sp)
    y_sample = _combine_call(dest2d, y_rows, x1, mod, bp, n_prompt // TD, ln2g, ln2b, bs, ss)
    return (y_prompt, y_sample)
```

```python
import functools

import numpy as np
import jax
import jax.numpy as jnp
from jax import lax
from jax.experimental import pallas as pl
from jax.experimental.pallas import tpu as pltpu

F32 = jnp.float32
I32 = jnp.int32
MXU_DTYPE = jnp.bfloat16

D_MODEL = 1024
A_WIDTH = 512
B_WIDTH = 512
CHUNK = 128
A_HEADS = 4
HEAD_DIM = A_WIDTH // A_HEADS
POOL_WINDOWS = (2, 4, 8, 16)
POOL_HALO = 8
GROUP_DIM = B_WIDTH // len(POOL_WINDOWS)
N_GROUPS = 4
EPG = 8
N_EXPERTS = N_GROUPS * EPG
D_EXPERT = 512
N_MOD = 6
LN_EPS = 1e-5
ALPHA = 2.0 ** 0.25

PAIRS = EPG * (EPG - 1) // 2
N_CLASSES = N_GROUPS * PAIRS
CLS_PAD = 128
N_ROUTE = N_GROUPS + N_EXPERTS
ROUTE_PAD = 40
U32 = jnp.uint32
LANES = 128
SUBLANES = 8
PACK_ROWS = D_MODEL // (2 * LANES)
Y_ROWS = D_MODEL // LANES
assert PACK_ROWS < SUBLANES and Y_ROWS == SUBLANES
HI_MASK = 0xFFFF0000

TS = 512
TD = 512
RB = 512
ROW_UNROLL = 8
BL = 2048
SUB = 256
VMEM_LIMIT = 56 * 1024 * 1024

_NT = (((1,), (1,)), ((), ()))

_GELU_K1 = -2.0 * (2.0 / np.pi) ** 0.5
_GELU_K3 = _GELU_K1 * 0.044715


def _gelu_tanh(x):
    return x / (1.0 + jnp.exp(x * (_GELU_K1 + _GELU_K3 * (x * x))))


def _layer_norm(x, g, b):
    mu = jnp.mean(x, axis=-1, keepdims=True)
    xc = x - mu
    var = jnp.mean(xc * xc, axis=-1, keepdims=True)
    return xc * lax.rsqrt(var + LN_EPS) * g + b


def _mdot(a, b):
    return jnp.dot(a, b, preferred_element_type=F32)


def _mod_kernel(c_ref, w_ref, b_ref, o_ref):
    a = jax.nn.silu(c_ref[...])
    o_ref[...] = jnp.dot(a, w_ref[...], precision=lax.Precision.HIGHEST,
                         preferred_element_type=F32) + b_ref[...]


def _modulation(c_all, w_ada, b_ada):
    nb, d = c_all.shape
    n = w_ada.shape[1]
    bn = 1536
    return pl.pallas_call(
        _mod_kernel,
        out_shape=jax.ShapeDtypeStruct((nb, n), F32),
        grid=(n // bn,),
        in_specs=[pl.BlockSpec((nb, d), lambda j: (0, 0)),
                  pl.BlockSpec((d, bn), lambda j: (0, j)),
                  pl.BlockSpec((1, bn), lambda j: (0, j))],
        out_specs=pl.BlockSpec((nb, bn), lambda j: (0, j)),
        compiler_params=pltpu.CompilerParams(dimension_semantics=("arbitrary",),
                                             vmem_limit_bytes=VMEM_LIMIT),
        name="adaln_mod",
    )(c_all, w_ada, b_ada)


def _mixer_kernel(xp_ref, xpp_ref, xpn_ref, xs_ref, xsp_ref, xsn_ref, mod_ref, modp_ref, inv_ref, win_ref, vng_ref,
                  vnb_ref, ws_ref, bst_ref, band_ref, wpool_ref, pscale_ref, wout_ref, ln1g_ref, ln1b_ref, wr_ref,
                  rb_ref, x1_ref, rows_ref, rid_ref, res_ref, *, prompt_tiles, prompt_seq, sample_seq):
    ts = xp_ref.shape[1]
    i = pl.program_id(0)
    slot = i % 2

    @pl.when(i == 0)
    def _():
        res_ref[1] = jnp.zeros(res_ref.shape[1:], F32)

    is_p = i < prompt_tiles
    ns = jnp.where(is_p, prompt_seq // ts, sample_seq // ts)
    s = jnp.where(is_p, i, i - prompt_tiles) % ns
    md = mod_ref[0]
    sh1, sc1, g1 = md[0:1], md[1:2], md[2:3]

    xt = jnp.where(is_p, xp_ref[0], xs_ref[0])
    h = xt * (1.0 + sc1) + sh1
    hp = jnp.where(s > 0, jnp.where(is_p, xpp_ref[0], xsp_ref[0]) * (1.0 + sc1) + sh1, 0.0)
    hn = jnp.where(s < ns - 1, jnp.where(is_p, xpn_ref[0], xsn_ref[0]) * (1.0 + sc1) + sh1, 0.0)
    hext = jnp.concatenate([h, hp, hn], axis=0).astype(MXU_DTYPE)
    zext = _mdot(hext, win_ref[...])
    z = zext[:ts]

    za = _gelu_tanh(z[:, :2 * A_WIDTH])
    u = za[:, :A_WIDTH]
    v = _layer_norm(za[:, A_WIDTH:], vng_ref[...], vnb_ref[...]).astype(MXU_DTYPE)
    bst = bst_ref[...]
    n_chunks = ts // CHUNK

    def chunks_on_lanes(a, rows, col0, width):
        return jnp.concatenate([a[c * CHUNK:c * CHUNK + rows, col0:col0 + width] for c in range(n_chunks)], axis=1)

    def chunks_on_rows(a, width):
        return jnp.concatenate([a[:, c * width:(c + 1) * width] for c in range(n_chunks)], axis=0)

    heads = [_mdot(ws_ref[hh], chunks_on_lanes(v, CHUNK, hh * HEAD_DIM, HEAD_DIM)) + bst[:, hh:hh + 1]
             for hh in range(A_HEADS)]
    a_out = u * jnp.concatenate([chunks_on_rows(o, HEAD_DIM) for o in heads], axis=1)

    p = z[:, 2 * A_WIDTH:]
    pall = jnp.concatenate([zext[ts:ts + POOL_HALO, 2 * A_WIDTH:], p,
                            zext[ts + POOL_HALO:ts + 2 * POOL_HALO, 2 * A_WIDTH:]], axis=0)
    p_hi = pall.astype(MXU_DTYPE)
    p_lo = (pall - p_hi.astype(F32)).astype(MXU_DTYPE)
    b_cols = []
    for g in range(len(POOL_WINDOWS)):
        seg_rows = CHUNK + 2 * POOL_HALO
        win = (_mdot(band_ref[g], chunks_on_lanes(p_hi, seg_rows, g * GROUP_DIM, GROUP_DIM))
               + _mdot(band_ref[g], chunks_on_lanes(p_lo, seg_rows, g * GROUP_DIM, GROUP_DIM)))
        pooled = chunks_on_rows(win, GROUP_DIM) * inv_ref[0, g] - p[:, g * GROUP_DIM:(g + 1) * GROUP_DIM]
        b_cols.append(_mdot(pooled.astype(MXU_DTYPE), wpool_ref[g]))
    b_out = jnp.concatenate(b_cols, axis=1) * pscale_ref[...]

    mix_in = jnp.concatenate([a_out, b_out], axis=1).astype(MXU_DTYPE)
    mdp = modp_ref[0]
    h2_prev, lt_prev = _ln1_router(res_ref[1 - slot], mdp[3:4], mdp[4:5], ln1g_ref, ln1b_ref, wr_ref, rb_ref,
                                   x1_ref)
    _route_and_pack(h2_prev, lt_prev, rows_ref, rid_ref)

    mix = _mdot(mix_in, wout_ref[...])
    res_ref[slot] = ALPHA * xt + g1 * mix


def _ln1_router(res, sh2, sc2, ln1g_ref, ln1b_ref, wr_ref, rb_ref, x1_ref):
    x1 = _layer_norm(res, ln1g_ref[...], ln1b_ref[...])
    x1_ref[...] = x1
    h2 = x1 * (1.0 + sc2) + sh2
    h2_hi = h2.astype(MXU_DTYPE)
    h2_lo = (h2 - h2_hi.astype(F32)).astype(MXU_DTYPE)
    l1 = lax.dot_general(wr_ref[...], h2_hi, _NT, preferred_element_type=F32)
    l2 = lax.dot_general(wr_ref[0:ROUTE_PAD], h2_lo, _NT, preferred_element_type=F32)
    return h2, l1[:ROUTE_PAD] + l1[ROUTE_PAD:] + l2 + rb_ref[...]


def _route_and_pack(h2, lt, rows_ref, rid_ref):
    ts = h2.shape[0]

    def row(r):
        return lt[r:r + 1, :]

    gl = [row(r) for r in range(N_GROUPS)]
    gmax = jnp.maximum(jnp.maximum(gl[0], gl[1]), jnp.maximum(gl[2], gl[3]))
    gidx = jnp.where(gl[0] == gmax, 0, jnp.where(gl[1] == gmax, 1, jnp.where(gl[2] == gmax, 2, 3)))
    gsum = (jnp.exp(gl[0] - gmax) + jnp.exp(gl[1] - gmax)) + (jnp.exp(gl[2] - gmax) + jnp.exp(gl[3] - gmax))
    gw = 1.0 / gsum
    ev = [jnp.where(gidx == 0, row(N_GROUPS + j),
                    jnp.where(gidx == 1, row(N_GROUPS + EPG + j),
                              jnp.where(gidx == 2, row(N_GROUPS + 2 * EPG + j), row(N_GROUPS + 3 * EPG + j))))
          for j in range(EPG)]

    def top1(vals):
        m = vals[0]
        for t in vals[1:]:
            m = jnp.maximum(m, t)
        idx = jnp.full(m.shape, EPG - 1, I32)
        for j in range(EPG - 2, -1, -1):
            idx = jnp.where(vals[j] == m, j, idx)
        return m, idx

    v1, j1 = top1(ev)
    v2, j2 = top1([jnp.where(j1 == j, -jnp.inf, ev[j]) for j in range(EPG)])
    t2 = jnp.exp(v2 - v1)
    den = 1.0 + t2
    w1 = (1.0 / den) * gw
    w2 = (t2 / den) * gw
    first = j1 < j2
    ea = jnp.minimum(j1, j2)
    eb = jnp.maximum(j1, j2)
    wa = jnp.where(first, w1, w2)
    wb = jnp.where(first, w2, w1)
    cls = gidx * PAIRS + jnp.right_shift(ea * (2 * EPG - 1 - ea), 1) + (eb - ea - 1)

    r8 = lax.broadcasted_iota(I32, (8, ts), 0)
    rid_ref[...] = jnp.where(r8 == 0, cls, jnp.where(r8 == 1, gidx * EPG + ea,
                                                      jnp.where(r8 == 2, gidx * EPG + eb, 0)))
    w8 = jnp.where(r8 == 0, wa, jnp.where(r8 == 1, wb, 0.0))
    w128 = jnp.concatenate([w8, jnp.zeros((LANES - 8, ts), F32)], axis=0)
    bits = lax.bitcast_convert_type(h2.astype(jnp.bfloat16).astype(F32), U32)
    half = D_MODEL // 2
    for k in range(PACK_ROWS):
        lo = jnp.right_shift(bits[:, k * LANES:(k + 1) * LANES], 16)
        hi = jnp.bitwise_and(bits[:, half + k * LANES:half + (k + 1) * LANES], jnp.uint32(HI_MASK))
        rows_ref[pl.ds(k, ts, stride=SUBLANES), :] = jnp.bitwise_or(lo, hi)
    rows_ref[pl.ds(PACK_ROWS, ts, stride=SUBLANES), :] = lax.bitcast_convert_type(w128.T, U32)
    for k in range(PACK_ROWS + 1, SUBLANES):
        rows_ref[pl.ds(k, ts, stride=SUBLANES), :] = jnp.zeros((ts, LANES), U32)


def _inv_population_table():
    r = np.arange(TS)
    out = np.empty((3, len(POOL_WINDOWS), TS, LANES), np.float32)
    for case in range(3):
        for g, w in enumerate(POOL_WINDOWS):
            lo = np.maximum(r - w // 2, 0) if case == 0 else r - w // 2
            hi = np.minimum(r + w // 2, TS) if case == 2 else r + w // 2
            out[case, g] = (1.0 / (hi - lo).astype(np.float64)).astype(np.float32)[:, None]
    return out


def _mixer_call(x_prompt, x_sample, mod, weights):
    bp, sp, d = x_prompt.shape
    bs, ss, _ = x_sample.shape
    nsp, nss = sp // TS, ss // TS
    assert nsp >= 2 and nss >= 2, "a tile is the first or the last of its sequence, not both"
    ntp, nts = bp * nsp, bs * nss
    n_tok = bp * sp + bs * ss
    hb = TS // POOL_HALO

    def p_tile(i):
        t = jnp.minimum(i, ntp - 1)
        return t // nsp, t % nsp

    def s_tile(i):
        t = jnp.clip(i - ntp, 0, nts - 1)
        return t // nss, t % nss

    def specs(tile_fn, seq):
        def cur(i):
            b, s = tile_fn(i)
            return (b, s, 0)

        def prev(i):
            b, s = tile_fn(i)
            return (b, jnp.maximum(s * hb - 1, 0), 0)

        def nxt(i):
            b, s = tile_fn(i)
            return (b, jnp.minimum((s + 1) * hb, seq // POOL_HALO - 1), 0)

        return [pl.BlockSpec((1, TS, d), cur), pl.BlockSpec((1, POOL_HALO, d), prev),
                pl.BlockSpec((1, POOL_HALO, d), nxt)]

    def mod_map(i):
        return (jnp.where(i < ntp, p_tile(i)[0], bp + s_tile(i)[0]), 0, 0)

    def inv_map(i):
        s = jnp.where(i < ntp, p_tile(i)[1], s_tile(i)[1])
        last = jnp.where(i < ntp, nsp - 1, nss - 1)
        return (jnp.where(s == 0, 0, jnp.where(s == last, 2, 1)), 0, 0, 0)

    def const(w):
        return pl.BlockSpec(w.shape, lambda i, nd=w.ndim: (0,) * nd)

    def prev_tile(i):
        return jnp.maximum(i - 1, 0)

    inv_tab = jnp.asarray(_inv_population_table())
    in_specs = (specs(p_tile, sp) + specs(s_tile, ss)
                + [pl.BlockSpec((1, N_MOD, d), mod_map),
                   pl.BlockSpec((1, N_MOD, d), lambda i: mod_map(prev_tile(i))),
                   pl.BlockSpec((1,) + inv_tab.shape[1:], inv_map)]
                + [const(w) for w in weights])
    out_shape = (jax.ShapeDtypeStruct((n_tok, d), F32),
                 jax.ShapeDtypeStruct((n_tok * SUBLANES, LANES), U32),
                 jax.ShapeDtypeStruct((8, n_tok), I32))
    out_specs = (pl.BlockSpec((TS, d), lambda i: (prev_tile(i), 0)),
                 pl.BlockSpec((TS * SUBLANES, LANES), lambda i: (prev_tile(i), 0)),
                 pl.BlockSpec((8, TS), lambda i: (0, prev_tile(i))))
    return pl.pallas_call(
        functools.partial(_mixer_kernel, prompt_tiles=ntp, prompt_seq=sp, sample_seq=ss),
        out_shape=out_shape,
        grid=(ntp + nts + 1,),
        in_specs=in_specs,
        out_specs=out_specs,
        scratch_shapes=[pltpu.VMEM((2, TS, d), F32)],
        compiler_params=pltpu.CompilerParams(dimension_semantics=("arbitrary",),
                                             vmem_limit_bytes=VMEM_LIMIT),
        name="mixer_ln1_route",
    )(x_prompt, x_prompt, x_prompt, x_sample, x_sample, x_sample, mod, mod, inv_tab, *weights)


def _rank_kernel(rid_ref, u_ref, tri_ref, dest_ref, meta_ref, cnt_ref, base_ref, *, nb_pad):
    phase = pl.program_id(0)
    j = pl.program_id(1)
    cls_iota = lax.broadcasted_iota(I32, (CLS_PAD, SUB), 0)

    @pl.when((phase == 0) & (j == 0))
    def _():
        cnt_ref[...] = jnp.zeros_like(cnt_ref)

    @pl.when(phase == 0)
    def _():
        acc = cnt_ref[...]
        for sb in range(BL // SUB):
            ids = rid_ref[0:1, sb * SUB:(sb + 1) * SUB]
            acc = acc + jnp.sum((cls_iota == ids).astype(F32), axis=1, keepdims=True)
        cnt_ref[...] = acc

    @pl.when((phase == 1) & (j == 0))
    def _():
        cnt = jnp.broadcast_to(cnt_ref[...], (CLS_PAD, CLS_PAD))
        nblk = jnp.floor((cnt + (RB - 1)) * (1.0 / RB))
        cum = jnp.dot(tri_ref[...], nblk, precision=lax.Precision.HIGHEST, preferred_element_type=F32)
        base_ref[...] = (cum[:, 0:1] - nblk[:, 0:1]) * RB
        blk = lax.broadcasted_iota(I32, (CLS_PAD, nb_pad), 1).astype(F32)
        bcls = jnp.sum((cum[:, 0:1] <= blk).astype(F32), axis=0, keepdims=True)
        bcls = jnp.minimum(bcls, N_CLASSES - 1).astype(I32)
        nused = jnp.broadcast_to(cum[CLS_PAD - 1:CLS_PAD, 0:1], (1, nb_pad)).astype(I32)
        first = cum[:, 0:1] - nblk[:, 0:1]
        inside = (first <= blk) & (blk < cum[:, 0:1])
        fill = jnp.clip(cnt[:, 0:1] - (blk - first) * RB, 0.0, float(RB))
        valid = jnp.sum(jnp.where(inside, fill, 0.0), axis=0, keepdims=True).astype(I32)
        cum_l = jnp.transpose(cum)[0:1].astype(I32)
        nblk_l = jnp.transpose(nblk)[0:1].astype(I32)
        pad = jnp.zeros((1, nb_pad - CLS_PAD), I32)
        r8 = lax.broadcasted_iota(I32, (8, nb_pad), 0)
        cum_row = jnp.concatenate([cum_l, pad], axis=1)
        nblk_row = jnp.concatenate([nblk_l, pad], axis=1)
        meta_ref[...] = jnp.where(r8 == 0, bcls, jnp.where(r8 == 1, nused,
                                  jnp.where(r8 == 2, cum_row, jnp.where(r8 == 3, nblk_row,
                                                                        jnp.where(r8 == 4, valid, 0)))))

    @pl.when(phase == 1)
    def _():
        base = base_ref[...]
        for sb in range(BL // SUB):
            ids = rid_ref[0:1, sb * SUB:(sb + 1) * SUB]
            hit = cls_iota == ids
            incl = _mdot(hit.astype(MXU_DTYPE), u_ref[...])
            slot = jnp.sum(jnp.where(hit, base + incl - 1.0, 0.0), axis=0, keepdims=True)
            dest_ref[0:1, sb * SUB:(sb + 1) * SUB] = slot.astype(I32)
            base = base + incl[:, SUB - 1:SUB]
        base_ref[...] = base


def _rank_call(rid, u_mat, tri, nb_pad):
    n_tok = rid.shape[1]
    nj = n_tok // BL
    return pl.pallas_call(
        functools.partial(_rank_kernel, nb_pad=nb_pad),
        out_shape=(jax.ShapeDtypeStruct((1, n_tok), I32), jax.ShapeDtypeStruct((8, nb_pad), I32)),
        grid=(2, nj),
        in_specs=[pl.BlockSpec((8, BL), lambda p, j: (0, j)),
                  pl.BlockSpec(u_mat.shape, lambda p, j: (0, 0)),
                  pl.BlockSpec(tri.shape, lambda p, j: (0, 0))],
        out_specs=(pl.BlockSpec((1, BL), lambda p, j: (0, p * j)),
                   pl.BlockSpec((8, nb_pad), lambda p, j: (0, 0))),
        scratch_shapes=[pltpu.VMEM((CLS_PAD, 1), F32), pltpu.VMEM((CLS_PAD, 1), F32)],
        compiler_params=pltpu.CompilerParams(dimension_semantics=("arbitrary", "arbitrary"),
                                             vmem_limit_bytes=VMEM_LIMIT),
        name="rank_tokens",
    )(rid, u_mat, tri)


def _dispatch_kernel(cum_ref, nblk_ref, nused_ref, dest_hbm, rows_hbm, out_hbm, idx_smem, rows_buf, zero_ref,
                     sem_idx, sem_in, sem_row, sem_zero):
    i = pl.program_id(0)
    n_steps = pl.num_programs(0)
    slot = i % 2
    tile_rows = TD * SUBLANES
    blk_rows = RB * SUBLANES
    n_blocks = out_hbm.shape[0] // blk_rows

    def fetch(step, s):
        start = pl.multiple_of(step * tile_rows, tile_rows)
        return (pltpu.make_async_copy(dest_hbm.at[step], idx_smem.at[s], sem_idx.at[s]),
                pltpu.make_async_copy(rows_hbm.at[pl.ds(start, tile_rows)], rows_buf.at[s], sem_in.at[s]))

    def scattered(s):
        return pltpu.make_async_copy(rows_buf.at[s], out_hbm.at[pl.ds(0, tile_rows)], sem_row.at[s])

    def zero_block(blk):
        start = pl.multiple_of(blk * blk_rows, blk_rows)
        return pltpu.make_async_copy(zero_ref, out_hbm.at[pl.ds(start, blk_rows)], sem_zero)

    @pl.when(i == 0)
    def _():
        zero_ref[...] = jnp.zeros_like(zero_ref)

        def start(c, carry):
            @pl.when(nblk_ref[c] > 0)
            def _():
                zero_block(cum_ref[c] - 1).start()
            return carry

        def wait(c, carry):
            @pl.when(nblk_ref[c] > 0)
            def _():
                zero_block(cum_ref[c] - 1).wait()
            return carry

        def start_tail(blk, carry):
            zero_block(blk).start()
            return carry

        def wait_tail(blk, carry):
            zero_block(blk).wait()
            return carry

        lax.fori_loop(0, N_CLASSES, start, 0)
        lax.fori_loop(nused_ref[0], n_blocks, start_tail, 0)
        lax.fori_loop(0, N_CLASSES, wait, 0)
        lax.fori_loop(nused_ref[0], n_blocks, wait_tail, 0)
        for cp in fetch(0, 0):
            cp.start()

    for cp in fetch(i, slot):
        cp.wait()

    @pl.when(i > 0)
    def _():
        scattered(1 - slot).wait()

    @pl.when(i + 1 < n_steps)
    def _():
        for cp in fetch(i + 1, 1 - slot):
            cp.start()

    def scatter(g, carry):
        for u in range(ROW_UNROLL):
            t = g * ROW_UNROLL + u
            src = pl.multiple_of(t * SUBLANES, SUBLANES)
            dst = pl.multiple_of(idx_smem[slot, t] * SUBLANES, SUBLANES)
            pltpu.make_async_copy(rows_buf.at[slot, pl.ds(src, SUBLANES)], out_hbm.at[pl.ds(dst, SUBLANES)],
                                  sem_row.at[slot]).start(priority=u % 2)
        return carry

    lax.fori_loop(0, TD // ROW_UNROLL, scatter, 0)

    @pl.when(i + 1 == n_steps)
    def _():
        scattered(slot).wait()


def _dispatch_call(cum_cls, nblk_cls, nused, dest2d, rows, n_rows):
    n_tok = rows.shape[0] // SUBLANES
    grid_spec = pltpu.PrefetchScalarGridSpec(
        num_scalar_prefetch=3,
        grid=(n_tok // TD,),
        in_specs=[pl.BlockSpec(memory_space=pl.ANY), pl.BlockSpec(memory_space=pl.ANY)],
        out_specs=pl.BlockSpec(memory_space=pl.ANY),
        scratch_shapes=[pltpu.SMEM((2, TD), I32), pltpu.VMEM((2, TD * SUBLANES, LANES), U32),
                        pltpu.VMEM((RB * SUBLANES, LANES), U32),
                        pltpu.SemaphoreType.DMA((2,)), pltpu.SemaphoreType.DMA((2,)),
                        pltpu.SemaphoreType.DMA((2,)), pltpu.SemaphoreType.DMA],
    )
    return pl.pallas_call(
        _dispatch_kernel,
        out_shape=jax.ShapeDtypeStruct((n_rows * SUBLANES, LANES), U32),
        grid_spec=grid_spec,
        compiler_params=pltpu.CompilerParams(dimension_semantics=("arbitrary",),
                                             vmem_limit_bytes=VMEM_LIMIT),
        name="dispatch_rows",
    )(cum_cls, nblk_cls, nused, dest2d, rows)


def _expert_kernel(ea_ref, eb_ref, nused_ref, valid_ref, rows_ref, wgu_a_ref, wd_a_ref, wgu_b_ref, wd_b_ref, y_ref):
    b = pl.program_id(0)
    live = b < nused_ref[0]
    half_rows = RB // 2

    def evaluate(m):
        def tile_row(k):
            return rows_ref[pl.ds(k, m, stride=SUBLANES), :]

        words = [tile_row(k) for k in range(PACK_ROWS)]
        lo = [lax.bitcast_convert_type(jnp.left_shift(w, 16), F32) for w in words]
        hi = [lax.bitcast_convert_type(jnp.bitwise_and(w, jnp.uint32(HI_MASK)), F32) for w in words]
        x = jnp.concatenate(lo + hi, axis=1).astype(MXU_DTYPE)
        gate = lax.bitcast_convert_type(tile_row(PACK_ROWS), F32)

        def expert(wgu_ref, wd_ref):
            gu = _mdot(x, wgu_ref[0])
            act = jax.nn.silu(gu[:, :D_EXPERT]) * gu[:, D_EXPERT:]
            return _mdot(act.astype(MXU_DTYPE), wd_ref[0])

        y = expert(wgu_a_ref, wd_a_ref) * gate[:, 0:1] + expert(wgu_b_ref, wd_b_ref) * gate[:, 1:2]
        for k in range(Y_ROWS):
            y_ref[pl.ds(k, m, stride=SUBLANES), :] = y[:, k * LANES:(k + 1) * LANES]

    @pl.when(live & (valid_ref[b] > half_rows))
    def _():
        evaluate(RB)

    @pl.when(live & (valid_ref[b] <= half_rows))
    def _():
        evaluate(half_rows)
        y_ref[half_rows * Y_ROWS:, :] = jnp.zeros((half_rows * Y_ROWS, LANES), F32)

    @pl.when(jnp.logical_not(live))
    def _():
        y_ref[...] = jnp.zeros_like(y_ref)


def _expert_call(blk_ea, blk_eb, nused, blk_valid, rows, w_gate_up, w_down):
    n_rows = rows.shape[0] // SUBLANES
    nblk = n_rows // RB

    def live(b, nu):
        return jnp.maximum(jnp.minimum(b, nu[0] - 1), 0)

    grid_spec = pltpu.PrefetchScalarGridSpec(
        num_scalar_prefetch=4,
        grid=(nblk,),
        in_specs=[pl.BlockSpec((RB * SUBLANES, LANES), lambda b, ea, eb, nu, nv: (live(b, nu), 0)),
                  pl.BlockSpec((1,) + w_gate_up.shape[1:], lambda b, ea, eb, nu, nv: (ea[live(b, nu)], 0, 0)),
                  pl.BlockSpec((1,) + w_down.shape[1:], lambda b, ea, eb, nu, nv: (ea[live(b, nu)], 0, 0)),
                  pl.BlockSpec((1,) + w_gate_up.shape[1:], lambda b, ea, eb, nu, nv: (eb[live(b, nu)], 0, 0)),
                  pl.BlockSpec((1,) + w_down.shape[1:], lambda b, ea, eb, nu, nv: (eb[live(b, nu)], 0, 0))],
        out_specs=pl.BlockSpec((RB * Y_ROWS, LANES), lambda b, ea, eb, nu, nv: (b, 0)),
    )
    return pl.pallas_call(
        _expert_kernel,
        out_shape=jax.ShapeDtypeStruct((n_rows * Y_ROWS, LANES), F32),
        grid_spec=grid_spec,
        compiler_params=pltpu.CompilerParams(dimension_semantics=("arbitrary",),
                                             vmem_limit_bytes=VMEM_LIMIT),
        name="pair_experts",
    )(blk_ea, blk_eb, nused, blk_valid, rows, w_gate_up, w_down, w_gate_up, w_down)


def _combine_kernel(dest_hbm, y_hbm, x1_ref, mod_ref, g_ref, b_ref, o_ref, idx_smem, ybuf, sem_idx, sem_row,
                    *, tile_off):
    ns = pl.num_programs(1)
    i = pl.program_id(0) * ns + pl.program_id(1)
    n_steps = pl.num_programs(0) * ns
    slot = i % 2

    def idx_copy(step, s):
        return pltpu.make_async_copy(dest_hbm.at[tile_off + step], idx_smem.at[s], sem_idx.at[s])

    def issue_gather(s):
        def gather(g, carry):
            for u in range(ROW_UNROLL):
                t = g * ROW_UNROLL + u
                src = pl.multiple_of(idx_smem[s, t] * Y_ROWS, Y_ROWS)
                dst = pl.multiple_of(t * Y_ROWS, Y_ROWS)
                pltpu.make_async_copy(y_hbm.at[pl.ds(src, Y_ROWS)], ybuf.at[s, pl.ds(dst, Y_ROWS)],
                                      sem_row.at[s]).start(priority=u % 2)
            return carry

        lax.fori_loop(0, TD // ROW_UNROLL, gather, 0)

    @pl.when(i == 0)
    def _():
        idx_copy(0, 0).start()
        idx_copy(0, 0).wait()
        issue_gather(0)

        @pl.when(n_steps > 1)
        def _():
            idx_copy(1, 1).start()

    @pl.when(i + 1 < n_steps)
    def _():
        idx_copy(i + 1, 1 - slot).wait()
        issue_gather(1 - slot)

    @pl.when(i + 2 < n_steps)
    def _():
        idx_copy(i + 2, slot).start()

    pltpu.make_async_copy(y_hbm.at[pl.ds(0, TD * Y_ROWS)], ybuf.at[slot], sem_row.at[slot]).wait()

    y = jnp.concatenate([ybuf[slot, pl.ds(k, TD, stride=Y_ROWS), :] for k in range(Y_ROWS)], axis=1)
    g2 = mod_ref[0][5:6]
    o_ref[0] = _layer_norm(ALPHA * x1_ref[...] + g2 * y, g_ref[...], b_ref[...])


def _combine_call(dest2d, y_rows, x1, mod, mod_off, tile_off, ln2g, ln2b, bsz, seq):
    d = x1.shape[1]
    ns = seq // TD
    return pl.pallas_call(
        functools.partial(_combine_kernel, tile_off=tile_off),
        out_shape=jax.ShapeDtypeStruct((bsz, seq, d), F32),
        grid=(bsz, ns),
        in_specs=[pl.BlockSpec(memory_space=pl.ANY),
                  pl.BlockSpec(memory_space=pl.ANY),
                  pl.BlockSpec((TD, d), lambda b, s: (tile_off + b * ns + s, 0)),
                  pl.BlockSpec((1, N_MOD, d), lambda b, s: (mod_off + b, 0, 0)),
                  pl.BlockSpec((1, d), lambda b, s: (0, 0)),
                  pl.BlockSpec((1, d), lambda b, s: (0, 0))],
        out_specs=pl.BlockSpec((1, TD, d), lambda b, s: (b, s, 0)),
        scratch_shapes=[pltpu.SMEM((2, TD), I32), pltpu.VMEM((2, TD * Y_ROWS, LANES), F32),
                        pltpu.SemaphoreType.DMA((2,)), pltpu.SemaphoreType.DMA((2,))],
        compiler_params=pltpu.CompilerParams(dimension_semantics=("arbitrary", "arbitrary"),
                                             vmem_limit_bytes=VMEM_LIMIT),
        name="combine_ln2",
    )(dest2d, y_rows, x1, mod, ln2g, ln2b)


def _band_matrices():
    rows = np.arange(CHUNK)[:, None]
    cols = np.arange(CHUNK + 2 * POOL_HALO)[None, :] - POOL_HALO
    return np.stack([((cols >= rows - w // 2) & (cols < rows + w // 2)) for w in POOL_WINDOWS]).astype(np.float32)


def _class_tables():
    ea, eb = [], []
    for g in range(N_GROUPS):
        for a in range(EPG):
            for b in range(a + 1, EPG):
                ea.append(g * EPG + a)
                eb.append(g * EPG + b)
    return np.asarray(ea, np.int32), np.asarray(eb, np.int32)


def _split_hi_lo(w):
    hi = w.astype(MXU_DTYPE)
    lo = (w - hi.astype(F32)).astype(MXU_DTYPE)
    return hi, lo


def kernel(x_prompt, x_sample, c_prompt, c_sample, w_ada, b_ada, w_in, v_norm_g, v_norm_b, w_spatial, b_spatial,
           w_pool, pool_scale, w_out, ln1_g, ln1_b, w_route_group, b_route_group, w_route_expert, b_route_expert,
           w_gate_up, w_down, ln2_g, ln2_b):
    assert w_ada.shape[0] == 1, "single-layer kernel"
    bp, sp, d = x_prompt.shape
    bs, ss, _ = x_sample.shape
    assert d == D_MODEL and sp % TS == 0 and ss % TS == 0
    n_prompt, n_sample = bp * sp, bs * ss
    n_tok = n_prompt + n_sample
    assert n_tok % BL == 0 and n_prompt % TD == 0

    c_all = jnp.concatenate([c_prompt, c_sample], axis=0)
    mod = _modulation(c_all, w_ada[0], b_ada).reshape(bp + bs, N_MOD, d)

    wr = jnp.concatenate([w_route_group[0], w_route_expert[0]], axis=1).T
    wr = jnp.pad(wr, ((0, ROUTE_PAD - N_ROUTE), (0, 0)))
    wr_hi, wr_lo = _split_hi_lo(wr)
    rbias = jnp.concatenate([b_route_group[0], b_route_expert[0].reshape(-1),
                             jnp.zeros((ROUTE_PAD - N_ROUTE,), F32)]).reshape(ROUTE_PAD, 1)
    row = lambda a: a.reshape(1, -1)
    weights = (w_in[0].astype(MXU_DTYPE), row(v_norm_g[0]), row(v_norm_b[0]), w_spatial[0].astype(MXU_DTYPE),
               b_spatial[0].T, jnp.asarray(_band_matrices(), MXU_DTYPE), w_pool[0].astype(MXU_DTYPE),
               row(pool_scale[0]), w_out[0].astype(MXU_DTYPE), row(ln1_g[0]), row(ln1_b[0]),
               jnp.concatenate([wr_hi, wr_lo], axis=0), rbias)

    x1, rows, rid = _mixer_call(x_prompt, x_sample, mod, weights)

    nblk = (n_tok + N_CLASSES * (RB - 1)) // RB
    nb_pad = -(-nblk // 128) * 128
    u_mat = jnp.asarray(np.triu(np.ones((SUB, SUB), np.float32)), MXU_DTYPE)
    tri = jnp.asarray(np.tril(np.ones((CLS_PAD, CLS_PAD), np.float32)))
    dest, meta = _rank_call(rid, u_mat, tri, nb_pad)
    dest2d = dest.reshape(n_tok // TD, TD)
    tab_a, tab_b = _class_tables()
    blk_cls = meta[0, :nblk]
    blk_ea = jnp.asarray(tab_a)[blk_cls]
    blk_eb = jnp.asarray(tab_b)[blk_cls]
    nused = meta[1, 0:1]
    cum_cls = meta[2, :CLS_PAD]
    nblk_cls = meta[3, :CLS_PAD]

    sorted_rows = _dispatch_call(cum_cls, nblk_cls, nused, dest2d, rows, nblk * RB)
    y_rows = _expert_call(blk_ea, blk_eb, nused, meta[4, :nblk], sorted_rows, w_gate_up[0].astype(MXU_DTYPE),
                          w_down[0].astype(MXU_DTYPE))

    ln2g, ln2b = row(ln2_g[0]), row(ln2_b[0])
    y_prompt = _combine_call(dest2d, y_rows, x1, mod, 0, 0, ln2g, ln2b, bp, sp)
    y_sample = _combine_call(dest2d, y_rows, x1, mod, bp, n_prompt // TD, ln2g, ln2b, bs, ss)
    return (y_prompt, y_sample)
```

```python
import functools

import numpy as np
import jax
import jax.numpy as jnp
from jax import lax
from jax.experimental import pallas as pl
from jax.experimental.pallas import tpu as pltpu

F32 = jnp.float32
I32 = jnp.int32
MXU_DTYPE = jnp.bfloat16

D_MODEL = 1024
A_WIDTH = 512
B_WIDTH = 512
CHUNK = 128
A_HEADS = 4
HEAD_DIM = A_WIDTH // A_HEADS
POOL_WINDOWS = (2, 4, 8, 16)
POOL_HALO = 8
GROUP_DIM = B_WIDTH // len(POOL_WINDOWS)
N_GROUPS = 4
EPG = 8
N_EXPERTS = N_GROUPS * EPG
D_EXPERT = 512
N_MOD = 6
LN_EPS = 1e-5
ALPHA = 2.0 ** 0.25

PAIRS = EPG * (EPG - 1) // 2
N_CLASSES = N_GROUPS * PAIRS
CLS_PAD = 128
N_ROUTE = N_GROUPS + N_EXPERTS
ROUTE_PAD = 40
U32 = jnp.uint32
LANES = 128
SUBLANES = 8
PACK_ROWS = D_MODEL // (2 * LANES)
Y_ROWS = D_MODEL // LANES
assert PACK_ROWS < SUBLANES and Y_ROWS == SUBLANES
HI_MASK = 0xFFFF0000

TS = 512
TD = 512
RB = 512
ZERO_ROWS = 128
ROW_UNROLL = 8
BL = 2048
SUB = 256
VMEM_LIMIT = 56 * 1024 * 1024

_NT = (((1,), (1,)), ((), ()))

_GELU_K1 = -2.0 * (2.0 / np.pi) ** 0.5
_GELU_K3 = _GELU_K1 * 0.044715


def _gelu_tanh(x):
    return x / (1.0 + jnp.exp(x * (_GELU_K1 + _GELU_K3 * (x * x))))


def _layer_norm(x, g, b):
    mu = jnp.mean(x, axis=-1, keepdims=True)
    xc = x - mu
    var = jnp.mean(xc * xc, axis=-1, keepdims=True)
    return xc * lax.rsqrt(var + LN_EPS) * g + b


def _mdot(a, b):
    return jnp.dot(a, b, preferred_element_type=F32)


def _mod_kernel(c_ref, w_ref, b_ref, o_ref):
    a = jax.nn.silu(c_ref[...])
    o_ref[...] = jnp.dot(a, w_ref[...], precision=lax.Precision.HIGHEST,
                         preferred_element_type=F32) + b_ref[...]


def _modulation(c_all, w_ada, b_ada):
    nb, d = c_all.shape
    n = w_ada.shape[1]
    bn = 1536
    return pl.pallas_call(
        _mod_kernel,
        out_shape=jax.ShapeDtypeStruct((nb, n), F32),
        grid=(n // bn,),
        in_specs=[pl.BlockSpec((nb, d), lambda j: (0, 0)),
                  pl.BlockSpec((d, bn), lambda j: (0, j)),
                  pl.BlockSpec((1, bn), lambda j: (0, j))],
        out_specs=pl.BlockSpec((nb, bn), lambda j: (0, j)),
        compiler_params=pltpu.CompilerParams(dimension_semantics=("arbitrary",),
                                             vmem_limit_bytes=VMEM_LIMIT),
        name="adaln_mod",
    )(c_all, w_ada, b_ada)


def _mixer_kernel(xp_ref, xpp_ref, xpn_ref, xs_ref, xsp_ref, xsn_ref, mod_ref, modp_ref, inv_ref, win_ref, vng_ref,
                  vnb_ref, ws_ref, bst_ref, band_ref, wpool_ref, pscale_ref, wout_ref, ln1g_ref, ln1b_ref, wr_ref,
                  rb_ref, x1_ref, rows_ref, rid_ref, res_ref, *, prompt_tiles, prompt_seq, sample_seq):
    ts = xp_ref.shape[1]
    i = pl.program_id(0)
    slot = i % 2

    @pl.when(i == 0)
    def _():
        res_ref[1] = jnp.zeros(res_ref.shape[1:], F32)

    is_p = i < prompt_tiles
    ns = jnp.where(is_p, prompt_seq // ts, sample_seq // ts)
    s = jnp.where(is_p, i, i - prompt_tiles) % ns
    md = mod_ref[0]
    sh1, sc1, g1 = md[0:1], md[1:2], md[2:3]

    xt = jnp.where(is_p, xp_ref[0], xs_ref[0])
    h = xt * (1.0 + sc1) + sh1
    hp = jnp.where(s > 0, jnp.where(is_p, xpp_ref[0], xsp_ref[0]) * (1.0 + sc1) + sh1, 0.0)
    hn = jnp.where(s < ns - 1, jnp.where(is_p, xpn_ref[0], xsn_ref[0]) * (1.0 + sc1) + sh1, 0.0)
    hext = jnp.concatenate([h, hp, hn], axis=0).astype(MXU_DTYPE)
    zext = _mdot(hext, win_ref[...])
    z = zext[:ts]

    za = _gelu_tanh(z[:, :2 * A_WIDTH])
    u = za[:, :A_WIDTH]
    v = _layer_norm(za[:, A_WIDTH:], vng_ref[...], vnb_ref[...]).astype(MXU_DTYPE)
    bst = bst_ref[...]
    n_chunks = ts // CHUNK

    def chunks_on_lanes(a, rows, col0, width):
        return jnp.concatenate([a[c * CHUNK:c * CHUNK + rows, col0:col0 + width] for c in range(n_chunks)], axis=1)

    def chunks_on_rows(a, width):
        return jnp.concatenate([a[:, c * width:(c + 1) * width] for c in range(n_chunks)], axis=0)

    heads = [_mdot(ws_ref[hh], chunks_on_lanes(v, CHUNK, hh * HEAD_DIM, HEAD_DIM)) + bst[:, hh:hh + 1]
             for hh in range(A_HEADS)]
    a_out = u * jnp.concatenate([chunks_on_rows(o, HEAD_DIM) for o in heads], axis=1)

    p = z[:, 2 * A_WIDTH:]
    pall = jnp.concatenate([zext[ts:ts + POOL_HALO, 2 * A_WIDTH:], p,
                            zext[ts + POOL_HALO:ts + 2 * POOL_HALO, 2 * A_WIDTH:]], axis=0)
    p_hi = pall.astype(MXU_DTYPE)
    p_lo = (pall - p_hi.astype(F32)).astype(MXU_DTYPE)
    b_cols = []
    for g in range(len(POOL_WINDOWS)):
        seg_rows = CHUNK + 2 * POOL_HALO
        win = (_mdot(band_ref[g], chunks_on_lanes(p_hi, seg_rows, g * GROUP_DIM, GROUP_DIM))
               + _mdot(band_ref[g], chunks_on_lanes(p_lo, seg_rows, g * GROUP_DIM, GROUP_DIM)))
        pooled = chunks_on_rows(win, GROUP_DIM) * inv_ref[0, g] - p[:, g * GROUP_DIM:(g + 1) * GROUP_DIM]
        b_cols.append(_mdot(pooled.astype(MXU_DTYPE), wpool_ref[g]))
    b_out = jnp.concatenate(b_cols, axis=1) * pscale_ref[...]

    mix_in = jnp.concatenate([a_out, b_out], axis=1).astype(MXU_DTYPE)
    mdp = modp_ref[0]
    h2_prev, lt_prev = _ln1_router(res_ref[1 - slot], mdp[3:4], mdp[4:5], ln1g_ref, ln1b_ref, wr_ref, rb_ref,
                                   x1_ref)
    _route_and_pack(h2_prev, lt_prev, rows_ref, rid_ref)

    mix = _mdot(mix_in, wout_ref[...])
    res_ref[slot] = ALPHA * xt + g1 * mix


def _ln1_router(res, sh2, sc2, ln1g_ref, ln1b_ref, wr_ref, rb_ref, x1_ref):
    x1 = _layer_norm(res, ln1g_ref[...], ln1b_ref[...])
    x1_ref[...] = x1
    h2 = x1 * (1.0 + sc2) + sh2
    h2_hi = h2.astype(MXU_DTYPE)
    h2_lo = (h2 - h2_hi.astype(F32)).astype(MXU_DTYPE)
    l1 = lax.dot_general(wr_ref[...], h2_hi, _NT, preferred_element_type=F32)
    l2 = lax.dot_general(wr_ref[0:ROUTE_PAD], h2_lo, _NT, preferred_element_type=F32)
    return h2, l1[:ROUTE_PAD] + l1[ROUTE_PAD:] + l2 + rb_ref[...]


def _route_and_pack(h2, lt, rows_ref, rid_ref):
    ts = h2.shape[0]

    def row(r):
        return lt[r:r + 1, :]

    gl = [row(r) for r in range(N_GROUPS)]
    gmax = jnp.maximum(jnp.maximum(gl[0], gl[1]), jnp.maximum(gl[2], gl[3]))
    gidx = jnp.where(gl[0] == gmax, 0, jnp.where(gl[1] == gmax, 1, jnp.where(gl[2] == gmax, 2, 3)))
    gsum = (jnp.exp(gl[0] - gmax) + jnp.exp(gl[1] - gmax)) + (jnp.exp(gl[2] - gmax) + jnp.exp(gl[3] - gmax))
    gw = 1.0 / gsum
    ev = [jnp.where(gidx == 0, row(N_GROUPS + j),
                    jnp.where(gidx == 1, row(N_GROUPS + EPG + j),
                              jnp.where(gidx == 2, row(N_GROUPS + 2 * EPG + j), row(N_GROUPS + 3 * EPG + j))))
          for j in range(EPG)]

    def top1(vals):
        m = vals[0]
        for t in vals[1:]:
            m = jnp.maximum(m, t)
        idx = jnp.full(m.shape, EPG - 1, I32)
        for j in range(EPG - 2, -1, -1):
            idx = jnp.where(vals[j] == m, j, idx)
        return m, idx

    v1, j1 = top1(ev)
    v2, j2 = top1([jnp.where(j1 == j, -jnp.inf, ev[j]) for j in range(EPG)])
    t2 = jnp.exp(v2 - v1)
    den = 1.0 + t2
    w1 = (1.0 / den) * gw
    w2 = (t2 / den) * gw
    first = j1 < j2
    ea = jnp.minimum(j1, j2)
    eb = jnp.maximum(j1, j2)
    wa = jnp.where(first, w1, w2)
    wb = jnp.where(first, w2, w1)
    cls = gidx * PAIRS + jnp.right_shift(ea * (2 * EPG - 1 - ea), 1) + (eb - ea - 1)

    r8 = lax.broadcasted_iota(I32, (8, ts), 0)
    rid_ref[...] = jnp.where(r8 == 0, cls, jnp.where(r8 == 1, gidx * EPG + ea,
                                                      jnp.where(r8 == 2, gidx * EPG + eb, 0)))
    w8 = jnp.where(r8 == 0, wa, jnp.where(r8 == 1, wb, 0.0))
    w128 = jnp.concatenate([w8, jnp.zeros((LANES - 8, ts), F32)], axis=0)
    bits = lax.bitcast_convert_type(h2.astype(jnp.bfloat16).astype(F32), U32)
    half = D_MODEL // 2
    for k in range(PACK_ROWS):
        lo = jnp.right_shift(bits[:, k * LANES:(k + 1) * LANES], 16)
        hi = jnp.bitwise_and(bits[:, half + k * LANES:half + (k + 1) * LANES], jnp.uint32(HI_MASK))
        rows_ref[pl.ds(k, ts, stride=SUBLANES), :] = jnp.bitwise_or(lo, hi)
    rows_ref[pl.ds(PACK_ROWS, ts, stride=SUBLANES), :] = lax.bitcast_convert_type(w128.T, U32)
    for k in range(PACK_ROWS + 1, SUBLANES):
        rows_ref[pl.ds(k, ts, stride=SUBLANES), :] = jnp.zeros((ts, LANES), U32)


def _inv_population_table():
    r = np.arange(TS)
    out = np.empty((3, len(POOL_WINDOWS), TS, LANES), np.float32)
    for case in range(3):
        for g, w in enumerate(POOL_WINDOWS):
            lo = np.maximum(r - w // 2, 0) if case == 0 else r - w // 2
            hi = np.minimum(r + w // 2, TS) if case == 2 else r + w // 2
            out[case, g] = (1.0 / (hi - lo).astype(np.float64)).astype(np.float32)[:, None]
    return out


def _mixer_call(x_prompt, x_sample, mod, weights):
    bp, sp, d = x_prompt.shape
    bs, ss, _ = x_sample.shape
    nsp, nss = sp // TS, ss // TS
    assert nsp >= 2 and nss >= 2, "a tile is the first or the last of its sequence, not both"
    ntp, nts = bp * nsp, bs * nss
    n_tok = bp * sp + bs * ss
    hb = TS // POOL_HALO

    def p_tile(i):
        t = jnp.minimum(i, ntp - 1)
        return t // nsp, t % nsp

    def s_tile(i):
        t = jnp.clip(i - ntp, 0, nts - 1)
        return t // nss, t % nss

    def specs(tile_fn, seq):
        def cur(i):
            b, s = tile_fn(i)
            return (b, s, 0)

        def prev(i):
            b, s = tile_fn(i)
            return (b, jnp.maximum(s * hb - 1, 0), 0)

        def nxt(i):
            b, s = tile_fn(i)
            return (b, jnp.minimum((s + 1) * hb, seq // POOL_HALO - 1), 0)

        return [pl.BlockSpec((1, TS, d), cur), pl.BlockSpec((1, POOL_HALO, d), prev),
                pl.BlockSpec((1, POOL_HALO, d), nxt)]

    def mod_map(i):
        return (jnp.where(i < ntp, p_tile(i)[0], bp + s_tile(i)[0]), 0, 0)

    def inv_map(i):
        s = jnp.where(i < ntp, p_tile(i)[1], s_tile(i)[1])
        last = jnp.where(i < ntp, nsp - 1, nss - 1)
        return (jnp.where(s == 0, 0, jnp.where(s == last, 2, 1)), 0, 0, 0)

    def const(w):
        return pl.BlockSpec(w.shape, lambda i, nd=w.ndim: (0,) * nd)

    def prev_tile(i):
        return jnp.maximum(i - 1, 0)

    inv_tab = jnp.asarray(_inv_population_table())
    in_specs = (specs(p_tile, sp) + specs(s_tile, ss)
                + [pl.BlockSpec((1, N_MOD, d), mod_map),
                   pl.BlockSpec((1, N_MOD, d), lambda i: mod_map(prev_tile(i))),
                   pl.BlockSpec((1,) + inv_tab.shape[1:], inv_map)]
                + [const(w) for w in weights])
    out_shape = (jax.ShapeDtypeStruct((n_tok, d), F32),
                 jax.ShapeDtypeStruct((n_tok * SUBLANES, LANES), U32),
                 jax.ShapeDtypeStruct((8, n_tok), I32))
    out_specs = (pl.BlockSpec((TS, d), lambda i: (prev_tile(i), 0)),
                 pl.BlockSpec((TS * SUBLANES, LANES), lambda i: (prev_tile(i), 0)),
                 pl.BlockSpec((8, TS), lambda i: (0, prev_tile(i))))
    return pl.pallas_call(
        functools.partial(_mixer_kernel, prompt_tiles=ntp, prompt_seq=sp, sample_seq=ss),
        out_shape=out_shape,
        grid=(ntp + nts + 1,),
        in_specs=in_specs,
        out_specs=out_specs,
        scratch_shapes=[pltpu.VMEM((2, TS, d), F32)],
        compiler_params=pltpu.CompilerParams(dimension_semantics=("arbitrary",),
                                             vmem_limit_bytes=VMEM_LIMIT),
        name="mixer_ln1_route",
    )(x_prompt, x_prompt, x_prompt, x_sample, x_sample, x_sample, mod, mod, inv_tab, *weights)


def _rank_kernel(rid_ref, u_ref, tri_ref, dest_ref, meta_ref, cnt_ref, base_ref, *, nb_pad):
    phase = pl.program_id(0)
    j = pl.program_id(1)
    cls_iota = lax.broadcasted_iota(I32, (CLS_PAD, SUB), 0)

    @pl.when((phase == 0) & (j == 0))
    def _():
        cnt_ref[...] = jnp.zeros_like(cnt_ref)

    @pl.when(phase == 0)
    def _():
        acc = cnt_ref[...]
        for sb in range(BL // SUB):
            ids = rid_ref[0:1, sb * SUB:(sb + 1) * SUB]
            acc = acc + jnp.sum((cls_iota == ids).astype(F32), axis=1, keepdims=True)
        cnt_ref[...] = acc

    @pl.when((phase == 1) & (j == 0))
    def _():
        cnt = jnp.broadcast_to(cnt_ref[...], (CLS_PAD, CLS_PAD))
        nblk = jnp.floor((cnt + (RB - 1)) * (1.0 / RB))
        cum = jnp.dot(tri_ref[...], nblk, precision=lax.Precision.HIGHEST, preferred_element_type=F32)
        base_ref[...] = cum[:, 0:1] * RB - cnt[:, 0:1]
        blk = lax.broadcasted_iota(I32, (CLS_PAD, nb_pad), 1).astype(F32)
        bcls = jnp.sum((cum[:, 0:1] <= blk).astype(F32), axis=0, keepdims=True)
        bcls = jnp.minimum(bcls, N_CLASSES - 1).astype(I32)
        nused = jnp.broadcast_to(cum[CLS_PAD - 1:CLS_PAD, 0:1], (1, nb_pad)).astype(I32)
        first = cum[:, 0:1] - nblk[:, 0:1]
        inside = (first <= blk) & (blk < cum[:, 0:1])
        fill = jnp.where(blk == first, cnt[:, 0:1] - (nblk[:, 0:1] - 1.0) * RB, float(RB))
        valid = jnp.sum(jnp.where(inside, fill, 0.0), axis=0, keepdims=True).astype(I32)
        cum_l = jnp.transpose(cum)[0:1].astype(I32)
        nblk_l = jnp.transpose(nblk)[0:1].astype(I32)
        pad = jnp.zeros((1, nb_pad - CLS_PAD), I32)
        r8 = lax.broadcasted_iota(I32, (8, nb_pad), 0)
        cum_row = jnp.concatenate([cum_l, pad], axis=1)
        nblk_row = jnp.concatenate([nblk_l, pad], axis=1)
        meta_ref[...] = jnp.where(r8 == 0, bcls, jnp.where(r8 == 1, nused,
                                  jnp.where(r8 == 2, cum_row, jnp.where(r8 == 3, nblk_row,
                                                                        jnp.where(r8 == 4, valid, 0)))))

    @pl.when(phase == 1)
    def _():
        base = base_ref[...]
        for sb in range(BL // SUB):
            ids = rid_ref[0:1, sb * SUB:(sb + 1) * SUB]
            hit = cls_iota == ids
            incl = _mdot(hit.astype(MXU_DTYPE), u_ref[...])
            slot = jnp.sum(jnp.where(hit, base + incl - 1.0, 0.0), axis=0, keepdims=True)
            dest_ref[0:1, sb * SUB:(sb + 1) * SUB] = slot.astype(I32)
            base = base + incl[:, SUB - 1:SUB]
        base_ref[...] = base


def _rank_call(rid, u_mat, tri, nb_pad):
    n_tok = rid.shape[1]
    nj = n_tok // BL
    return pl.pallas_call(
        functools.partial(_rank_kernel, nb_pad=nb_pad),
        out_shape=(jax.ShapeDtypeStruct((1, n_tok), I32), jax.ShapeDtypeStruct((8, nb_pad), I32)),
        grid=(2, nj),
        in_specs=[pl.BlockSpec((8, BL), lambda p, j: (0, j)),
                  pl.BlockSpec(u_mat.shape, lambda p, j: (0, 0)),
                  pl.BlockSpec(tri.shape, lambda p, j: (0, 0))],
        out_specs=(pl.BlockSpec((1, BL), lambda p, j: (0, p * j)),
                   pl.BlockSpec((8, nb_pad), lambda p, j: (0, 0))),
        scratch_shapes=[pltpu.VMEM((CLS_PAD, 1), F32), pltpu.VMEM((CLS_PAD, 1), F32)],
        compiler_params=pltpu.CompilerParams(dimension_semantics=("arbitrary", "arbitrary"),
                                             vmem_limit_bytes=VMEM_LIMIT),
        name="rank_tokens",
    )(rid, u_mat, tri)


def _dispatch_kernel(cum_ref, nblk_ref, nused_ref, valid_ref, dest_hbm, rows_hbm, out_hbm, idx_smem, rows_buf,
                     zero_ref, sem_idx, sem_in, sem_row, sem_zero):
    i = pl.program_id(0)
    n_steps = pl.num_programs(0)
    slot = i % 2
    tile_rows = TD * SUBLANES
    part_rows = ZERO_ROWS * SUBLANES
    parts = RB // ZERO_ROWS
    n_blocks = out_hbm.shape[0] // (RB * SUBLANES)

    def fetch(step, s):
        start = pl.multiple_of(step * tile_rows, tile_rows)
        return (pltpu.make_async_copy(dest_hbm.at[step], idx_smem.at[s], sem_idx.at[s]),
                pltpu.make_async_copy(rows_hbm.at[pl.ds(start, tile_rows)], rows_buf.at[s], sem_in.at[s]))

    def scattered(s):
        return pltpu.make_async_copy(rows_buf.at[s], out_hbm.at[pl.ds(0, tile_rows)], sem_row.at[s])

    def zero_part(part):
        start = pl.multiple_of(part * part_rows, part_rows)
        return pltpu.make_async_copy(zero_ref, out_hbm.at[pl.ds(start, part_rows)], sem_zero)

    @pl.when(i == 0)
    def _():
        zero_ref[...] = jnp.zeros_like(zero_ref)

        def class_parts(c, fn):
            first = cum_ref[c] - nblk_ref[c]
            padding = RB - valid_ref[jnp.maximum(first, 0)]
            for q in range(parts):
                @pl.when((nblk_ref[c] > 0) & (q * ZERO_ROWS < padding))
                def _(q=q):
                    fn(zero_part(first * parts + q))

        def start(c, carry):
            class_parts(c, lambda cp: cp.start())
            return carry

        def wait(c, carry):
            class_parts(c, lambda cp: cp.wait())
            return carry

        def start_tail(part, carry):
            zero_part(part).start()
            return carry

        def wait_tail(part, carry):
            zero_part(part).wait()
            return carry

        lax.fori_loop(0, N_CLASSES, start, 0)
        lax.fori_loop(nused_ref[0] * parts, n_blocks * parts, start_tail, 0)
        lax.fori_loop(0, N_CLASSES, wait, 0)
        lax.fori_loop(nused_ref[0] * parts, n_blocks * parts, wait_tail, 0)
        for cp in fetch(0, 0):
            cp.start()

    for cp in fetch(i, slot):
        cp.wait()

    @pl.when(i > 0)
    def _():
        scattered(1 - slot).wait()

    @pl.when(i + 1 < n_steps)
    def _():
        for cp in fetch(i + 1, 1 - slot):
            cp.start()

    def scatter(g, carry):
        for u in range(ROW_UNROLL):
            t = g * ROW_UNROLL + u
            src = pl.multiple_of(t * SUBLANES, SUBLANES)
            dst = pl.multiple_of(idx_smem[slot, t] * SUBLANES, SUBLANES)
            pltpu.make_async_copy(rows_buf.at[slot, pl.ds(src, SUBLANES)], out_hbm.at[pl.ds(dst, SUBLANES)],
                                  sem_row.at[slot]).start(priority=u % 2)
        return carry

    lax.fori_loop(0, TD // ROW_UNROLL, scatter, 0)

    @pl.when(i + 1 == n_steps)
    def _():
        scattered(slot).wait()


def _dispatch_call(cum_cls, nblk_cls, nused, blk_valid, dest2d, rows, n_rows):
    n_tok = rows.shape[0] // SUBLANES
    grid_spec = pltpu.PrefetchScalarGridSpec(
        num_scalar_prefetch=4,
        grid=(n_tok // TD,),
        in_specs=[pl.BlockSpec(memory_space=pl.ANY), pl.BlockSpec(memory_space=pl.ANY)],
        out_specs=pl.BlockSpec(memory_space=pl.ANY),
        scratch_shapes=[pltpu.SMEM((2, TD), I32), pltpu.VMEM((2, TD * SUBLANES, LANES), U32),
                        pltpu.VMEM((ZERO_ROWS * SUBLANES, LANES), U32),
                        pltpu.SemaphoreType.DMA((2,)), pltpu.SemaphoreType.DMA((2,)),
                        pltpu.SemaphoreType.DMA((2,)), pltpu.SemaphoreType.DMA],
    )
    return pl.pallas_call(
        _dispatch_kernel,
        out_shape=jax.ShapeDtypeStruct((n_rows * SUBLANES, LANES), U32),
        grid_spec=grid_spec,
        compiler_params=pltpu.CompilerParams(dimension_semantics=("arbitrary",),
                                             vmem_limit_bytes=VMEM_LIMIT),
        name="dispatch_rows",
    )(cum_cls, nblk_cls, nused, blk_valid, dest2d, rows)


def _expert_kernel(ea_ref, eb_ref, nused_ref, valid_ref, rows_ref, wgu_a_ref, wd_a_ref, wgu_b_ref, wd_b_ref, y_ref):
    b = pl.program_id(0)
    live = b < nused_ref[0]
    half_rows = RB // 2

    def evaluate(m):
        row0 = (RB - m) * SUBLANES

        def tile_row(k):
            return rows_ref[pl.ds(row0 + k, m, stride=SUBLANES), :]

        words = [tile_row(k) for k in range(PACK_ROWS)]
        lo = [lax.bitcast_convert_type(jnp.left_shift(w, 16), F32) for w in words]
        hi = [lax.bitcast_convert_type(jnp.bitwise_and(w, jnp.uint32(HI_MASK)), F32) for w in words]
        x = jnp.concatenate(lo + hi, axis=1).astype(MXU_DTYPE)
        gate = lax.bitcast_convert_type(tile_row(PACK_ROWS), F32)

        def expert(wgu_ref, wd_ref):
            gu = _mdot(x, wgu_ref[0])
            act = jax.nn.silu(gu[:, :D_EXPERT]) * gu[:, D_EXPERT:]
            return _mdot(act.astype(MXU_DTYPE), wd_ref[0])

        y = expert(wgu_a_ref, wd_a_ref) * gate[:, 0:1] + expert(wgu_b_ref, wd_b_ref) * gate[:, 1:2]
        for k in range(Y_ROWS):
            y_ref[pl.ds(row0 + k, m, stride=SUBLANES), :] = y[:, k * LANES:(k + 1) * LANES]

    @pl.when(live & (valid_ref[b] > half_rows))
    def _():
        evaluate(RB)

    @pl.when(live & (valid_ref[b] <= half_rows))
    def _():
        evaluate(half_rows)
        y_ref[:half_rows * Y_ROWS, :] = jnp.zeros((half_rows * Y_ROWS, LANES), F32)

    @pl.when(jnp.logical_not(live))
    def _():
        y_ref[...] = jnp.zeros_like(y_ref)


def _expert_call(blk_ea, blk_eb, nused, blk_valid, rows, w_gate_up, w_down):
    n_rows = rows.shape[0] // SUBLANES
    nblk = n_rows // RB

    def live(b, nu):
        return jnp.maximum(jnp.minimum(b, nu[0] - 1), 0)

    grid_spec = pltpu.PrefetchScalarGridSpec(
        num_scalar_prefetch=4,
        grid=(nblk,),
        in_specs=[pl.BlockSpec((RB * SUBLANES, LANES), lambda b, ea, eb, nu, nv: (live(b, nu), 0)),
                  pl.BlockSpec((1,) + w_gate_up.shape[1:], lambda b, ea, eb, nu, nv: (ea[live(b, nu)], 0, 0)),
                  pl.BlockSpec((1,) + w_down.shape[1:], lambda b, ea, eb, nu, nv: (ea[live(b, nu)], 0, 0)),
                  pl.BlockSpec((1,) + w_gate_up.shape[1:], lambda b, ea, eb, nu, nv: (eb[live(b, nu)], 0, 0)),
                  pl.BlockSpec((1,) + w_down.shape[1:], lambda b, ea, eb, nu, nv: (eb[live(b, nu)], 0, 0))],
        out_specs=pl.BlockSpec((RB * Y_ROWS, LANES), lambda b, ea, eb, nu, nv: (b, 0)),
    )
    return pl.pallas_call(
        _expert_kernel,
        out_shape=jax.ShapeDtypeStruct((n_rows * Y_ROWS, LANES), F32),
        grid_spec=grid_spec,
        compiler_params=pltpu.CompilerParams(dimension_semantics=("arbitrary",),
                                             vmem_limit_bytes=VMEM_LIMIT),
        name="pair_experts",
    )(blk_ea, blk_eb, nused, blk_valid, rows, w_gate_up, w_down, w_gate_up, w_down)


def _combine_kernel(dest_hbm, y_hbm, x1_ref, mod_ref, g_ref, b_ref, o_ref, idx_smem, ybuf, sem_idx, sem_row,
                    *, tile_off):
    ns = pl.num_programs(1)
    i = pl.program_id(0) * ns + pl.program_id(1)
    n_steps = pl.num_programs(0) * ns
    slot = i % 2

    def idx_copy(step, s):
        return pltpu.make_async_copy(dest_hbm.at[tile_off + step], idx_smem.at[s], sem_idx.at[s])

    def issue_gather(s):
        def gather(g, carry):
            for u in range(ROW_UNROLL):
                t = g * ROW_UNROLL + u
                src = pl.multiple_of(idx_smem[s, t] * Y_ROWS, Y_ROWS)
                dst = pl.multiple_of(t * Y_ROWS, Y_ROWS)
                pltpu.make_async_copy(y_hbm.at[pl.ds(src, Y_ROWS)], ybuf.at[s, pl.ds(dst, Y_ROWS)],
                                      sem_row.at[s]).start(priority=u % 2)
            return carry

        lax.fori_loop(0, TD // ROW_UNROLL, gather, 0)

    @pl.when(i == 0)
    def _():
        idx_copy(0, 0).start()
        idx_copy(0, 0).wait()
        issue_gather(0)

        @pl.when(n_steps > 1)
        def _():
            idx_copy(1, 1).start()

    @pl.when(i + 1 < n_steps)
    def _():
        idx_copy(i + 1, 1 - slot).wait()
        issue_gather(1 - slot)

    @pl.when(i + 2 < n_steps)
    def _():
        idx_copy(i + 2, slot).start()

    pltpu.make_async_copy(y_hbm.at[pl.ds(0, TD * Y_ROWS)], ybuf.at[slot], sem_row.at[slot]).wait()

    y = jnp.concatenate([ybuf[slot, pl.ds(k, TD, stride=Y_ROWS), :] for k in range(Y_ROWS)], axis=1)
    g2 = mod_ref[0][5:6]
    o_ref[0] = _layer_norm(ALPHA * x1_ref[...] + g2 * y, g_ref[...], b_ref[...])


def _combine_call(dest2d, y_rows, x1, mod, mod_off, tile_off, ln2g, ln2b, bsz, seq):
    d = x1.shape[1]
    ns = seq // TD
    return pl.pallas_call(
        functools.partial(_combine_kernel, tile_off=tile_off),
        out_shape=jax.ShapeDtypeStruct((bsz, seq, d), F32),
        grid=(bsz, ns),
        in_specs=[pl.BlockSpec(memory_space=pl.ANY),
                  pl.BlockSpec(memory_space=pl.ANY),
                  pl.BlockSpec((TD, d), lambda b, s: (tile_off + b * ns + s, 0)),
                  pl.BlockSpec((1, N_MOD, d), lambda b, s: (mod_off + b, 0, 0)),
                  pl.BlockSpec((1, d), lambda b, s: (0, 0)),
                  pl.BlockSpec((1, d), lambda b, s: (0, 0))],
        out_specs=pl.BlockSpec((1, TD, d), lambda b, s: (b, s, 0)),
        scratch_shapes=[pltpu.SMEM((2, TD), I32), pltpu.VMEM((2, TD * Y_ROWS, LANES), F32),
                        pltpu.SemaphoreType.DMA((2,)), pltpu.SemaphoreType.DMA((2,))],
        compiler_params=pltpu.CompilerParams(dimension_semantics=("arbitrary", "arbitrary"),
                                             vmem_limit_bytes=VMEM_LIMIT),
        name="combine_ln2",
    )(dest2d, y_rows, x1, mod, ln2g, ln2b)


def _band_matrices():
    rows = np.arange(CHUNK)[:, None]
    cols = np.arange(CHUNK + 2 * POOL_HALO)[None, :] - POOL_HALO
    return np.stack([((cols >= rows - w // 2) & (cols < rows + w // 2)) for w in POOL_WINDOWS]).astype(np.float32)


def _class_tables():
    ea, eb = [], []
    for g in range(N_GROUPS):
        for a in range(EPG):
            for b in range(a + 1, EPG):
                ea.append(g * EPG + a)
                eb.append(g * EPG + b)
    return np.asarray(ea, np.int32), np.asarray(eb, np.int32)


def _split_hi_lo(w):
    hi = w.astype(MXU_DTYPE)
    lo = (w - hi.astype(F32)).astype(MXU_DTYPE)
    return hi, lo


def kernel(x_prompt, x_sample, c_prompt, c_sample, w_ada, b_ada, w_in, v_norm_g, v_norm_b, w_spatial, b_spatial,
           w_pool, pool_scale, w_out, ln1_g, ln1_b, w_route_group, b_route_group, w_route_expert, b_route_expert,
           w_gate_up, w_down, ln2_g, ln2_b):
    assert w_ada.shape[0] == 1, "single-layer kernel"
    bp, sp, d = x_prompt.shape
    bs, ss, _ = x_sample.shape
    assert d == D_MODEL and sp % TS == 0 and ss % TS == 0
    n_prompt, n_sample = bp * sp, bs * ss
    n_tok = n_prompt + n_sample
    assert n_tok % BL == 0 and n_prompt % TD == 0

    c_all = jnp.concatenate([c_prompt, c_sample], axis=0)
    mod = _modulation(c_all, w_ada[0], b_ada).reshape(bp + bs, N_MOD, d)

    wr = jnp.concatenate([w_route_group[0], w_route_expert[0]], axis=1).T
    wr = jnp.pad(wr, ((0, ROUTE_PAD - N_ROUTE), (0, 0)))
    wr_hi, wr_lo = _split_hi_lo(wr)
    rbias = jnp.concatenate([b_route_group[0], b_route_expert[0].reshape(-1),
                             jnp.zeros((ROUTE_PAD - N_ROUTE,), F32)]).reshape(ROUTE_PAD, 1)
    row = lambda a: a.reshape(1, -1)
    weights = (w_in[0].astype(MXU_DTYPE), row(v_norm_g[0]), row(v_norm_b[0]), w_spatial[0].astype(MXU_DTYPE),
               b_spatial[0].T, jnp.asarray(_band_matrices(), MXU_DTYPE), w_pool[0].astype(MXU_DTYPE),
               row(pool_scale[0]), w_out[0].astype(MXU_DTYPE), row(ln1_g[0]), row(ln1_b[0]),
               jnp.concatenate([wr_hi, wr_lo], axis=0), rbias)

    x1, rows, rid = _mixer_call(x_prompt, x_sample, mod, weights)

    nblk = (n_tok + N_CLASSES * (RB - 1)) // RB
    nb_pad = -(-nblk // 128) * 128
    u_mat = jnp.asarray(np.triu(np.ones((SUB, SUB), np.float32)), MXU_DTYPE)
    tri = jnp.asarray(np.tril(np.ones((CLS_PAD, CLS_PAD), np.float32)))
    dest, meta = _rank_call(rid, u_mat, tri, nb_pad)
    dest2d = dest.reshape(n_tok // TD, TD)
    tab_a, tab_b = _class_tables()
    blk_cls = meta[0, :nblk]
    blk_ea = jnp.asarray(tab_a)[blk_cls]
    blk_eb = jnp.asarray(tab_b)[blk_cls]
    nused = meta[1, 0:1]
    cum_cls = meta[2, :CLS_PAD]
    nblk_cls = meta[3, :CLS_PAD]

    blk_valid = meta[4, :nblk]
    sorted_rows = _dispatch_call(cum_cls, nblk_cls, nused, blk_valid, dest2d, rows, nblk * RB)
    y_rows = _expert_call(blk_ea, blk_eb, nused, blk_valid, sorted_rows, w_gate_up[0].astype(MXU_DTYPE),
                          w_down[0].astype(MXU_DTYPE))

    ln2g, ln2b = row(ln2_g[0]), row(ln2_b[0])
    y_prompt = _combine_call(dest2d, y_rows, x1, mod, 0, 0, ln2g, ln2b, bp, sp)
    y_sample = _combine_call(dest2d, y_rows, x1, mod, bp, n_prompt // TD, ln2g, ln2b, bs, ss)
    return (y_prompt, y_sample)
```

```python
import functools

import numpy as np
import jax
import jax.numpy as jnp
from jax import lax
from jax.experimental import pallas as pl
from jax.experimental.pallas import tpu as pltpu

F32 = jnp.float32
I32 = jnp.int32
MXU_DTYPE = jnp.bfloat16

D_MODEL = 1024
A_WIDTH = 512
B_WIDTH = 512
CHUNK = 128
A_HEADS = 4
HEAD_DIM = A_WIDTH // A_HEADS
POOL_WINDOWS = (2, 4, 8, 16)
POOL_HALO = 8
GROUP_DIM = B_WIDTH // len(POOL_WINDOWS)
N_GROUPS = 4
EPG = 8
N_EXPERTS = N_GROUPS * EPG
D_EXPERT = 512
N_MOD = 6
LN_EPS = 1e-5
ALPHA = 2.0 ** 0.25

PAIRS = EPG * (EPG - 1) // 2
N_CLASSES = N_GROUPS * PAIRS
CLS_PAD = 128
N_ROUTE = N_GROUPS + N_EXPERTS
ROUTE_PAD = 40
U32 = jnp.uint32
LANES = 128
SUBLANES = 8
PACK_ROWS = D_MODEL // (2 * LANES)
Y_ROWS = D_MODEL // LANES
assert PACK_ROWS < SUBLANES and Y_ROWS == SUBLANES
HI_MASK = 0xFFFF0000

TS = 512
TD = 512
RB = 512
ZERO_ROWS = 128
ROW_UNROLL = 8
BL = 2048
SUB = 256
VMEM_LIMIT = 56 * 1024 * 1024

_NT = (((1,), (1,)), ((), ()))

_GELU_K1 = -2.0 * (2.0 / np.pi) ** 0.5
_GELU_K3 = _GELU_K1 * 0.044715


def _gelu_tanh(x):
    return x / (1.0 + jnp.exp(x * (_GELU_K1 + _GELU_K3 * (x * x))))


def _layer_norm(x, g, b):
    mu = jnp.mean(x, axis=-1, keepdims=True)
    xc = x - mu
    var = jnp.mean(xc * xc, axis=-1, keepdims=True)
    return xc * lax.rsqrt(var + LN_EPS) * g + b


def _mdot(a, b):
    return jnp.dot(a, b, preferred_element_type=F32)


def _mod_kernel(c_ref, w_ref, b_ref, o_ref):
    a = jax.nn.silu(c_ref[...])
    o_ref[...] = jnp.dot(a, w_ref[...], precision=lax.Precision.HIGHEST,
                         preferred_element_type=F32) + b_ref[...]


def _modulation(c_all, w_ada, b_ada):
    nb, d = c_all.shape
    n = w_ada.shape[1]
    bn = 1536
    return pl.pallas_call(
        _mod_kernel,
        out_shape=jax.ShapeDtypeStruct((nb, n), F32),
        grid=(n // bn,),
        in_specs=[pl.BlockSpec((nb, d), lambda j: (0, 0)),
                  pl.BlockSpec((d, bn), lambda j: (0, j)),
                  pl.BlockSpec((1, bn), lambda j: (0, j))],
        out_specs=pl.BlockSpec((nb, bn), lambda j: (0, j)),
        compiler_params=pltpu.CompilerParams(dimension_semantics=("arbitrary",),
                                             vmem_limit_bytes=VMEM_LIMIT),
        name="adaln_mod",
    )(c_all, w_ada, b_ada)


def _mixer_kernel(xp_ref, xpp_ref, xpn_ref, xs_ref, xsp_ref, xsn_ref, mod_ref, modp_ref, inv_ref, win_ref, vng_ref,
                  vnb_ref, ws_ref, bst_ref, band_ref, wpool_ref, pscale_ref, wout_ref, ln1g_ref, ln1b_ref, wr_ref,
                  rb_ref, x1_ref, rows_ref, rid_ref, res_ref, *, prompt_tiles, prompt_seq, sample_seq):
    ts = xp_ref.shape[1]
    i = pl.program_id(0)
    slot = i % 2

    @pl.when(i == 0)
    def _():
        res_ref[1] = jnp.zeros(res_ref.shape[1:], F32)

    is_p = i < prompt_tiles
    ns = jnp.where(is_p, prompt_seq // ts, sample_seq // ts)
    s = jnp.where(is_p, i, i - prompt_tiles) % ns
    md = mod_ref[0]
    sh1, sc1, g1 = md[0:1], md[1:2], md[2:3]

    xt = jnp.where(is_p, xp_ref[0], xs_ref[0])
    h = xt * (1.0 + sc1) + sh1
    hp = jnp.where(s > 0, jnp.where(is_p, xpp_ref[0], xsp_ref[0]) * (1.0 + sc1) + sh1, 0.0)
    hn = jnp.where(s < ns - 1, jnp.where(is_p, xpn_ref[0], xsn_ref[0]) * (1.0 + sc1) + sh1, 0.0)
    hext = jnp.concatenate([h, hp, hn], axis=0).astype(MXU_DTYPE)
    zext = _mdot(hext, win_ref[...])
    z = zext[:ts]

    za = _gelu_tanh(z[:, :2 * A_WIDTH])
    u = za[:, :A_WIDTH]
    v = _layer_norm(za[:, A_WIDTH:], vng_ref[...], vnb_ref[...]).astype(MXU_DTYPE)
    bst = bst_ref[...]
    n_chunks = ts // CHUNK

    def chunks_on_lanes(a, rows, col0, width):
        return jnp.concatenate([a[c * CHUNK:c * CHUNK + rows, col0:col0 + width] for c in range(n_chunks)], axis=1)

    def chunks_on_rows(a, width):
        return jnp.concatenate([a[:, c * width:(c + 1) * width] for c in range(n_chunks)], axis=0)

    heads = [_mdot(ws_ref[hh], chunks_on_lanes(v, CHUNK, hh * HEAD_DIM, HEAD_DIM)) + bst[:, hh:hh + 1]
             for hh in range(A_HEADS)]
    a_out = u * jnp.concatenate([chunks_on_rows(o, HEAD_DIM) for o in heads], axis=1)

    p = z[:, 2 * A_WIDTH:]
    pall = jnp.concatenate([zext[ts:ts + POOL_HALO, 2 * A_WIDTH:], p,
                            zext[ts + POOL_HALO:ts + 2 * POOL_HALO, 2 * A_WIDTH:]], axis=0)
    p_hi = pall.astype(MXU_DTYPE)
    p_lo = (pall - p_hi.astype(F32)).astype(MXU_DTYPE)
    b_cols = []
    for g in range(len(POOL_WINDOWS)):
        seg_rows = CHUNK + 2 * POOL_HALO
        win = (_mdot(band_ref[g], chunks_on_lanes(p_hi, seg_rows, g * GROUP_DIM, GROUP_DIM))
               + _mdot(band_ref[g], chunks_on_lanes(p_lo, seg_rows, g * GROUP_DIM, GROUP_DIM)))
        pooled = chunks_on_rows(win, GROUP_DIM) * inv_ref[0, g] - p[:, g * GROUP_DIM:(g + 1) * GROUP_DIM]
        b_cols.append(_mdot(pooled.astype(MXU_DTYPE), wpool_ref[g]))
    b_out = jnp.concatenate(b_cols, axis=1) * pscale_ref[...]

    mix_in = jnp.concatenate([a_out, b_out], axis=1).astype(MXU_DTYPE)
    mdp = modp_ref[0]
    h2_prev, lt_prev = _ln1_router(res_ref[1 - slot], mdp[3:4], mdp[4:5], ln1g_ref, ln1b_ref, wr_ref, rb_ref,
                                   x1_ref)
    _route_and_pack(h2_prev, lt_prev, rows_ref, rid_ref)

    mix = _mdot(mix_in, wout_ref[...])
    res_ref[slot] = ALPHA * xt + g1 * mix


def _ln1_router(res, sh2, sc2, ln1g_ref, ln1b_ref, wr_ref, rb_ref, x1_ref):
    x1 = _layer_norm(res, ln1g_ref[...], ln1b_ref[...])
    x1_ref[...] = x1
    h2 = x1 * (1.0 + sc2) + sh2
    h2_hi = h2.astype(MXU_DTYPE)
    h2_lo = (h2 - h2_hi.astype(F32)).astype(MXU_DTYPE)
    l1 = lax.dot_general(wr_ref[...], h2_hi, _NT, preferred_element_type=F32)
    l2 = lax.dot_general(wr_ref[0:ROUTE_PAD], h2_lo, _NT, preferred_element_type=F32)
    return h2, l1[:ROUTE_PAD] + l1[ROUTE_PAD:] + l2 + rb_ref[...]


def _route_and_pack(h2, lt, rows_ref, rid_ref):
    ts = h2.shape[0]

    def row(r):
        return lt[r:r + 1, :]

    gl = [row(r) for r in range(N_GROUPS)]
    gmax = jnp.maximum(jnp.maximum(gl[0], gl[1]), jnp.maximum(gl[2], gl[3]))
    gidx = jnp.where(gl[0] == gmax, 0, jnp.where(gl[1] == gmax, 1, jnp.where(gl[2] == gmax, 2, 3)))
    gsum = (jnp.exp(gl[0] - gmax) + jnp.exp(gl[1] - gmax)) + (jnp.exp(gl[2] - gmax) + jnp.exp(gl[3] - gmax))
    gw = 1.0 / gsum
    ev = [jnp.where(gidx == 0, row(N_GROUPS + j),
                    jnp.where(gidx == 1, row(N_GROUPS + EPG + j),
                              jnp.where(gidx == 2, row(N_GROUPS + 2 * EPG + j), row(N_GROUPS + 3 * EPG + j))))
          for j in range(EPG)]

    def top1(vals):
        m = vals[0]
        for t in vals[1:]:
            m = jnp.maximum(m, t)
        idx = jnp.full(m.shape, EPG - 1, I32)
        for j in range(EPG - 2, -1, -1):
            idx = jnp.where(vals[j] == m, j, idx)
        return m, idx

    v1, j1 = top1(ev)
    v2, j2 = top1([jnp.where(j1 == j, -jnp.inf, ev[j]) for j in range(EPG)])
    t2 = jnp.exp(v2 - v1)
    den = 1.0 + t2
    w1 = (1.0 / den) * gw
    w2 = (t2 / den) * gw
    first = j1 < j2
    ea = jnp.minimum(j1, j2)
    eb = jnp.maximum(j1, j2)
    wa = jnp.where(first, w1, w2)
    wb = jnp.where(first, w2, w1)
    cls = gidx * PAIRS + jnp.right_shift(ea * (2 * EPG - 1 - ea), 1) + (eb - ea - 1)

    r8 = lax.broadcasted_iota(I32, (8, ts), 0)
    rid_ref[...] = jnp.where(r8 == 0, cls, jnp.where(r8 == 1, gidx * EPG + ea,
                                                      jnp.where(r8 == 2, gidx * EPG + eb, 0)))
    w8 = jnp.where(r8 == 0, wa, jnp.where(r8 == 1, wb, 0.0))
    w128 = jnp.concatenate([w8, jnp.zeros((LANES - 8, ts), F32)], axis=0)
    bits = lax.bitcast_convert_type(h2.astype(jnp.bfloat16).astype(F32), U32)
    half = D_MODEL // 2
    for k in range(PACK_ROWS):
        lo = jnp.right_shift(bits[:, k * LANES:(k + 1) * LANES], 16)
        hi = jnp.bitwise_and(bits[:, half + k * LANES:half + (k + 1) * LANES], jnp.uint32(HI_MASK))
        rows_ref[pl.ds(k, ts, stride=SUBLANES), :] = jnp.bitwise_or(lo, hi)
    rows_ref[pl.ds(PACK_ROWS, ts, stride=SUBLANES), :] = lax.bitcast_convert_type(w128.T, U32)
    for k in range(PACK_ROWS + 1, SUBLANES):
        rows_ref[pl.ds(k, ts, stride=SUBLANES), :] = jnp.zeros((ts, LANES), U32)


def _inv_population_table():
    r = np.arange(TS)
    out = np.empty((3, len(POOL_WINDOWS), TS, LANES), np.float32)
    for case in range(3):
        for g, w in enumerate(POOL_WINDOWS):
            lo = np.maximum(r - w // 2, 0) if case == 0 else r - w // 2
            hi = np.minimum(r + w // 2, TS) if case == 2 else r + w // 2
            out[case, g] = (1.0 / (hi - lo).astype(np.float64)).astype(np.float32)[:, None]
    return out


def _mixer_call(x_prompt, x_sample, mod, weights):
    bp, sp, d = x_prompt.shape
    bs, ss, _ = x_sample.shape
    nsp, nss = sp // TS, ss // TS
    assert nsp >= 2 and nss >= 2, "a tile is the first or the last of its sequence, not both"
    ntp, nts = bp * nsp, bs * nss
    n_tok = bp * sp + bs * ss
    hb = TS // POOL_HALO

    def p_tile(i):
        t = jnp.minimum(i, ntp - 1)
        return t // nsp, t % nsp

    def s_tile(i):
        t = jnp.clip(i - ntp, 0, nts - 1)
        return t // nss, t % nss

    def specs(tile_fn, seq):
        def cur(i):
            b, s = tile_fn(i)
            return (b, s, 0)

        def prev(i):
            b, s = tile_fn(i)
            return (b, jnp.maximum(s * hb - 1, 0), 0)

        def nxt(i):
            b, s = tile_fn(i)
            return (b, jnp.minimum((s + 1) * hb, seq // POOL_HALO - 1), 0)

        return [pl.BlockSpec((1, TS, d), cur), pl.BlockSpec((1, POOL_HALO, d), prev),
                pl.BlockSpec((1, POOL_HALO, d), nxt)]

    def mod_map(i):
        return (jnp.where(i < ntp, p_tile(i)[0], bp + s_tile(i)[0]), 0, 0)

    def inv_map(i):
        s = jnp.where(i < ntp, p_tile(i)[1], s_tile(i)[1])
        last = jnp.where(i < ntp, nsp - 1, nss - 1)
        return (jnp.where(s == 0, 0, jnp.where(s == last, 2, 1)), 0, 0, 0)

    def const(w):
        return pl.BlockSpec(w.shape, lambda i, nd=w.ndim: (0,) * nd)

    def prev_tile(i):
        return jnp.maximum(i - 1, 0)

    inv_tab = jnp.asarray(_inv_population_table())
    in_specs = (specs(p_tile, sp) + specs(s_tile, ss)
                + [pl.BlockSpec((1, N_MOD, d), mod_map),
                   pl.BlockSpec((1, N_MOD, d), lambda i: mod_map(prev_tile(i))),
                   pl.BlockSpec((1,) + inv_tab.shape[1:], inv_map)]
                + [const(w) for w in weights])
    out_shape = (jax.ShapeDtypeStruct((n_tok, d), F32),
                 jax.ShapeDtypeStruct((n_tok * SUBLANES, LANES), U32),
                 jax.ShapeDtypeStruct((8, n_tok), I32))
    out_specs = (pl.BlockSpec((TS, d), lambda i: (prev_tile(i), 0)),
                 pl.BlockSpec((TS * SUBLANES, LANES), lambda i: (prev_tile(i), 0)),
                 pl.BlockSpec((8, TS), lambda i: (0, prev_tile(i))))
    return pl.pallas_call(
        functools.partial(_mixer_kernel, prompt_tiles=ntp, prompt_seq=sp, sample_seq=ss),
        out_shape=out_shape,
        grid=(ntp + nts + 1,),
        in_specs=in_specs,
        out_specs=out_specs,
        scratch_shapes=[pltpu.VMEM((2, TS, d), F32)],
        compiler_params=pltpu.CompilerParams(dimension_semantics=("arbitrary",),
                                             vmem_limit_bytes=VMEM_LIMIT),
        name="mixer_ln1_route",
    )(x_prompt, x_prompt, x_prompt, x_sample, x_sample, x_sample, mod, mod, inv_tab, *weights)


def _rank_kernel(rid_ref, u_ref, tri_ref, dest_ref, meta_ref, cnt_ref, base_ref, *, nb_pad):
    phase = pl.program_id(0)
    j = pl.program_id(1)
    cls_iota = lax.broadcasted_iota(I32, (CLS_PAD, SUB), 0)

    @pl.when((phase == 0) & (j == 0))
    def _():
        cnt_ref[...] = jnp.zeros_like(cnt_ref)

    @pl.when(phase == 0)
    def _():
        acc = cnt_ref[...]
        for sb in range(BL // SUB):
            ids = rid_ref[0:1, sb * SUB:(sb + 1) * SUB]
            acc = acc + jnp.sum((cls_iota == ids).astype(F32), axis=1, keepdims=True)
        cnt_ref[...] = acc

    @pl.when((phase == 1) & (j == 0))
    def _():
        cnt = jnp.broadcast_to(cnt_ref[...], (CLS_PAD, CLS_PAD))
        nblk = jnp.floor((cnt + (RB - 1)) * (1.0 / RB))
        cum = jnp.dot(tri_ref[...], nblk, precision=lax.Precision.HIGHEST, preferred_element_type=F32)
        base_ref[...] = cum[:, 0:1] * RB - cnt[:, 0:1]
        blk = lax.broadcasted_iota(I32, (CLS_PAD, nb_pad), 1).astype(F32)
        bcls = jnp.sum((cum[:, 0:1] <= blk).astype(F32), axis=0, keepdims=True)
        bcls = jnp.minimum(bcls, N_CLASSES - 1).astype(I32)
        nused = jnp.broadcast_to(cum[CLS_PAD - 1:CLS_PAD, 0:1], (1, nb_pad)).astype(I32)
        first = cum[:, 0:1] - nblk[:, 0:1]
        inside = (first <= blk) & (blk < cum[:, 0:1])
        fill = jnp.where(blk == first, cnt[:, 0:1] - (nblk[:, 0:1] - 1.0) * RB, float(RB))
        valid = jnp.sum(jnp.where(inside, fill, 0.0), axis=0, keepdims=True).astype(I32)
        cum_l = jnp.transpose(cum)[0:1].astype(I32)
        nblk_l = jnp.transpose(nblk)[0:1].astype(I32)
        pad = jnp.zeros((1, nb_pad - CLS_PAD), I32)
        r8 = lax.broadcasted_iota(I32, (8, nb_pad), 0)
        cum_row = jnp.concatenate([cum_l, pad], axis=1)
        nblk_row = jnp.concatenate([nblk_l, pad], axis=1)
        meta_ref[...] = jnp.where(r8 == 0, bcls, jnp.where(r8 == 1, nused,
                                  jnp.where(r8 == 2, cum_row, jnp.where(r8 == 3, nblk_row,
                                                                        jnp.where(r8 == 4, valid, 0)))))

    @pl.when(phase == 1)
    def _():
        base = base_ref[...]
        for sb in range(BL // SUB):
            ids = rid_ref[0:1, sb * SUB:(sb + 1) * SUB]
            hit = cls_iota == ids
            incl = _mdot(hit.astype(MXU_DTYPE), u_ref[...])
            slot = jnp.sum(jnp.where(hit, base + incl - 1.0, 0.0), axis=0, keepdims=True)
            dest_ref[0:1, sb * SUB:(sb + 1) * SUB] = slot.astype(I32)
            base = base + incl[:, SUB - 1:SUB]
        base_ref[...] = base


def _rank_call(rid, u_mat, tri, nb_pad):
    n_tok = rid.shape[1]
    nj = n_tok // BL
    return pl.pallas_call(
        functools.partial(_rank_kernel, nb_pad=nb_pad),
        out_shape=(jax.ShapeDtypeStruct((1, n_tok), I32), jax.ShapeDtypeStruct((8, nb_pad), I32)),
        grid=(2, nj),
        in_specs=[pl.BlockSpec((8, BL), lambda p, j: (0, j)),
                  pl.BlockSpec(u_mat.shape, lambda p, j: (0, 0)),
                  pl.BlockSpec(tri.shape, lambda p, j: (0, 0))],
        out_specs=(pl.BlockSpec((1, BL), lambda p, j: (0, p * j)),
                   pl.BlockSpec((8, nb_pad), lambda p, j: (0, 0))),
        scratch_shapes=[pltpu.VMEM((CLS_PAD, 1), F32), pltpu.VMEM((CLS_PAD, 1), F32)],
        compiler_params=pltpu.CompilerParams(dimension_semantics=("arbitrary", "arbitrary"),
                                             vmem_limit_bytes=VMEM_LIMIT),
        name="rank_tokens",
    )(rid, u_mat, tri)


def _dispatch_kernel(cum_ref, nblk_ref, nused_ref, valid_ref, dest_hbm, rows_hbm, wgu_ref, wd_ref, out_hbm, wgu_o_ref,
                     wd_o_ref, idx_smem, rows_buf, zero_ref, sem_idx, sem_in, sem_row, sem_zero, *, gu_steps):
    i = pl.program_id(0)
    n_steps = pl.num_programs(0)

    @pl.when(i < gu_steps)
    def _():
        wgu_o_ref[...] = wgu_ref[...].astype(MXU_DTYPE)

    @pl.when(i >= gu_steps)
    def _():
        wd_o_ref[...] = wd_ref[...].astype(MXU_DTYPE)

    slot = i % 2
    tile_rows = TD * SUBLANES
    part_rows = ZERO_ROWS * SUBLANES
    parts = RB // ZERO_ROWS
    n_blocks = out_hbm.shape[0] // (RB * SUBLANES)

    def fetch(step, s):
        start = pl.multiple_of(step * tile_rows, tile_rows)
        return (pltpu.make_async_copy(dest_hbm.at[step], idx_smem.at[s], sem_idx.at[s]),
                pltpu.make_async_copy(rows_hbm.at[pl.ds(start, tile_rows)], rows_buf.at[s], sem_in.at[s]))

    def scattered(s):
        return pltpu.make_async_copy(rows_buf.at[s], out_hbm.at[pl.ds(0, tile_rows)], sem_row.at[s])

    def zero_part(part):
        start = pl.multiple_of(part * part_rows, part_rows)
        return pltpu.make_async_copy(zero_ref, out_hbm.at[pl.ds(start, part_rows)], sem_zero)

    @pl.when(i == 0)
    def _():
        zero_ref[...] = jnp.zeros_like(zero_ref)

        def class_parts(c, fn):
            first = cum_ref[c] - nblk_ref[c]
            padding = RB - valid_ref[jnp.maximum(first, 0)]
            for q in range(parts):
                @pl.when((nblk_ref[c] > 0) & (q * ZERO_ROWS < padding))
                def _(q=q):
                    fn(zero_part(first * parts + q))

        def start(c, carry):
            class_parts(c, lambda cp: cp.start())
            return carry

        def wait(c, carry):
            class_parts(c, lambda cp: cp.wait())
            return carry

        def start_tail(part, carry):
            zero_part(part).start()
            return carry

        def wait_tail(part, carry):
            zero_part(part).wait()
            return carry

        lax.fori_loop(0, N_CLASSES, start, 0)
        lax.fori_loop(nused_ref[0] * parts, n_blocks * parts, start_tail, 0)
        lax.fori_loop(0, N_CLASSES, wait, 0)
        lax.fori_loop(nused_ref[0] * parts, n_blocks * parts, wait_tail, 0)
        for cp in fetch(0, 0):
            cp.start()

    for cp in fetch(i, slot):
        cp.wait()

    @pl.when(i > 0)
    def _():
        scattered(1 - slot).wait()

    @pl.when(i + 1 < n_steps)
    def _():
        for cp in fetch(i + 1, 1 - slot):
            cp.start()

    def scatter(g, carry):
        for u in range(ROW_UNROLL):
            t = g * ROW_UNROLL + u
            src = pl.multiple_of(t * SUBLANES, SUBLANES)
            dst = pl.multiple_of(idx_smem[slot, t] * SUBLANES, SUBLANES)
            pltpu.make_async_copy(rows_buf.at[slot, pl.ds(src, SUBLANES)], out_hbm.at[pl.ds(dst, SUBLANES)],
                                  sem_row.at[slot]).start(priority=u % 2)
        return carry

    lax.fori_loop(0, TD // ROW_UNROLL, scatter, 0)

    @pl.when(i + 1 == n_steps)
    def _():
        scattered(slot).wait()


def _dispatch_call(cum_cls, nblk_cls, nused, blk_valid, dest2d, rows, n_rows, w_gate_up, w_down):
    n_tok = rows.shape[0] // SUBLANES
    n_steps = n_tok // TD
    wgu2d = w_gate_up.reshape(-1, w_gate_up.shape[-1])
    wd2d = w_down.reshape(-1, w_down.shape[-1])
    assert wgu2d.shape[1] == wd2d.shape[1]
    slab = max(16, pl.next_power_of_2(pl.cdiv(wgu2d.shape[0] + wd2d.shape[0], n_steps)))
    assert wgu2d.shape[0] % slab == 0 and wd2d.shape[0] % slab == 0
    gu_steps, d_steps = wgu2d.shape[0] // slab, wd2d.shape[0] // slab
    assert gu_steps + d_steps <= n_steps

    def gu_map(i, *_):
        return (jnp.minimum(i, gu_steps - 1), 0)

    def d_map(i, *_):
        return (jnp.clip(i - gu_steps, 0, d_steps - 1), 0)

    wspec = lambda m: pl.BlockSpec((slab, wgu2d.shape[1]), m)
    grid_spec = pltpu.PrefetchScalarGridSpec(
        num_scalar_prefetch=4,
        grid=(n_steps,),
        in_specs=[pl.BlockSpec(memory_space=pl.ANY), pl.BlockSpec(memory_space=pl.ANY), wspec(gu_map), wspec(d_map)],
        out_specs=(pl.BlockSpec(memory_space=pl.ANY), wspec(gu_map), wspec(d_map)),
        scratch_shapes=[pltpu.SMEM((2, TD), I32), pltpu.VMEM((2, TD * SUBLANES, LANES), U32),
                        pltpu.VMEM((ZERO_ROWS * SUBLANES, LANES), U32),
                        pltpu.SemaphoreType.DMA((2,)), pltpu.SemaphoreType.DMA((2,)),
                        pltpu.SemaphoreType.DMA((2,)), pltpu.SemaphoreType.DMA],
    )
    sorted_rows, wgu_c, wd_c = pl.pallas_call(
        functools.partial(_dispatch_kernel, gu_steps=gu_steps),
        out_shape=(jax.ShapeDtypeStruct((n_rows * SUBLANES, LANES), U32),
                   jax.ShapeDtypeStruct(wgu2d.shape, MXU_DTYPE), jax.ShapeDtypeStruct(wd2d.shape, MXU_DTYPE)),
        grid_spec=grid_spec,
        compiler_params=pltpu.CompilerParams(dimension_semantics=("arbitrary",),
                                             vmem_limit_bytes=VMEM_LIMIT),
        name="dispatch_rows",
    )(cum_cls, nblk_cls, nused, blk_valid, dest2d, rows, wgu2d, wd2d)
    return sorted_rows, wgu_c.reshape(w_gate_up.shape), wd_c.reshape(w_down.shape)


def _expert_kernel(ea_ref, eb_ref, nused_ref, valid_ref, rows_ref, wgu_a_ref, wd_a_ref, wgu_b_ref, wd_b_ref, y_ref):
    b = pl.program_id(0)
    live = b < nused_ref[0]
    half_rows = RB // 2

    def evaluate(m):
        row0 = (RB - m) * SUBLANES

        def tile_row(k):
            return rows_ref[pl.ds(row0 + k, m, stride=SUBLANES), :]

        words = [tile_row(k) for k in range(PACK_ROWS)]
        lo = [lax.bitcast_convert_type(jnp.left_shift(w, 16), F32) for w in words]
        hi = [lax.bitcast_convert_type(jnp.bitwise_and(w, jnp.uint32(HI_MASK)), F32) for w in words]
        x = jnp.concatenate(lo + hi, axis=1).astype(MXU_DTYPE)
        gate = lax.bitcast_convert_type(tile_row(PACK_ROWS), F32)

        def expert(wgu_ref, wd_ref):
            gu = _mdot(x, wgu_ref[0])
            act = jax.nn.silu(gu[:, :D_EXPERT]) * gu[:, D_EXPERT:]
            return _mdot(act.astype(MXU_DTYPE), wd_ref[0])

        y = expert(wgu_a_ref, wd_a_ref) * gate[:, 0:1] + expert(wgu_b_ref, wd_b_ref) * gate[:, 1:2]
        for k in range(Y_ROWS):
            y_ref[pl.ds(row0 + k, m, stride=SUBLANES), :] = y[:, k * LANES:(k + 1) * LANES]

    @pl.when(live & (valid_ref[b] > half_rows))
    def _():
        evaluate(RB)

    @pl.when(live & (valid_ref[b] <= half_rows))
    def _():
        evaluate(half_rows)
        y_ref[:half_rows * Y_ROWS, :] = jnp.zeros((half_rows * Y_ROWS, LANES), F32)

    @pl.when(jnp.logical_not(live))
    def _():
        y_ref[...] = jnp.zeros_like(y_ref)


def _expert_call(blk_ea, blk_eb, nused, blk_valid, rows, w_gate_up, w_down):
    n_rows = rows.shape[0] // SUBLANES
    nblk = n_rows // RB

    def live(b, nu):
        return jnp.maximum(jnp.minimum(b, nu[0] - 1), 0)

    grid_spec = pltpu.PrefetchScalarGridSpec(
        num_scalar_prefetch=4,
        grid=(nblk,),
        in_specs=[pl.BlockSpec((RB * SUBLANES, LANES), lambda b, ea, eb, nu, nv: (live(b, nu), 0)),
                  pl.BlockSpec((1,) + w_gate_up.shape[1:], lambda b, ea, eb, nu, nv: (ea[live(b, nu)], 0, 0)),
                  pl.BlockSpec((1,) + w_down.shape[1:], lambda b, ea, eb, nu, nv: (ea[live(b, nu)], 0, 0)),
                  pl.BlockSpec((1,) + w_gate_up.shape[1:], lambda b, ea, eb, nu, nv: (eb[live(b, nu)], 0, 0)),
                  pl.BlockSpec((1,) + w_down.shape[1:], lambda b, ea, eb, nu, nv: (eb[live(b, nu)], 0, 0))],
        out_specs=pl.BlockSpec((RB * Y_ROWS, LANES), lambda b, ea, eb, nu, nv: (b, 0)),
    )
    return pl.pallas_call(
        _expert_kernel,
        out_shape=jax.ShapeDtypeStruct((n_rows * Y_ROWS, LANES), F32),
        grid_spec=grid_spec,
        compiler_params=pltpu.CompilerParams(dimension_semantics=("arbitrary",),
                                             vmem_limit_bytes=VMEM_LIMIT),
        name="pair_experts",
    )(blk_ea, blk_eb, nused, blk_valid, rows, w_gate_up, w_down, w_gate_up, w_down)


def _combine_kernel(dest_hbm, y_hbm, x1_ref, mod_ref, g_ref, b_ref, o_ref, idx_smem, ybuf, sem_idx, sem_row,
                    *, tile_off):
    ns = pl.num_programs(1)
    i = pl.program_id(0) * ns + pl.program_id(1)
    n_steps = pl.num_programs(0) * ns
    slot = i % 2

    def idx_copy(step, s):
        return pltpu.make_async_copy(dest_hbm.at[tile_off + step], idx_smem.at[s], sem_idx.at[s])

    def issue_gather(s):
        def gather(g, carry):
            for u in range(ROW_UNROLL):
                t = g * ROW_UNROLL + u
                src = pl.multiple_of(idx_smem[s, t] * Y_ROWS, Y_ROWS)
                dst = pl.multiple_of(t * Y_ROWS, Y_ROWS)
                pltpu.make_async_copy(y_hbm.at[pl.ds(src, Y_ROWS)], ybuf.at[s, pl.ds(dst, Y_ROWS)],
                                      sem_row.at[s]).start(priority=u % 2)
            return carry

        lax.fori_loop(0, TD // ROW_UNROLL, gather, 0)

    @pl.when(i == 0)
    def _():
        idx_copy(0, 0).start()
        idx_copy(0, 0).wait()
        issue_gather(0)

        @pl.when(n_steps > 1)
        def _():
            idx_copy(1, 1).start()

    @pl.when(i + 1 < n_steps)
    def _():
        idx_copy(i + 1, 1 - slot).wait()
        issue_gather(1 - slot)

    @pl.when(i + 2 < n_steps)
    def _():
        idx_copy(i + 2, slot).start()

    pltpu.make_async_copy(y_hbm.at[pl.ds(0, TD * Y_ROWS)], ybuf.at[slot], sem_row.at[slot]).wait()

    y = jnp.concatenate([ybuf[slot, pl.ds(k, TD, stride=Y_ROWS), :] for k in range(Y_ROWS)], axis=1)
    g2 = mod_ref[0][5:6]
    o_ref[0] = _layer_norm(ALPHA * x1_ref[...] + g2 * y, g_ref[...], b_ref[...])


def _combine_call(dest2d, y_rows, x1, mod, mod_off, tile_off, ln2g, ln2b, bsz, seq):
    d = x1.shape[1]
    ns = seq // TD
    return pl.pallas_call(
        functools.partial(_combine_kernel, tile_off=tile_off),
        out_shape=jax.ShapeDtypeStruct((bsz, seq, d), F32),
        grid=(bsz, ns),
        in_specs=[pl.BlockSpec(memory_space=pl.ANY),
                  pl.BlockSpec(memory_space=pl.ANY),
                  pl.BlockSpec((TD, d), lambda b, s: (tile_off + b * ns + s, 0)),
                  pl.BlockSpec((1, N_MOD, d), lambda b, s: (mod_off + b, 0, 0)),
                  pl.BlockSpec((1, d), lambda b, s: (0, 0)),
                  pl.BlockSpec((1, d), lambda b, s: (0, 0))],
        out_specs=pl.BlockSpec((1, TD, d), lambda b, s: (b, s, 0)),
        scratch_shapes=[pltpu.SMEM((2, TD), I32), pltpu.VMEM((2, TD * Y_ROWS, LANES), F32),
                        pltpu.SemaphoreType.DMA((2,)), pltpu.SemaphoreType.DMA((2,))],
        compiler_params=pltpu.CompilerParams(dimension_semantics=("arbitrary", "arbitrary"),
                                             vmem_limit_bytes=VMEM_LIMIT),
        name="combine_ln2",
    )(dest2d, y_rows, x1, mod, ln2g, ln2b)


def _band_matrices():
    rows = np.arange(CHUNK)[:, None]
    cols = np.arange(CHUNK + 2 * POOL_HALO)[None, :] - POOL_HALO
    return np.stack([((cols >= rows - w // 2) & (cols < rows + w // 2)) for w in POOL_WINDOWS]).astype(np.float32)


def _class_tables():
    ea, eb = [], []
    for g in range(N_GROUPS):
        for a in range(EPG):
            for b in range(a + 1, EPG):
                ea.append(g * EPG + a)
                eb.append(g * EPG + b)
    return np.asarray(ea, np.int32), np.asarray(eb, np.int32)


def _split_hi_lo(w):
    hi = w.astype(MXU_DTYPE)
    lo = (w - hi.astype(F32)).astype(MXU_DTYPE)
    return hi, lo


def kernel(x_prompt, x_sample, c_prompt, c_sample, w_ada, b_ada, w_in, v_norm_g, v_norm_b, w_spatial, b_spatial,
           w_pool, pool_scale, w_out, ln1_g, ln1_b, w_route_group, b_route_group, w_route_expert, b_route_expert,
           w_gate_up, w_down, ln2_g, ln2_b):
    assert w_ada.shape[0] == 1, "single-layer kernel"
    bp, sp, d = x_prompt.shape
    bs, ss, _ = x_sample.shape
    assert d == D_MODEL and sp % TS == 0 and ss % TS == 0
    n_prompt, n_sample = bp * sp, bs * ss
    n_tok = n_prompt + n_sample
    assert n_tok % BL == 0 and n_prompt % TD == 0

    c_all = jnp.concatenate([c_prompt, c_sample], axis=0)
    mod = _modulation(c_all, w_ada[0], b_ada).reshape(bp + bs, N_MOD, d)

    wr = jnp.concatenate([w_route_group[0], w_route_expert[0]], axis=1).T
    wr = jnp.pad(wr, ((0, ROUTE_PAD - N_ROUTE), (0, 0)))
    wr_hi, wr_lo = _split_hi_lo(wr)
    rbias = jnp.concatenate([b_route_group[0], b_route_expert[0].reshape(-1),
                             jnp.zeros((ROUTE_PAD - N_ROUTE,), F32)]).reshape(ROUTE_PAD, 1)
    row = lambda a: a.reshape(1, -1)
    weights = (w_in[0].astype(MXU_DTYPE), row(v_norm_g[0]), row(v_norm_b[0]), w_spatial[0].astype(MXU_DTYPE),
               b_spatial[0].T, jnp.asarray(_band_matrices(), MXU_DTYPE), w_pool[0].astype(MXU_DTYPE),
               row(pool_scale[0]), w_out[0].astype(MXU_DTYPE), row(ln1_g[0]), row(ln1_b[0]),
               jnp.concatenate([wr_hi, wr_lo], axis=0), rbias)

    x1, rows, rid = _mixer_call(x_prompt, x_sample, mod, weights)

    nblk = (n_tok + N_CLASSES * (RB - 1)) // RB
    nb_pad = -(-nblk // 128) * 128
    u_mat = jnp.asarray(np.triu(np.ones((SUB, SUB), np.float32)), MXU_DTYPE)
    tri = jnp.asarray(np.tril(np.ones((CLS_PAD, CLS_PAD), np.float32)))
    dest, meta = _rank_call(rid, u_mat, tri, nb_pad)
    dest2d = dest.reshape(n_tok // TD, TD)
    tab_a, tab_b = _class_tables()
    blk_cls = meta[0, :nblk]
    blk_ea = jnp.asarray(tab_a)[blk_cls]
    blk_eb = jnp.asarray(tab_b)[blk_cls]
    nused = meta[1, 0:1]
    cum_cls = meta[2, :CLS_PAD]
    nblk_cls = meta[3, :CLS_PAD]

    blk_valid = meta[4, :nblk]
    sorted_rows, wgu, wd = _dispatch_call(cum_cls, nblk_cls, nused, blk_valid, dest2d, rows, nblk * RB,
                                          w_gate_up[0], w_down[0])
    y_rows = _expert_call(blk_ea, blk_eb, nused, blk_valid, sorted_rows, wgu, wd)

    ln2g, ln2b = row(ln2_g[0]), row(ln2_b[0])
    y_prompt = _combine_call(dest2d, y_rows, x1, mod, 0, 0, ln2g, ln2b, bp, sp)
    y_sample = _combine_call(dest2d, y_rows, x1, mod, bp, n_prompt // TD, ln2g, ln2b, bs, ss)
    return (y_prompt, y_sample)
```

```python
import functools

import numpy as np
import jax
import jax.numpy as jnp
from jax import lax
from jax.experimental import pallas as pl
from jax.experimental.pallas import tpu as pltpu

F32 = jnp.float32
I32 = jnp.int32
MXU_DTYPE = jnp.bfloat16

D_MODEL = 1024
A_WIDTH = 512
B_WIDTH = 512
CHUNK = 128
A_HEADS = 4
HEAD_DIM = A_WIDTH // A_HEADS
POOL_WINDOWS = (2, 4, 8, 16)
POOL_HALO = 8
GROUP_DIM = B_WIDTH // len(POOL_WINDOWS)
N_GROUPS = 4
EPG = 8
N_EXPERTS = N_GROUPS * EPG
D_EXPERT = 512
N_MOD = 6
LN_EPS = 1e-5
ALPHA = 2.0 ** 0.25

PAIRS = EPG * (EPG - 1) // 2
N_CLASSES = N_GROUPS * PAIRS
CLS_PAD = 128
N_ROUTE = N_GROUPS + N_EXPERTS
ROUTE_PAD = 40
U32 = jnp.uint32
LANES = 128
SUBLANES = 8
PACK_ROWS = D_MODEL // (2 * LANES)
Y_ROWS = D_MODEL // LANES
assert PACK_ROWS < SUBLANES and Y_ROWS == SUBLANES
HI_MASK = 0xFFFF0000

TS = 512
TD = 1024
RB = 512
ZERO_ROWS = 128
ROW_UNROLL = 8
BL = 2048
SUB = 256
VMEM_LIMIT = 56 * 1024 * 1024

_NT = (((1,), (1,)), ((), ()))

_GELU_K1 = -2.0 * (2.0 / np.pi) ** 0.5
_GELU_K3 = _GELU_K1 * 0.044715


def _gelu_tanh(x):
    return x / (1.0 + jnp.exp(x * (_GELU_K1 + _GELU_K3 * (x * x))))


def _layer_norm(x, g, b):
    mu = jnp.mean(x, axis=-1, keepdims=True)
    xc = x - mu
    var = jnp.mean(xc * xc, axis=-1, keepdims=True)
    return xc * lax.rsqrt(var + LN_EPS) * g + b


def _mdot(a, b):
    return jnp.dot(a, b, preferred_element_type=F32)


def _mod_kernel(c_ref, w_ref, b_ref, o_ref):
    a = jax.nn.silu(c_ref[...])
    o_ref[...] = jnp.dot(a, w_ref[...], precision=lax.Precision.HIGHEST,
                         preferred_element_type=F32) + b_ref[...]


def _modulation(c_all, w_ada, b_ada):
    nb, d = c_all.shape
    n = w_ada.shape[1]
    bn = 1536
    return pl.pallas_call(
        _mod_kernel,
        out_shape=jax.ShapeDtypeStruct((nb, n), F32),
        grid=(n // bn,),
        in_specs=[pl.BlockSpec((nb, d), lambda j: (0, 0)),
                  pl.BlockSpec((d, bn), lambda j: (0, j)),
                  pl.BlockSpec((1, bn), lambda j: (0, j))],
        out_specs=pl.BlockSpec((nb, bn), lambda j: (0, j)),
        compiler_params=pltpu.CompilerParams(dimension_semantics=("arbitrary",),
                                             vmem_limit_bytes=VMEM_LIMIT),
        name="adaln_mod",
    )(c_all, w_ada, b_ada)


def _mixer_kernel(xp_ref, xpp_ref, xpn_ref, xs_ref, xsp_ref, xsn_ref, mod_ref, modp_ref, inv_ref, win_ref, vng_ref,
                  vnb_ref, ws_ref, bst_ref, band_ref, wpool_ref, pscale_ref, wout_ref, ln1g_ref, ln1b_ref, wr_ref,
                  rb_ref, x1_ref, h2_ref, rid_ref, gate_ref, res_ref, *, prompt_tiles, prompt_seq, sample_seq):
    ts = xp_ref.shape[1]
    i = pl.program_id(0)
    slot = i % 2

    @pl.when(i == 0)
    def _():
        res_ref[1] = jnp.zeros(res_ref.shape[1:], F32)

    is_p = i < prompt_tiles
    ns = jnp.where(is_p, prompt_seq // ts, sample_seq // ts)
    s = jnp.where(is_p, i, i - prompt_tiles) % ns
    md = mod_ref[0]
    sh1, sc1, g1 = md[0:1], md[1:2], md[2:3]

    xt = jnp.where(is_p, xp_ref[0], xs_ref[0])
    h = xt * (1.0 + sc1) + sh1
    hp = jnp.where(s > 0, jnp.where(is_p, xpp_ref[0], xsp_ref[0]) * (1.0 + sc1) + sh1, 0.0)
    hn = jnp.where(s < ns - 1, jnp.where(is_p, xpn_ref[0], xsn_ref[0]) * (1.0 + sc1) + sh1, 0.0)
    hext = jnp.concatenate([h, hp, hn], axis=0).astype(MXU_DTYPE)
    zext = _mdot(hext, win_ref[...])
    z = zext[:ts]

    za = _gelu_tanh(z[:, :2 * A_WIDTH])
    u = za[:, :A_WIDTH]
    v = _layer_norm(za[:, A_WIDTH:], vng_ref[...], vnb_ref[...]).astype(MXU_DTYPE)
    bst = bst_ref[...]
    n_chunks = ts // CHUNK

    def chunks_on_lanes(a, rows, col0, width):
        return jnp.concatenate([a[c * CHUNK:c * CHUNK + rows, col0:col0 + width] for c in range(n_chunks)], axis=1)

    def chunks_on_rows(a, width):
        return jnp.concatenate([a[:, c * width:(c + 1) * width] for c in range(n_chunks)], axis=0)

    heads = [_mdot(ws_ref[hh], chunks_on_lanes(v, CHUNK, hh * HEAD_DIM, HEAD_DIM)) + bst[:, hh:hh + 1]
             for hh in range(A_HEADS)]
    a_out = u * jnp.concatenate([chunks_on_rows(o, HEAD_DIM) for o in heads], axis=1)

    p = z[:, 2 * A_WIDTH:]
    pall = jnp.concatenate([zext[ts:ts + POOL_HALO, 2 * A_WIDTH:], p,
                            zext[ts + POOL_HALO:ts + 2 * POOL_HALO, 2 * A_WIDTH:]], axis=0)
    p_hi = pall.astype(MXU_DTYPE)
    p_lo = (pall - p_hi.astype(F32)).astype(MXU_DTYPE)
    b_cols = []
    for g in range(len(POOL_WINDOWS)):
        seg_rows = CHUNK + 2 * POOL_HALO
        win = (_mdot(band_ref[g], chunks_on_lanes(p_hi, seg_rows, g * GROUP_DIM, GROUP_DIM))
               + _mdot(band_ref[g], chunks_on_lanes(p_lo, seg_rows, g * GROUP_DIM, GROUP_DIM)))
        pooled = chunks_on_rows(win, GROUP_DIM) * inv_ref[0, g] - p[:, g * GROUP_DIM:(g + 1) * GROUP_DIM]
        b_cols.append(_mdot(pooled.astype(MXU_DTYPE), wpool_ref[g]))
    b_out = jnp.concatenate(b_cols, axis=1) * pscale_ref[...]

    mix_in = jnp.concatenate([a_out, b_out], axis=1).astype(MXU_DTYPE)
    mdp = modp_ref[0]
    h2_prev, lt_prev = _ln1_router(res_ref[1 - slot], mdp[3:4], mdp[4:5], ln1g_ref, ln1b_ref, wr_ref, rb_ref,
                                   x1_ref)
    _route(h2_prev, lt_prev, h2_ref, rid_ref, gate_ref)

    mix = _mdot(mix_in, wout_ref[...])
    res_ref[slot] = ALPHA * xt + g1 * mix


def _ln1_router(res, sh2, sc2, ln1g_ref, ln1b_ref, wr_ref, rb_ref, x1_ref):
    x1 = _layer_norm(res, ln1g_ref[...], ln1b_ref[...])
    x1_ref[...] = x1
    h2 = x1 * (1.0 + sc2) + sh2
    h2_hi = h2.astype(MXU_DTYPE)
    h2_lo = (h2 - h2_hi.astype(F32)).astype(MXU_DTYPE)
    l1 = lax.dot_general(wr_ref[...], h2_hi, _NT, preferred_element_type=F32)
    l2 = lax.dot_general(wr_ref[0:ROUTE_PAD], h2_lo, _NT, preferred_element_type=F32)
    return h2, l1[:ROUTE_PAD] + l1[ROUTE_PAD:] + l2 + rb_ref[...]


def _route(h2, lt, h2_ref, rid_ref, gate_ref):
    ts = h2.shape[0]

    def row(r):
        return lt[r:r + 1, :]

    gl = [row(r) for r in range(N_GROUPS)]
    gmax = jnp.maximum(jnp.maximum(gl[0], gl[1]), jnp.maximum(gl[2], gl[3]))
    gidx = jnp.where(gl[0] == gmax, 0, jnp.where(gl[1] == gmax, 1, jnp.where(gl[2] == gmax, 2, 3)))
    gsum = (jnp.exp(gl[0] - gmax) + jnp.exp(gl[1] - gmax)) + (jnp.exp(gl[2] - gmax) + jnp.exp(gl[3] - gmax))
    gw = 1.0 / gsum
    ev = [jnp.where(gidx == 0, row(N_GROUPS + j),
                    jnp.where(gidx == 1, row(N_GROUPS + EPG + j),
                              jnp.where(gidx == 2, row(N_GROUPS + 2 * EPG + j), row(N_GROUPS + 3 * EPG + j))))
          for j in range(EPG)]

    def top1(vals):
        m = vals[0]
        for t in vals[1:]:
            m = jnp.maximum(m, t)
        idx = jnp.full(m.shape, EPG - 1, I32)
        for j in range(EPG - 2, -1, -1):
            idx = jnp.where(vals[j] == m, j, idx)
        return m, idx

    v1, j1 = top1(ev)
    v2, j2 = top1([jnp.where(j1 == j, -jnp.inf, ev[j]) for j in range(EPG)])
    t2 = jnp.exp(v2 - v1)
    den = 1.0 + t2
    w1 = (1.0 / den) * gw
    w2 = (t2 / den) * gw
    first = j1 < j2
    ea = jnp.minimum(j1, j2)
    eb = jnp.maximum(j1, j2)
    wa = jnp.where(first, w1, w2)
    wb = jnp.where(first, w2, w1)
    cls = gidx * PAIRS + jnp.right_shift(ea * (2 * EPG - 1 - ea), 1) + (eb - ea - 1)

    r8 = lax.broadcasted_iota(I32, (8, ts), 0)
    rid_ref[...] = jnp.where(r8 == 0, cls, jnp.where(r8 == 1, gidx * EPG + ea,
                                                      jnp.where(r8 == 2, gidx * EPG + eb, 0)))
    gate_ref[...] = jnp.where(r8 == 0, wa, jnp.where(r8 == 1, wb, 0.0))
    h2_ref[...] = h2.astype(jnp.bfloat16)


def _pack_rows(h2, gates, rows_ref):
    n = h2.shape[0]
    bits = lax.bitcast_convert_type(h2.astype(F32), U32)
    half = D_MODEL // 2
    for k in range(PACK_ROWS):
        lo = jnp.right_shift(bits[:, k * LANES:(k + 1) * LANES], 16)
        hi = jnp.bitwise_and(bits[:, half + k * LANES:half + (k + 1) * LANES], jnp.uint32(HI_MASK))
        rows_ref[pl.ds(k, n, stride=SUBLANES), :] = jnp.bitwise_or(lo, hi)
    g128 = jnp.concatenate([gates, jnp.zeros((LANES - gates.shape[0], n), F32)], axis=0)
    rows_ref[pl.ds(PACK_ROWS, n, stride=SUBLANES), :] = lax.bitcast_convert_type(g128.T, U32)
    for k in range(PACK_ROWS + 1, SUBLANES):
        rows_ref[pl.ds(k, n, stride=SUBLANES), :] = jnp.zeros((n, LANES), U32)


def _inv_population_table():
    r = np.arange(TS)
    out = np.empty((3, len(POOL_WINDOWS), TS, LANES), np.float32)
    for case in range(3):
        for g, w in enumerate(POOL_WINDOWS):
            lo = np.maximum(r - w // 2, 0) if case == 0 else r - w // 2
            hi = np.minimum(r + w // 2, TS) if case == 2 else r + w // 2
            out[case, g] = (1.0 / (hi - lo).astype(np.float64)).astype(np.float32)[:, None]
    return out


def _mixer_call(x_prompt, x_sample, mod, weights):
    bp, sp, d = x_prompt.shape
    bs, ss, _ = x_sample.shape
    nsp, nss = sp // TS, ss // TS
    assert nsp >= 2 and nss >= 2, "a tile is the first or the last of its sequence, not both"
    ntp, nts = bp * nsp, bs * nss
    n_tok = bp * sp + bs * ss
    hb = TS // POOL_HALO

    def p_tile(i):
        t = jnp.minimum(i, ntp - 1)
        return t // nsp, t % nsp

    def s_tile(i):
        t = jnp.clip(i - ntp, 0, nts - 1)
        return t // nss, t % nss

    def specs(tile_fn, seq):
        def cur(i):
            b, s = tile_fn(i)
            return (b, s, 0)

        def prev(i):
            b, s = tile_fn(i)
            return (b, jnp.maximum(s * hb - 1, 0), 0)

        def nxt(i):
            b, s = tile_fn(i)
            return (b, jnp.minimum((s + 1) * hb, seq // POOL_HALO - 1), 0)

        return [pl.BlockSpec((1, TS, d), cur), pl.BlockSpec((1, POOL_HALO, d), prev),
                pl.BlockSpec((1, POOL_HALO, d), nxt)]

    def mod_map(i):
        return (jnp.where(i < ntp, p_tile(i)[0], bp + s_tile(i)[0]), 0, 0)

    def inv_map(i):
        s = jnp.where(i < ntp, p_tile(i)[1], s_tile(i)[1])
        last = jnp.where(i < ntp, nsp - 1, nss - 1)
        return (jnp.where(s == 0, 0, jnp.where(s == last, 2, 1)), 0, 0, 0)

    def const(w):
        return pl.BlockSpec(w.shape, lambda i, nd=w.ndim: (0,) * nd)

    def prev_tile(i):
        return jnp.maximum(i - 1, 0)

    inv_tab = jnp.asarray(_inv_population_table())
    in_specs = (specs(p_tile, sp) + specs(s_tile, ss)
                + [pl.BlockSpec((1, N_MOD, d), mod_map),
                   pl.BlockSpec((1, N_MOD, d), lambda i: mod_map(prev_tile(i))),
                   pl.BlockSpec((1,) + inv_tab.shape[1:], inv_map)]
                + [const(w) for w in weights])
    out_shape = (jax.ShapeDtypeStruct((n_tok, d), F32),
                 jax.ShapeDtypeStruct((n_tok, d), jnp.bfloat16),
                 jax.ShapeDtypeStruct((8, n_tok), I32),
                 jax.ShapeDtypeStruct((8, n_tok), F32))
    out_specs = (pl.BlockSpec((TS, d), lambda i: (prev_tile(i), 0)),
                 pl.BlockSpec((TS, d), lambda i: (prev_tile(i), 0)),
                 pl.BlockSpec((8, TS), lambda i: (0, prev_tile(i))),
                 pl.BlockSpec((8, TS), lambda i: (0, prev_tile(i))))
    return pl.pallas_call(
        functools.partial(_mixer_kernel, prompt_tiles=ntp, prompt_seq=sp, sample_seq=ss),
        out_shape=out_shape,
        grid=(ntp + nts + 1,),
        in_specs=in_specs,
        out_specs=out_specs,
        scratch_shapes=[pltpu.VMEM((2, TS, d), F32)],
        compiler_params=pltpu.CompilerParams(dimension_semantics=("arbitrary",),
                                             vmem_limit_bytes=VMEM_LIMIT),
        name="mixer_ln1_route",
    )(x_prompt, x_prompt, x_prompt, x_sample, x_sample, x_sample, mod, mod, inv_tab, *weights)


def _rank_kernel(rid_ref, u_ref, tri_ref, dest_ref, meta_ref, cnt_ref, base_ref, *, nb_pad):
    phase = pl.program_id(0)
    j = pl.program_id(1)
    cls_iota = lax.broadcasted_iota(I32, (CLS_PAD, SUB), 0)

    @pl.when((phase == 0) & (j == 0))
    def _():
        cnt_ref[...] = jnp.zeros_like(cnt_ref)

    @pl.when(phase == 0)
    def _():
        acc = cnt_ref[...]
        for sb in range(BL // SUB):
            ids = rid_ref[0:1, sb * SUB:(sb + 1) * SUB]
            acc = acc + jnp.sum((cls_iota == ids).astype(F32), axis=1, keepdims=True)
        cnt_ref[...] = acc

    @pl.when((phase == 1) & (j == 0))
    def _():
        cnt = jnp.broadcast_to(cnt_ref[...], (CLS_PAD, CLS_PAD))
        nblk = jnp.floor((cnt + (RB - 1)) * (1.0 / RB))
        cum = jnp.dot(tri_ref[...], nblk, precision=lax.Precision.HIGHEST, preferred_element_type=F32)
        base_ref[...] = cum[:, 0:1] * RB - cnt[:, 0:1]
        blk = lax.broadcasted_iota(I32, (CLS_PAD, nb_pad), 1).astype(F32)
        bcls = jnp.sum((cum[:, 0:1] <= blk).astype(F32), axis=0, keepdims=True)
        bcls = jnp.minimum(bcls, N_CLASSES - 1).astype(I32)
        nused = jnp.broadcast_to(cum[CLS_PAD - 1:CLS_PAD, 0:1], (1, nb_pad)).astype(I32)
        first = cum[:, 0:1] - nblk[:, 0:1]
        inside = (first <= blk) & (blk < cum[:, 0:1])
        fill = jnp.where(blk == first, cnt[:, 0:1] - (nblk[:, 0:1] - 1.0) * RB, float(RB))
        valid = jnp.sum(jnp.where(inside, fill, 0.0), axis=0, keepdims=True).astype(I32)
        cum_l = jnp.transpose(cum)[0:1].astype(I32)
        nblk_l = jnp.transpose(nblk)[0:1].astype(I32)
        pad = jnp.zeros((1, nb_pad - CLS_PAD), I32)
        r8 = lax.broadcasted_iota(I32, (8, nb_pad), 0)
        cum_row = jnp.concatenate([cum_l, pad], axis=1)
        nblk_row = jnp.concatenate([nblk_l, pad], axis=1)
        meta_ref[...] = jnp.where(r8 == 0, bcls, jnp.where(r8 == 1, nused,
                                  jnp.where(r8 == 2, cum_row, jnp.where(r8 == 3, nblk_row,
                                                                        jnp.where(r8 == 4, valid, 0)))))

    @pl.when(phase == 1)
    def _():
        base = base_ref[...]
        for sb in range(BL // SUB):
            ids = rid_ref[0:1, sb * SUB:(sb + 1) * SUB]
            hit = cls_iota == ids
            incl = _mdot(hit.astype(MXU_DTYPE), u_ref[...])
            slot = jnp.sum(jnp.where(hit, base + incl - 1.0, 0.0), axis=0, keepdims=True)
            dest_ref[0:1, sb * SUB:(sb + 1) * SUB] = slot.astype(I32)
            base = base + incl[:, SUB - 1:SUB]
        base_ref[...] = base


def _rank_call(rid, u_mat, tri, nb_pad):
    n_tok = rid.shape[1]
    nj = n_tok // BL
    return pl.pallas_call(
        functools.partial(_rank_kernel, nb_pad=nb_pad),
        out_shape=(jax.ShapeDtypeStruct((1, n_tok), I32), jax.ShapeDtypeStruct((8, nb_pad), I32)),
        grid=(2, nj),
        in_specs=[pl.BlockSpec((8, BL), lambda p, j: (0, j)),
                  pl.BlockSpec(u_mat.shape, lambda p, j: (0, 0)),
                  pl.BlockSpec(tri.shape, lambda p, j: (0, 0))],
        out_specs=(pl.BlockSpec((1, BL), lambda p, j: (0, p * j)),
                   pl.BlockSpec((8, nb_pad), lambda p, j: (0, 0))),
        scratch_shapes=[pltpu.VMEM((CLS_PAD, 1), F32), pltpu.VMEM((CLS_PAD, 1), F32)],
        compiler_params=pltpu.CompilerParams(dimension_semantics=("arbitrary", "arbitrary"),
                                             vmem_limit_bytes=VMEM_LIMIT),
        name="rank_tokens",
    )(rid, u_mat, tri)


def _dispatch_kernel(cum_ref, nblk_ref, nused_ref, valid_ref, dest_hbm, h2_ref, gate_ref, wgu_ref, wd_ref, out_hbm,
                     wgu_o_ref, wd_o_ref, idx_smem, rows_buf, zero_ref, sem_idx, sem_row, sem_zero, *, gu_steps):
    i = pl.program_id(0)
    n_steps = pl.num_programs(0)

    @pl.when(i < gu_steps)
    def _():
        wgu_o_ref[...] = wgu_ref[...].astype(MXU_DTYPE)

    @pl.when(i >= gu_steps)
    def _():
        wd_o_ref[...] = wd_ref[...].astype(MXU_DTYPE)

    slot = i % 2
    tile_rows = TD * SUBLANES
    part_rows = ZERO_ROWS * SUBLANES
    parts = RB // ZERO_ROWS
    n_blocks = out_hbm.shape[0] // (RB * SUBLANES)

    def idx_copy(step, s):
        return pltpu.make_async_copy(dest_hbm.at[step], idx_smem.at[s], sem_idx.at[s])

    def scattered(s):
        return pltpu.make_async_copy(rows_buf.at[s], out_hbm.at[pl.ds(0, tile_rows)], sem_row.at[s])

    def zero_part(part):
        start = pl.multiple_of(part * part_rows, part_rows)
        return pltpu.make_async_copy(zero_ref, out_hbm.at[pl.ds(start, part_rows)], sem_zero)

    @pl.when(i == 0)
    def _():
        idx_copy(0, 0).start()
        zero_ref[...] = jnp.zeros_like(zero_ref)

        def class_parts(c, fn):
            first = cum_ref[c] - nblk_ref[c]
            padding = RB - valid_ref[jnp.maximum(first, 0)]
            for q in range(parts):
                @pl.when((nblk_ref[c] > 0) & (q * ZERO_ROWS < padding))
                def _(q=q):
                    fn(zero_part(first * parts + q))

        def start(c, carry):
            class_parts(c, lambda cp: cp.start())
            return carry

        def wait(c, carry):
            class_parts(c, lambda cp: cp.wait())
            return carry

        def start_tail(part, carry):
            zero_part(part).start()
            return carry

        def wait_tail(part, carry):
            zero_part(part).wait()
            return carry

        lax.fori_loop(0, N_CLASSES, start, 0)
        lax.fori_loop(nused_ref[0] * parts, n_blocks * parts, start_tail, 0)
        lax.fori_loop(0, N_CLASSES, wait, 0)
        lax.fori_loop(nused_ref[0] * parts, n_blocks * parts, wait_tail, 0)

    @pl.when(i >= 2)
    def _():
        scattered(slot).wait()

    _pack_rows(h2_ref[...], gate_ref[...], rows_buf.at[slot])

    idx_copy(i, slot).wait()

    @pl.when(i + 1 < n_steps)
    def _():
        idx_copy(i + 1, 1 - slot).start()

    def scatter(g, carry):
        for u in range(ROW_UNROLL):
            t = g * ROW_UNROLL + u
            src = pl.multiple_of(t * SUBLANES, SUBLANES)
            dst = pl.multiple_of(idx_smem[slot, t] * SUBLANES, SUBLANES)
            pltpu.make_async_copy(rows_buf.at[slot, pl.ds(src, SUBLANES)], out_hbm.at[pl.ds(dst, SUBLANES)],
                                  sem_row.at[slot]).start(priority=u % 2)
        return carry

    lax.fori_loop(0, TD // ROW_UNROLL, scatter, 0)

    @pl.when(i + 1 == n_steps)
    def _():
        @pl.when(n_steps > 1)
        def _():
            scattered(1 - slot).wait()

        scattered(slot).wait()


def _dispatch_call(cum_cls, nblk_cls, nused, blk_valid, dest2d, h2, gates, n_rows, w_gate_up, w_down):
    n_tok, d = h2.shape
    n_steps = n_tok // TD
    wgu2d = w_gate_up.reshape(-1, w_gate_up.shape[-1])
    wd2d = w_down.reshape(-1, w_down.shape[-1])
    assert wgu2d.shape[1] == wd2d.shape[1]
    slab = max(16, pl.next_power_of_2(pl.cdiv(wgu2d.shape[0] + wd2d.shape[0], n_steps)))
    assert wgu2d.shape[0] % slab == 0 and wd2d.shape[0] % slab == 0
    gu_steps, d_steps = wgu2d.shape[0] // slab, wd2d.shape[0] // slab
    assert gu_steps + d_steps <= n_steps

    def gu_map(i, *_):
        return (jnp.minimum(i, gu_steps - 1), 0)

    def d_map(i, *_):
        return (jnp.clip(i - gu_steps, 0, d_steps - 1), 0)

    wspec = lambda m: pl.BlockSpec((slab, wgu2d.shape[1]), m)
    grid_spec = pltpu.PrefetchScalarGridSpec(
        num_scalar_prefetch=4,
        grid=(n_steps,),
        in_specs=[pl.BlockSpec(memory_space=pl.ANY), pl.BlockSpec((TD, d), lambda i, *_: (i, 0)),
                  pl.BlockSpec((gates.shape[0], TD), lambda i, *_: (0, i)), wspec(gu_map), wspec(d_map)],
        out_specs=(pl.BlockSpec(memory_space=pl.ANY), wspec(gu_map), wspec(d_map)),
        scratch_shapes=[pltpu.SMEM((2, TD), I32), pltpu.VMEM((2, TD * SUBLANES, LANES), U32),
                        pltpu.VMEM((ZERO_ROWS * SUBLANES, LANES), U32),
                        pltpu.SemaphoreType.DMA((2,)), pltpu.SemaphoreType.DMA((2,)), pltpu.SemaphoreType.DMA],
    )
    sorted_rows, wgu_c, wd_c = pl.pallas_call(
        functools.partial(_dispatch_kernel, gu_steps=gu_steps),
        out_shape=(jax.ShapeDtypeStruct((n_rows * SUBLANES, LANES), U32),
                   jax.ShapeDtypeStruct(wgu2d.shape, MXU_DTYPE), jax.ShapeDtypeStruct(wd2d.shape, MXU_DTYPE)),
        grid_spec=grid_spec,
        compiler_params=pltpu.CompilerParams(dimension_semantics=("arbitrary",),
                                             vmem_limit_bytes=VMEM_LIMIT),
        name="dispatch_rows",
    )(cum_cls, nblk_cls, nused, blk_valid, dest2d, h2, gates, wgu2d, wd2d)
    return sorted_rows, wgu_c.reshape(w_gate_up.shape), wd_c.reshape(w_down.shape)


def _expert_kernel(ea_ref, eb_ref, nused_ref, valid_ref, rows_ref, wgu_a_ref, wd_a_ref, wgu_b_ref, wd_b_ref, y_ref):
    b = pl.program_id(0)
    live = b < nused_ref[0]
    half_rows = RB // 2

    def evaluate(m):
        row0 = (RB - m) * SUBLANES

        def tile_row(k):
            return rows_ref[pl.ds(row0 + k, m, stride=SUBLANES), :]

        words = [tile_row(k) for k in range(PACK_ROWS)]
        lo = [lax.bitcast_convert_type(jnp.left_shift(w, 16), F32) for w in words]
        hi = [lax.bitcast_convert_type(jnp.bitwise_and(w, jnp.uint32(HI_MASK)), F32) for w in words]
        x = jnp.concatenate(lo + hi, axis=1).astype(MXU_DTYPE)
        gate = lax.bitcast_convert_type(tile_row(PACK_ROWS), F32)

        def expert(wgu_ref, wd_ref):
            gu = _mdot(x, wgu_ref[0])
            act = jax.nn.silu(gu[:, :D_EXPERT]) * gu[:, D_EXPERT:]
            return _mdot(act.astype(MXU_DTYPE), wd_ref[0])

        y = expert(wgu_a_ref, wd_a_ref) * gate[:, 0:1] + expert(wgu_b_ref, wd_b_ref) * gate[:, 1:2]
        for k in range(Y_ROWS):
            y_ref[pl.ds(row0 + k, m, stride=SUBLANES), :] = y[:, k * LANES:(k + 1) * LANES]

    @pl.when(live & (valid_ref[b] > half_rows))
    def _():
        evaluate(RB)

    @pl.when(live & (valid_ref[b] <= half_rows))
    def _():
        evaluate(half_rows)
        y_ref[:half_rows * Y_ROWS, :] = jnp.zeros((half_rows * Y_ROWS, LANES), F32)

    @pl.when(jnp.logical_not(live))
    def _():
        y_ref[...] = jnp.zeros_like(y_ref)


def _expert_call(blk_ea, blk_eb, nused, blk_valid, rows, w_gate_up, w_down):
    n_rows = rows.shape[0] // SUBLANES
    nblk = n_rows // RB

    def live(b, nu):
        return jnp.maximum(jnp.minimum(b, nu[0] - 1), 0)

    grid_spec = pltpu.PrefetchScalarGridSpec(
        num_scalar_prefetch=4,
        grid=(nblk,),
        in_specs=[pl.BlockSpec((RB * SUBLANES, LANES), lambda b, ea, eb, nu, nv: (live(b, nu), 0)),
                  pl.BlockSpec((1,) + w_gate_up.shape[1:], lambda b, ea, eb, nu, nv: (ea[live(b, nu)], 0, 0)),
                  pl.BlockSpec((1,) + w_down.shape[1:], lambda b, ea, eb, nu, nv: (ea[live(b, nu)], 0, 0)),
                  pl.BlockSpec((1,) + w_gate_up.shape[1:], lambda b, ea, eb, nu, nv: (eb[live(b, nu)], 0, 0)),
                  pl.BlockSpec((1,) + w_down.shape[1:], lambda b, ea, eb, nu, nv: (eb[live(b, nu)], 0, 0))],
        out_specs=pl.BlockSpec((RB * Y_ROWS, LANES), lambda b, ea, eb, nu, nv: (b, 0)),
    )
    return pl.pallas_call(
        _expert_kernel,
        out_shape=jax.ShapeDtypeStruct((n_rows * Y_ROWS, LANES), F32),
        grid_spec=grid_spec,
        compiler_params=pltpu.CompilerParams(dimension_semantics=("arbitrary",),
                                             vmem_limit_bytes=VMEM_LIMIT),
        name="pair_experts",
    )(blk_ea, blk_eb, nused, blk_valid, rows, w_gate_up, w_down, w_gate_up, w_down)


def _combine_kernel(dest_hbm, y_hbm, x1_ref, mod_ref, g_ref, b_ref, o_ref, idx_smem, ybuf, sem_idx, sem_row,
                    *, tile_off):
    ns = pl.num_programs(1)
    i = pl.program_id(0) * ns + pl.program_id(1)
    n_steps = pl.num_programs(0) * ns
    slot = i % 2

    def idx_copy(step, s):
        return pltpu.make_async_copy(dest_hbm.at[tile_off + step], idx_smem.at[s], sem_idx.at[s])

    def issue_gather(s):
        def gather(g, carry):
            for u in range(ROW_UNROLL):
                t = g * ROW_UNROLL + u
                src = pl.multiple_of(idx_smem[s, t] * Y_ROWS, Y_ROWS)
                dst = pl.multiple_of(t * Y_ROWS, Y_ROWS)
                pltpu.make_async_copy(y_hbm.at[pl.ds(src, Y_ROWS)], ybuf.at[s, pl.ds(dst, Y_ROWS)],
                                      sem_row.at[s]).start(priority=u % 2)
            return carry

        lax.fori_loop(0, TD // ROW_UNROLL, gather, 0)

    @pl.when(i == 0)
    def _():
        idx_copy(0, 0).start()
        idx_copy(0, 0).wait()
        issue_gather(0)

        @pl.when(n_steps > 1)
        def _():
            idx_copy(1, 1).start()

    @pl.when(i + 1 < n_steps)
    def _():
        idx_copy(i + 1, 1 - slot).wait()
        issue_gather(1 - slot)

    @pl.when(i + 2 < n_steps)
    def _():
        idx_copy(i + 2, slot).start()

    pltpu.make_async_copy(y_hbm.at[pl.ds(0, TD * Y_ROWS)], ybuf.at[slot], sem_row.at[slot]).wait()

    y = jnp.concatenate([ybuf[slot, pl.ds(k, TD, stride=Y_ROWS), :] for k in range(Y_ROWS)], axis=1)
    g2 = mod_ref[0][5:6]
    o_ref[0] = _layer_norm(ALPHA * x1_ref[...] + g2 * y, g_ref[...], b_ref[...])


def _combine_call(dest2d, y_rows, x1, mod, mod_off, tile_off, ln2g, ln2b, bsz, seq):
    d = x1.shape[1]
    ns = seq // TD
    return pl.pallas_call(
        functools.partial(_combine_kernel, tile_off=tile_off),
        out_shape=jax.ShapeDtypeStruct((bsz, seq, d), F32),
        grid=(bsz, ns),
        in_specs=[pl.BlockSpec(memory_space=pl.ANY),
                  pl.BlockSpec(memory_space=pl.ANY),
                  pl.BlockSpec((TD, d), lambda b, s: (tile_off + b * ns + s, 0)),
                  pl.BlockSpec((1, N_MOD, d), lambda b, s: (mod_off + b, 0, 0)),
                  pl.BlockSpec((1, d), lambda b, s: (0, 0)),
                  pl.BlockSpec((1, d), lambda b, s: (0, 0))],
        out_specs=pl.BlockSpec((1, TD, d), lambda b, s: (b, s, 0)),
        scratch_shapes=[pltpu.SMEM((2, TD), I32), pltpu.VMEM((2, TD * Y_ROWS, LANES), F32),
                        pltpu.SemaphoreType.DMA((2,)), pltpu.SemaphoreType.DMA((2,))],
        compiler_params=pltpu.CompilerParams(dimension_semantics=("arbitrary", "arbitrary"),
                                             vmem_limit_bytes=VMEM_LIMIT),
        name="combine_ln2",
    )(dest2d, y_rows, x1, mod, ln2g, ln2b)


def _band_matrices():
    rows = np.arange(CHUNK)[:, None]
    cols = np.arange(CHUNK + 2 * POOL_HALO)[None, :] - POOL_HALO
    return np.stack([((cols >= rows - w // 2) & (cols < rows + w // 2)) for w in POOL_WINDOWS]).astype(np.float32)


def _class_tables():
    ea, eb = [], []
    for g in range(N_GROUPS):
        for a in range(EPG):
            for b in range(a + 1, EPG):
                ea.append(g * EPG + a)
                eb.append(g * EPG + b)
    return np.asarray(ea, np.int32), np.asarray(eb, np.int32)


def _split_hi_lo(w):
    hi = w.astype(MXU_DTYPE)
    lo = (w - hi.astype(F32)).astype(MXU_DTYPE)
    return hi, lo


def kernel(x_prompt, x_sample, c_prompt, c_sample, w_ada, b_ada, w_in, v_norm_g, v_norm_b, w_spatial, b_spatial,
           w_pool, pool_scale, w_out, ln1_g, ln1_b, w_route_group, b_route_group, w_route_expert, b_route_expert,
           w_gate_up, w_down, ln2_g, ln2_b):
    assert w_ada.shape[0] == 1, "single-layer kernel"
    bp, sp, d = x_prompt.shape
    bs, ss, _ = x_sample.shape
    assert d == D_MODEL and sp % TS == 0 and ss % TS == 0
    n_prompt, n_sample = bp * sp, bs * ss
    n_tok = n_prompt + n_sample
    assert n_tok % BL == 0 and n_prompt % TD == 0

    c_all = jnp.concatenate([c_prompt, c_sample], axis=0)
    mod = _modulation(c_all, w_ada[0], b_ada).reshape(bp + bs, N_MOD, d)

    wr = jnp.concatenate([w_route_group[0], w_route_expert[0]], axis=1).T
    wr = jnp.pad(wr, ((0, ROUTE_PAD - N_ROUTE), (0, 0)))
    wr_hi, wr_lo = _split_hi_lo(wr)
    rbias = jnp.concatenate([b_route_group[0], b_route_expert[0].reshape(-1),
                             jnp.zeros((ROUTE_PAD - N_ROUTE,), F32)]).reshape(ROUTE_PAD, 1)
    row = lambda a: a.reshape(1, -1)
    weights = (w_in[0].astype(MXU_DTYPE), row(v_norm_g[0]), row(v_norm_b[0]), w_spatial[0].astype(MXU_DTYPE),
               b_spatial[0].T, jnp.asarray(_band_matrices(), MXU_DTYPE), w_pool[0].astype(MXU_DTYPE),
               row(pool_scale[0]), w_out[0].astype(MXU_DTYPE), row(ln1_g[0]), row(ln1_b[0]),
               jnp.concatenate([wr_hi, wr_lo], axis=0), rbias)

    x1, h2, rid, gates = _mixer_call(x_prompt, x_sample, mod, weights)

    nblk = (n_tok + N_CLASSES * (RB - 1)) // RB
    nb_pad = -(-nblk // 128) * 128
    u_mat = jnp.asarray(np.triu(np.ones((SUB, SUB), np.float32)), MXU_DTYPE)
    tri = jnp.asarray(np.tril(np.ones((CLS_PAD, CLS_PAD), np.float32)))
    dest, meta = _rank_call(rid, u_mat, tri, nb_pad)
    dest2d = dest.reshape(n_tok // TD, TD)
    tab_a, tab_b = _class_tables()
    blk_cls = meta[0, :nblk]
    blk_ea = jnp.asarray(tab_a)[blk_cls]
    blk_eb = jnp.asarray(tab_b)[blk_cls]
    nused = meta[1, 0:1]
    cum_cls = meta[2, :CLS_PAD]
    nblk_cls = meta[3, :CLS_PAD]

    blk_valid = meta[4, :nblk]
    sorted_rows, wgu, wd = _dispatch_call(cum_cls, nblk_cls, nused, blk_valid, dest2d, h2, gates, nblk * RB,
                                          w_gate_up[0], w_down[0])
    y_rows = _expert_call(blk_ea, blk_eb, nused, blk_valid, sorted_rows, wgu, wd)

    ln2g, ln2b = row(ln2_g[0]), row(ln2_b[0])
    y_prompt = _combine_call(dest2d, y_rows, x1, mod, 0, 0, ln2g, ln2b, bp, sp)
    y_sample = _combine_call(dest2d, y_rows, x1, mod, bp, n_prompt // TD, ln2g, ln2b, bs, ss)
    return (y_prompt, y_sample)
```

```python
import functools

import numpy as np
import jax
import jax.numpy as jnp
from jax import lax
from jax.experimental import pallas as pl
from jax.experimental.pallas import tpu as pltpu

F32 = jnp.float32
I32 = jnp.int32
MXU_DTYPE = jnp.bfloat16

D_MODEL = 1024
A_WIDTH = 512
B_WIDTH = 512
CHUNK = 128
A_HEADS = 4
HEAD_DIM = A_WIDTH // A_HEADS
POOL_WINDOWS = (2, 4, 8, 16)
POOL_HALO = 8
GROUP_DIM = B_WIDTH // len(POOL_WINDOWS)
N_GROUPS = 4
EPG = 8
N_EXPERTS = N_GROUPS * EPG
D_EXPERT = 512
N_MOD = 6
LN_EPS = 1e-5
ALPHA = 2.0 ** 0.25

PAIRS = EPG * (EPG - 1) // 2
N_CLASSES = N_GROUPS * PAIRS
CLS_PAD = 128
N_ROUTE = N_GROUPS + N_EXPERTS
ROUTE_PAD = 40
U32 = jnp.uint32
LANES = 128
SUBLANES = 8
PACK_ROWS = D_MODEL // (2 * LANES)
Y_ROWS = D_MODEL // LANES
assert PACK_ROWS < SUBLANES and Y_ROWS == SUBLANES
HI_MASK = 0xFFFF0000

TS = 512
TD = 1024
RB = 512
PACK_TILE = 128
ZERO_ROWS = 128
ROW_UNROLL = 8
BL = 2048
SUB = 256
VMEM_LIMIT = 56 * 1024 * 1024

_NT = (((1,), (1,)), ((), ()))

_GELU_K1 = -2.0 * (2.0 / np.pi) ** 0.5
_GELU_K3 = _GELU_K1 * 0.044715


def _gelu_tanh(x):
    return x / (1.0 + jnp.exp(x * (_GELU_K1 + _GELU_K3 * (x * x))))


def _layer_norm(x, g, b):
    mu = jnp.mean(x, axis=-1, keepdims=True)
    xc = x - mu
    var = jnp.mean(xc * xc, axis=-1, keepdims=True)
    return xc * lax.rsqrt(var + LN_EPS) * g + b


def _mdot(a, b):
    return jnp.dot(a, b, preferred_element_type=F32)


def _mod_kernel(c_ref, w_ref, b_ref, o_ref):
    a = jax.nn.silu(c_ref[...])
    o_ref[...] = jnp.dot(a, w_ref[...], precision=lax.Precision.HIGHEST,
                         preferred_element_type=F32) + b_ref[...]


def _modulation(c_all, w_ada, b_ada):
    nb, d = c_all.shape
    n = w_ada.shape[1]
    bn = 1536
    return pl.pallas_call(
        _mod_kernel,
        out_shape=jax.ShapeDtypeStruct((nb, n), F32),
        grid=(n // bn,),
        in_specs=[pl.BlockSpec((nb, d), lambda j: (0, 0)),
                  pl.BlockSpec((d, bn), lambda j: (0, j)),
                  pl.BlockSpec((1, bn), lambda j: (0, j))],
        out_specs=pl.BlockSpec((nb, bn), lambda j: (0, j)),
        compiler_params=pltpu.CompilerParams(dimension_semantics=("arbitrary",),
                                             vmem_limit_bytes=VMEM_LIMIT),
        name="adaln_mod",
    )(c_all, w_ada, b_ada)


def _mixer_kernel(xp_ref, xpp_ref, xpn_ref, xs_ref, xsp_ref, xsn_ref, mod_ref, modp_ref, inv_ref, win_ref, vng_ref,
                  vnb_ref, ws_ref, bst_ref, band_ref, wpool_ref, pscale_ref, wout_ref, ln1g_ref, ln1b_ref, wr_ref,
                  rb_ref, x1_ref, h2_ref, rid_ref, gate_ref, res_ref, *, prompt_tiles, prompt_seq, sample_seq):
    ts = xp_ref.shape[1]
    i = pl.program_id(0)
    slot = i % 2

    @pl.when(i == 0)
    def _():
        res_ref[1] = jnp.zeros(res_ref.shape[1:], F32)

    is_p = i < prompt_tiles
    ns = jnp.where(is_p, prompt_seq // ts, sample_seq // ts)
    s = jnp.where(is_p, i, i - prompt_tiles) % ns
    md = mod_ref[0]
    sh1, sc1, g1 = md[0:1], md[1:2], md[2:3]

    xt = jnp.where(is_p, xp_ref[0], xs_ref[0])
    h = xt * (1.0 + sc1) + sh1
    hp = jnp.where(s > 0, jnp.where(is_p, xpp_ref[0], xsp_ref[0]) * (1.0 + sc1) + sh1, 0.0)
    hn = jnp.where(s < ns - 1, jnp.where(is_p, xpn_ref[0], xsn_ref[0]) * (1.0 + sc1) + sh1, 0.0)
    hext = jnp.concatenate([h, hp, hn], axis=0).astype(MXU_DTYPE)
    zext = _mdot(hext, win_ref[...])
    z = zext[:ts]

    za = _gelu_tanh(z[:, :2 * A_WIDTH])
    u = za[:, :A_WIDTH]
    v = _layer_norm(za[:, A_WIDTH:], vng_ref[...], vnb_ref[...]).astype(MXU_DTYPE)
    bst = bst_ref[...]
    n_chunks = ts // CHUNK

    def chunks_on_lanes(a, rows, col0, width):
        return jnp.concatenate([a[c * CHUNK:c * CHUNK + rows, col0:col0 + width] for c in range(n_chunks)], axis=1)

    def chunks_on_rows(a, width):
        return jnp.concatenate([a[:, c * width:(c + 1) * width] for c in range(n_chunks)], axis=0)

    heads = [_mdot(ws_ref[hh], chunks_on_lanes(v, CHUNK, hh * HEAD_DIM, HEAD_DIM)) + bst[:, hh:hh + 1]
             for hh in range(A_HEADS)]
    a_out = u * jnp.concatenate([chunks_on_rows(o, HEAD_DIM) for o in heads], axis=1)

    p = z[:, 2 * A_WIDTH:]
    pall = jnp.concatenate([zext[ts:ts + POOL_HALO, 2 * A_WIDTH:], p,
                            zext[ts + POOL_HALO:ts + 2 * POOL_HALO, 2 * A_WIDTH:]], axis=0)
    p_hi = pall.astype(MXU_DTYPE)
    p_lo = (pall - p_hi.astype(F32)).astype(MXU_DTYPE)
    b_cols = []
    for g in range(len(POOL_WINDOWS)):
        seg_rows = CHUNK + 2 * POOL_HALO
        win = (_mdot(band_ref[g], chunks_on_lanes(p_hi, seg_rows, g * GROUP_DIM, GROUP_DIM))
               + _mdot(band_ref[g], chunks_on_lanes(p_lo, seg_rows, g * GROUP_DIM, GROUP_DIM)))
        pooled = chunks_on_rows(win, GROUP_DIM) * inv_ref[0, g] - p[:, g * GROUP_DIM:(g + 1) * GROUP_DIM]
        b_cols.append(_mdot(pooled.astype(MXU_DTYPE), wpool_ref[g]))
    b_out = jnp.concatenate(b_cols, axis=1) * pscale_ref[...]

    mix_in = jnp.concatenate([a_out, b_out], axis=1).astype(MXU_DTYPE)
    mdp = modp_ref[0]
    h2_prev, lt_prev = _ln1_router(res_ref[1 - slot], mdp[3:4], mdp[4:5], ln1g_ref, ln1b_ref, wr_ref, rb_ref,
                                   x1_ref)
    _route(h2_prev, lt_prev, h2_ref, rid_ref, gate_ref)

    mix = _mdot(mix_in, wout_ref[...])
    res_ref[slot] = ALPHA * xt + g1 * mix


def _ln1_router(res, sh2, sc2, ln1g_ref, ln1b_ref, wr_ref, rb_ref, x1_ref):
    x1 = _layer_norm(res, ln1g_ref[...], ln1b_ref[...])
    x1_ref[...] = x1
    h2 = x1 * (1.0 + sc2) + sh2
    h2_hi = h2.astype(MXU_DTYPE)
    h2_lo = (h2 - h2_hi.astype(F32)).astype(MXU_DTYPE)
    l1 = lax.dot_general(wr_ref[...], h2_hi, _NT, preferred_element_type=F32)
    l2 = lax.dot_general(wr_ref[0:ROUTE_PAD], h2_lo, _NT, preferred_element_type=F32)
    return h2, l1[:ROUTE_PAD] + l1[ROUTE_PAD:] + l2 + rb_ref[...]


def _route(h2, lt, h2_ref, rid_ref, gate_ref):
    ts = h2.shape[0]

    def row(r):
        return lt[r:r + 1, :]

    gl = [row(r) for r in range(N_GROUPS)]
    gmax = jnp.maximum(jnp.maximum(gl[0], gl[1]), jnp.maximum(gl[2], gl[3]))
    gidx = jnp.where(gl[0] == gmax, 0, jnp.where(gl[1] == gmax, 1, jnp.where(gl[2] == gmax, 2, 3)))
    gsum = (jnp.exp(gl[0] - gmax) + jnp.exp(gl[1] - gmax)) + (jnp.exp(gl[2] - gmax) + jnp.exp(gl[3] - gmax))
    gw = 1.0 / gsum
    ev = [jnp.where(gidx == 0, row(N_GROUPS + j),
                    jnp.where(gidx == 1, row(N_GROUPS + EPG + j),
                              jnp.where(gidx == 2, row(N_GROUPS + 2 * EPG + j), row(N_GROUPS + 3 * EPG + j))))
          for j in range(EPG)]

    def top1(vals):
        m = vals[0]
        for t in vals[1:]:
            m = jnp.maximum(m, t)
        idx = jnp.full(m.shape, EPG - 1, I32)
        for j in range(EPG - 2, -1, -1):
            idx = jnp.where(vals[j] == m, j, idx)
        return m, idx

    v1, j1 = top1(ev)
    v2, j2 = top1([jnp.where(j1 == j, -jnp.inf, ev[j]) for j in range(EPG)])
    t2 = jnp.exp(v2 - v1)
    den = 1.0 + t2
    w1 = (1.0 / den) * gw
    w2 = (t2 / den) * gw
    first = j1 < j2
    ea = jnp.minimum(j1, j2)
    eb = jnp.maximum(j1, j2)
    wa = jnp.where(first, w1, w2)
    wb = jnp.where(first, w2, w1)
    cls = gidx * PAIRS + jnp.right_shift(ea * (2 * EPG - 1 - ea), 1) + (eb - ea - 1)

    r8 = lax.broadcasted_iota(I32, (8, ts), 0)
    rid_ref[...] = jnp.where(r8 == 0, cls, jnp.where(r8 == 1, gidx * EPG + ea,
                                                      jnp.where(r8 == 2, gidx * EPG + eb, 0)))
    gate_ref[...] = jnp.where(r8 == 0, wa, jnp.where(r8 == 1, wb, 0.0))
    h2_ref[...] = h2.astype(jnp.bfloat16)


def _pack_rows(h2, gates, rows_ref):
    n = h2.shape[0]
    bits = lax.bitcast_convert_type(h2.astype(F32), U32)
    half = D_MODEL // 2
    for k in range(PACK_ROWS):
        lo = jnp.right_shift(bits[:, k * LANES:(k + 1) * LANES], 16)
        hi = jnp.bitwise_and(bits[:, half + k * LANES:half + (k + 1) * LANES], jnp.uint32(HI_MASK))
        rows_ref[pl.ds(k, n, stride=SUBLANES), :] = jnp.bitwise_or(lo, hi)
    g128 = jnp.concatenate([gates, jnp.zeros((LANES - gates.shape[0], n), F32)], axis=0)
    rows_ref[pl.ds(PACK_ROWS, n, stride=SUBLANES), :] = lax.bitcast_convert_type(g128.T, U32)
    for k in range(PACK_ROWS + 1, SUBLANES):
        rows_ref[pl.ds(k, n, stride=SUBLANES), :] = jnp.zeros((n, LANES), U32)


def _inv_population_table():
    r = np.arange(TS)
    out = np.empty((3, len(POOL_WINDOWS), TS, LANES), np.float32)
    for case in range(3):
        for g, w in enumerate(POOL_WINDOWS):
            lo = np.maximum(r - w // 2, 0) if case == 0 else r - w // 2
            hi = np.minimum(r + w // 2, TS) if case == 2 else r + w // 2
            out[case, g] = (1.0 / (hi - lo).astype(np.float64)).astype(np.float32)[:, None]
    return out


def _mixer_call(x_prompt, x_sample, mod, weights):
    bp, sp, d = x_prompt.shape
    bs, ss, _ = x_sample.shape
    nsp, nss = sp // TS, ss // TS
    assert nsp >= 2 and nss >= 2, "a tile is the first or the last of its sequence, not both"
    ntp, nts = bp * nsp, bs * nss
    n_tok = bp * sp + bs * ss
    hb = TS // POOL_HALO

    def p_tile(i):
        t = jnp.minimum(i, ntp - 1)
        return t // nsp, t % nsp

    def s_tile(i):
        t = jnp.clip(i - ntp, 0, nts - 1)
        return t // nss, t % nss

    def specs(tile_fn, seq):
        def cur(i):
            b, s = tile_fn(i)
            return (b, s, 0)

        def prev(i):
            b, s = tile_fn(i)
            return (b, jnp.maximum(s * hb - 1, 0), 0)

        def nxt(i):
            b, s = tile_fn(i)
            return (b, jnp.minimum((s + 1) * hb, seq // POOL_HALO - 1), 0)

        return [pl.BlockSpec((1, TS, d), cur), pl.BlockSpec((1, POOL_HALO, d), prev),
                pl.BlockSpec((1, POOL_HALO, d), nxt)]

    def mod_map(i):
        return (jnp.where(i < ntp, p_tile(i)[0], bp + s_tile(i)[0]), 0, 0)

    def inv_map(i):
        s = jnp.where(i < ntp, p_tile(i)[1], s_tile(i)[1])
        last = jnp.where(i < ntp, nsp - 1, nss - 1)
        return (jnp.where(s == 0, 0, jnp.where(s == last, 2, 1)), 0, 0, 0)

    def const(w):
        return pl.BlockSpec(w.shape, lambda i, nd=w.ndim: (0,) * nd)

    def prev_tile(i):
        return jnp.maximum(i - 1, 0)

    inv_tab = jnp.asarray(_inv_population_table())
    in_specs = (specs(p_tile, sp) + specs(s_tile, ss)
                + [pl.BlockSpec((1, N_MOD, d), mod_map),
                   pl.BlockSpec((1, N_MOD, d), lambda i: mod_map(prev_tile(i))),
                   pl.BlockSpec((1,) + inv_tab.shape[1:], inv_map)]
                + [const(w) for w in weights])
    out_shape = (jax.ShapeDtypeStruct((n_tok, d), F32),
                 jax.ShapeDtypeStruct((n_tok, d), jnp.bfloat16),
                 jax.ShapeDtypeStruct((8, n_tok), I32),
                 jax.ShapeDtypeStruct((8, n_tok), F32))
    out_specs = (pl.BlockSpec((TS, d), lambda i: (prev_tile(i), 0)),
                 pl.BlockSpec((TS, d), lambda i: (prev_tile(i), 0)),
                 pl.BlockSpec((8, TS), lambda i: (0, prev_tile(i))),
                 pl.BlockSpec((8, TS), lambda i: (0, prev_tile(i))))
    return pl.pallas_call(
        functools.partial(_mixer_kernel, prompt_tiles=ntp, prompt_seq=sp, sample_seq=ss),
        out_shape=out_shape,
        grid=(ntp + nts + 1,),
        in_specs=in_specs,
        out_specs=out_specs,
        scratch_shapes=[pltpu.VMEM((2, TS, d), F32)],
        compiler_params=pltpu.CompilerParams(dimension_semantics=("arbitrary",),
                                             vmem_limit_bytes=VMEM_LIMIT),
        name="mixer_ln1_route",
    )(x_prompt, x_prompt, x_prompt, x_sample, x_sample, x_sample, mod, mod, inv_tab, *weights)


def _rank_kernel(rid_ref, u_ref, tri_ref, dest_ref, meta_ref, cnt_ref, base_ref, *, nb_pad):
    phase = pl.program_id(0)
    j = pl.program_id(1)
    cls_iota = lax.broadcasted_iota(I32, (CLS_PAD, SUB), 0)

    @pl.when((phase == 0) & (j == 0))
    def _():
        cnt_ref[...] = jnp.zeros_like(cnt_ref)

    @pl.when(phase == 0)
    def _():
        acc = cnt_ref[...]
        for sb in range(BL // SUB):
            ids = rid_ref[0:1, sb * SUB:(sb + 1) * SUB]
            acc = acc + jnp.sum((cls_iota == ids).astype(F32), axis=1, keepdims=True)
        cnt_ref[...] = acc

    @pl.when((phase == 1) & (j == 0))
    def _():
        cnt = jnp.broadcast_to(cnt_ref[...], (CLS_PAD, CLS_PAD))
        nblk = jnp.floor((cnt + (RB - 1)) * (1.0 / RB))
        cum = jnp.dot(tri_ref[...], nblk, precision=lax.Precision.HIGHEST, preferred_element_type=F32)
        base_ref[...] = cum[:, 0:1] * RB - cnt[:, 0:1]
        blk = lax.broadcasted_iota(I32, (CLS_PAD, nb_pad), 1).astype(F32)
        bcls = jnp.sum((cum[:, 0:1] <= blk).astype(F32), axis=0, keepdims=True)
        bcls = jnp.minimum(bcls, N_CLASSES - 1).astype(I32)
        nused = jnp.broadcast_to(cum[CLS_PAD - 1:CLS_PAD, 0:1], (1, nb_pad)).astype(I32)
        first = cum[:, 0:1] - nblk[:, 0:1]
        inside = (first <= blk) & (blk < cum[:, 0:1])
        fill = jnp.where(blk == first, cnt[:, 0:1] - (nblk[:, 0:1] - 1.0) * RB, float(RB))
        valid = jnp.sum(jnp.where(inside, fill, 0.0), axis=0, keepdims=True).astype(I32)
        cum_l = jnp.transpose(cum)[0:1].astype(I32)
        nblk_l = jnp.transpose(nblk)[0:1].astype(I32)
        pad = jnp.zeros((1, nb_pad - CLS_PAD), I32)
        r8 = lax.broadcasted_iota(I32, (8, nb_pad), 0)
        cum_row = jnp.concatenate([cum_l, pad], axis=1)
        nblk_row = jnp.concatenate([nblk_l, pad], axis=1)
        meta_ref[...] = jnp.where(r8 == 0, bcls, jnp.where(r8 == 1, nused,
                                  jnp.where(r8 == 2, cum_row, jnp.where(r8 == 3, nblk_row,
                                                                        jnp.where(r8 == 4, valid, 0)))))

    @pl.when(phase == 1)
    def _():
        base = base_ref[...]
        for sb in range(BL // SUB):
            ids = rid_ref[0:1, sb * SUB:(sb + 1) * SUB]
            hit = cls_iota == ids
            incl = _mdot(hit.astype(MXU_DTYPE), u_ref[...])
            slot = jnp.sum(jnp.where(hit, base + incl - 1.0, 0.0), axis=0, keepdims=True)
            dest_ref[0:1, sb * SUB:(sb + 1) * SUB] = slot.astype(I32)
            base = base + incl[:, SUB - 1:SUB]
        base_ref[...] = base


def _rank_call(rid, u_mat, tri, nb_pad):
    n_tok = rid.shape[1]
    nj = n_tok // BL
    return pl.pallas_call(
        functools.partial(_rank_kernel, nb_pad=nb_pad),
        out_shape=(jax.ShapeDtypeStruct((1, n_tok), I32), jax.ShapeDtypeStruct((8, nb_pad), I32)),
        grid=(2, nj),
        in_specs=[pl.BlockSpec((8, BL), lambda p, j: (0, j)),
                  pl.BlockSpec(u_mat.shape, lambda p, j: (0, 0)),
                  pl.BlockSpec(tri.shape, lambda p, j: (0, 0))],
        out_specs=(pl.BlockSpec((1, BL), lambda p, j: (0, p * j)),
                   pl.BlockSpec((8, nb_pad), lambda p, j: (0, 0))),
        scratch_shapes=[pltpu.VMEM((CLS_PAD, 1), F32), pltpu.VMEM((CLS_PAD, 1), F32)],
        compiler_params=pltpu.CompilerParams(dimension_semantics=("arbitrary", "arbitrary"),
                                             vmem_limit_bytes=VMEM_LIMIT),
        name="rank_tokens",
    )(rid, u_mat, tri)


def _dispatch_kernel(cum_ref, nblk_ref, nused_ref, valid_ref, dest_hbm, h2_ref, gate_ref, wgu_ref, wd_ref, out_hbm,
                     wgu_o_ref, wd_o_ref, idx_smem, rows_buf, zero_ref, sem_idx, sem_row, sem_zero, *, gu_steps):
    i = pl.program_id(0)
    n_steps = pl.num_programs(0)

    @pl.when(i < gu_steps)
    def _():
        wgu_o_ref[...] = wgu_ref[...].astype(MXU_DTYPE)

    @pl.when(i >= gu_steps)
    def _():
        wd_o_ref[...] = wd_ref[...].astype(MXU_DTYPE)

    slot = i % 2
    tile_rows = TD * SUBLANES
    part_rows = ZERO_ROWS * SUBLANES
    parts = RB // ZERO_ROWS
    n_blocks = out_hbm.shape[0] // (RB * SUBLANES)

    def idx_copy(step, s):
        return pltpu.make_async_copy(dest_hbm.at[step], idx_smem.at[s], sem_idx.at[s])

    def scattered(s):
        return pltpu.make_async_copy(rows_buf.at[s], out_hbm.at[pl.ds(0, tile_rows)], sem_row.at[s])

    def zero_part(part):
        start = pl.multiple_of(part * part_rows, part_rows)
        return pltpu.make_async_copy(zero_ref, out_hbm.at[pl.ds(start, part_rows)], sem_zero)

    @pl.when(i == 0)
    def _():
        idx_copy(0, 0).start()
        zero_ref[...] = jnp.zeros_like(zero_ref)

        def class_parts(c, fn):
            first = cum_ref[c] - nblk_ref[c]
            padding = RB - valid_ref[jnp.maximum(first, 0)]
            for q in range(parts):
                @pl.when((nblk_ref[c] > 0) & (q * ZERO_ROWS < padding))
                def _(q=q):
                    fn(zero_part(first * parts + q))

        def start(c, carry):
            class_parts(c, lambda cp: cp.start())
            return carry

        def wait(c, carry):
            class_parts(c, lambda cp: cp.wait())
            return carry

        def start_tail(part, carry):
            zero_part(part).start()
            return carry

        def wait_tail(part, carry):
            zero_part(part).wait()
            return carry

        lax.fori_loop(0, N_CLASSES, start, 0)
        lax.fori_loop(nused_ref[0] * parts, n_blocks * parts, start_tail, 0)
        lax.fori_loop(0, N_CLASSES, wait, 0)
        lax.fori_loop(nused_ref[0] * parts, n_blocks * parts, wait_tail, 0)

    @pl.when(i >= 2)
    def _():
        scattered(slot).wait()

    idx_copy(i, slot).wait()

    @pl.when(i + 1 < n_steps)
    def _():
        idx_copy(i + 1, 1 - slot).start()

    for j in range(TD // PACK_TILE):
        r0 = j * PACK_TILE
        _pack_rows(h2_ref[r0:r0 + PACK_TILE, :], gate_ref[:, r0:r0 + PACK_TILE],
                   rows_buf.at[slot, pl.ds(r0 * SUBLANES, PACK_TILE * SUBLANES)])

        def scatter(g, carry, r0=r0):
            for u in range(ROW_UNROLL):
                t = r0 + g * ROW_UNROLL + u
                src = pl.multiple_of(t * SUBLANES, SUBLANES)
                dst = pl.multiple_of(idx_smem[slot, t] * SUBLANES, SUBLANES)
                pltpu.make_async_copy(rows_buf.at[slot, pl.ds(src, SUBLANES)], out_hbm.at[pl.ds(dst, SUBLANES)],
                                      sem_row.at[slot]).start(priority=u % 2)
            return carry

        lax.fori_loop(0, PACK_TILE // ROW_UNROLL, scatter, 0)

    @pl.when(i + 1 == n_steps)
    def _():
        @pl.when(n_steps > 1)
        def _():
            scattered(1 - slot).wait()

        scattered(slot).wait()


def _dispatch_call(cum_cls, nblk_cls, nused, blk_valid, dest2d, h2, gates, n_rows, w_gate_up, w_down):
    n_tok, d = h2.shape
    n_steps = n_tok // TD
    wgu2d = w_gate_up.reshape(-1, w_gate_up.shape[-1])
    wd2d = w_down.reshape(-1, w_down.shape[-1])
    assert wgu2d.shape[1] == wd2d.shape[1]
    slab = max(16, pl.next_power_of_2(pl.cdiv(wgu2d.shape[0] + wd2d.shape[0], n_steps)))
    assert wgu2d.shape[0] % slab == 0 and wd2d.shape[0] % slab == 0
    gu_steps, d_steps = wgu2d.shape[0] // slab, wd2d.shape[0] // slab
    assert gu_steps + d_steps <= n_steps

    def gu_map(i, *_):
        return (jnp.minimum(i, gu_steps - 1), 0)

    def d_map(i, *_):
        return (jnp.clip(i - gu_steps, 0, d_steps - 1), 0)

    wspec = lambda m: pl.BlockSpec((slab, wgu2d.shape[1]), m)
    grid_spec = pltpu.PrefetchScalarGridSpec(
        num_scalar_prefetch=4,
        grid=(n_steps,),
        in_specs=[pl.BlockSpec(memory_space=pl.ANY), pl.BlockSpec((TD, d), lambda i, *_: (i, 0)),
                  pl.BlockSpec((gates.shape[0], TD), lambda i, *_: (0, i)), wspec(gu_map), wspec(d_map)],
        out_specs=(pl.BlockSpec(memory_space=pl.ANY), wspec(gu_map), wspec(d_map)),
        scratch_shapes=[pltpu.SMEM((2, TD), I32), pltpu.VMEM((2, TD * SUBLANES, LANES), U32),
                        pltpu.VMEM((ZERO_ROWS * SUBLANES, LANES), U32),
                        pltpu.SemaphoreType.DMA((2,)), pltpu.SemaphoreType.DMA((2,)), pltpu.SemaphoreType.DMA],
    )
    sorted_rows, wgu_c, wd_c = pl.pallas_call(
        functools.partial(_dispatch_kernel, gu_steps=gu_steps),
        out_shape=(jax.ShapeDtypeStruct((n_rows * SUBLANES, LANES), U32),
                   jax.ShapeDtypeStruct(wgu2d.shape, MXU_DTYPE), jax.ShapeDtypeStruct(wd2d.shape, MXU_DTYPE)),
        grid_spec=grid_spec,
        compiler_params=pltpu.CompilerParams(dimension_semantics=("arbitrary",),
                                             vmem_limit_bytes=VMEM_LIMIT),
        name="dispatch_rows",
    )(cum_cls, nblk_cls, nused, blk_valid, dest2d, h2, gates, wgu2d, wd2d)
    return sorted_rows, wgu_c.reshape(w_gate_up.shape), wd_c.reshape(w_down.shape)


def _expert_kernel(ea_ref, eb_ref, nused_ref, valid_ref, rows_ref, wgu_a_ref, wd_a_ref, wgu_b_ref, wd_b_ref, y_ref):
    b = pl.program_id(0)
    live = b < nused_ref[0]
    half_rows = RB // 2

    def evaluate(m):
        row0 = (RB - m) * SUBLANES

        def tile_row(k):
            return rows_ref[pl.ds(row0 + k, m, stride=SUBLANES), :]

        words = [tile_row(k) for k in range(PACK_ROWS)]
        lo = [lax.bitcast_convert_type(jnp.left_shift(w, 16), F32) for w in words]
        hi = [lax.bitcast_convert_type(jnp.bitwise_and(w, jnp.uint32(HI_MASK)), F32) for w in words]
        x = jnp.concatenate(lo + hi, axis=1).astype(MXU_DTYPE)
        gate = lax.bitcast_convert_type(tile_row(PACK_ROWS), F32)

        def expert(wgu_ref, wd_ref):
            gu = _mdot(x, wgu_ref[0])
            act = jax.nn.silu(gu[:, :D_EXPERT]) * gu[:, D_EXPERT:]
            return _mdot(act.astype(MXU_DTYPE), wd_ref[0])

        y = expert(wgu_a_ref, wd_a_ref) * gate[:, 0:1] + expert(wgu_b_ref, wd_b_ref) * gate[:, 1:2]
        for k in range(Y_ROWS):
            y_ref[pl.ds(row0 + k, m, stride=SUBLANES), :] = y[:, k * LANES:(k + 1) * LANES]

    @pl.when(live & (valid_ref[b] > half_rows))
    def _():
        evaluate(RB)

    @pl.when(live & (valid_ref[b] <= half_rows))
    def _():
        evaluate(half_rows)
        y_ref[:half_rows * Y_ROWS, :] = jnp.zeros((half_rows * Y_ROWS, LANES), F32)

    @pl.when(jnp.logical_not(live))
    def _():
        y_ref[...] = jnp.zeros_like(y_ref)


def _expert_call(blk_ea, blk_eb, nused, blk_valid, rows, w_gate_up, w_down):
    n_rows = rows.shape[0] // SUBLANES
    nblk = n_rows // RB

    def live(b, nu):
        return jnp.maximum(jnp.minimum(b, nu[0] - 1), 0)

    grid_spec = pltpu.PrefetchScalarGridSpec(
        num_scalar_prefetch=4,
        grid=(nblk,),
        in_specs=[pl.BlockSpec((RB * SUBLANES, LANES), lambda b, ea, eb, nu, nv: (live(b, nu), 0)),
                  pl.BlockSpec((1,) + w_gate_up.shape[1:], lambda b, ea, eb, nu, nv: (ea[live(b, nu)], 0, 0)),
                  pl.BlockSpec((1,) + w_down.shape[1:], lambda b, ea, eb, nu, nv: (ea[live(b, nu)], 0, 0)),
                  pl.BlockSpec((1,) + w_gate_up.shape[1:], lambda b, ea, eb, nu, nv: (eb[live(b, nu)], 0, 0)),
                  pl.BlockSpec((1,) + w_down.shape[1:], lambda b, ea, eb, nu, nv: (eb[live(b, nu)], 0, 0))],
        out_specs=pl.BlockSpec((RB * Y_ROWS, LANES), lambda b, ea, eb, nu, nv: (b, 0)),
    )
    return pl.pallas_call(
        _expert_kernel,
        out_shape=jax.ShapeDtypeStruct((n_rows * Y_ROWS, LANES), F32),
        grid_spec=grid_spec,
        compiler_params=pltpu.CompilerParams(dimension_semantics=("arbitrary",),
                                             vmem_limit_bytes=VMEM_LIMIT),
        name="pair_experts",
    )(blk_ea, blk_eb, nused, blk_valid, rows, w_gate_up, w_down, w_gate_up, w_down)


def _combine_kernel(dest_hbm, y_hbm, x1_ref, mod_ref, g_ref, b_ref, o_ref, idx_smem, ybuf, sem_idx, sem_row,
                    *, tile_off):
    ns = pl.num_programs(1)
    i = pl.program_id(0) * ns + pl.program_id(1)
    n_steps = pl.num_programs(0) * ns
    slot = i % 2

    def idx_copy(step, s):
        return pltpu.make_async_copy(dest_hbm.at[tile_off + step], idx_smem.at[s], sem_idx.at[s])

    def issue_gather(s):
        def gather(g, carry):
            for u in range(ROW_UNROLL):
                t = g * ROW_UNROLL + u
                src = pl.multiple_of(idx_smem[s, t] * Y_ROWS, Y_ROWS)
                dst = pl.multiple_of(t * Y_ROWS, Y_ROWS)
                pltpu.make_async_copy(y_hbm.at[pl.ds(src, Y_ROWS)], ybuf.at[s, pl.ds(dst, Y_ROWS)],
                                      sem_row.at[s]).start(priority=u % 2)
            return carry

        lax.fori_loop(0, TD // ROW_UNROLL, gather, 0)

    @pl.when(i == 0)
    def _():
        idx_copy(0, 0).start()
        idx_copy(0, 0).wait()
        issue_gather(0)

        @pl.when(n_steps > 1)
        def _():
            idx_copy(1, 1).start()

    @pl.when(i + 1 < n_steps)
    def _():
        idx_copy(i + 1, 1 - slot).wait()
        issue_gather(1 - slot)

    @pl.when(i + 2 < n_steps)
    def _():
        idx_copy(i + 2, slot).start()

    pltpu.make_async_copy(y_hbm.at[pl.ds(0, TD * Y_ROWS)], ybuf.at[slot], sem_row.at[slot]).wait()

    y = jnp.concatenate([ybuf[slot, pl.ds(k, TD, stride=Y_ROWS), :] for k in range(Y_ROWS)], axis=1)
    g2 = mod_ref[0][5:6]
    o_ref[0] = _layer_norm(ALPHA * x1_ref[...] + g2 * y, g_ref[...], b_ref[...])


def _combine_call(dest2d, y_rows, x1, mod, mod_off, tile_off, ln2g, ln2b, bsz, seq):
    d = x1.shape[1]
    ns = seq // TD
    return pl.pallas_call(
        functools.partial(_combine_kernel, tile_off=tile_off),
        out_shape=jax.ShapeDtypeStruct((bsz, seq, d), F32),
        grid=(bsz, ns),
        in_specs=[pl.BlockSpec(memory_space=pl.ANY),
                  pl.BlockSpec(memory_space=pl.ANY),
                  pl.BlockSpec((TD, d), lambda b, s: (tile_off + b * ns + s, 0)),
                  pl.BlockSpec((1, N_MOD, d), lambda b, s: (mod_off + b, 0, 0)),
                  pl.BlockSpec((1, d), lambda b, s: (0, 0)),
                  pl.BlockSpec((1, d), lambda b, s: (0, 0))],
        out_specs=pl.BlockSpec((1, TD, d), lambda b, s: (b, s, 0)),
        scratch_shapes=[pltpu.SMEM((2, TD), I32), pltpu.VMEM((2, TD * Y_ROWS, LANES), F32),
                        pltpu.SemaphoreType.DMA((2,)), pltpu.SemaphoreType.DMA((2,))],
        compiler_params=pltpu.CompilerParams(dimension_semantics=("arbitrary", "arbitrary"),
                                             vmem_limit_bytes=VMEM_LIMIT),
        name="combine_ln2",
    )(dest2d, y_rows, x1, mod, ln2g, ln2b)


def _band_matrices():
    rows = np.arange(CHUNK)[:, None]
    cols = np.arange(CHUNK + 2 * POOL_HALO)[None, :] - POOL_HALO
    return np.stack([((cols >= rows - w // 2) & (cols < rows + w // 2)) for w in POOL_WINDOWS]).astype(np.float32)


def _class_tables():
    ea, eb = [], []
    for g in range(N_GROUPS):
        for a in range(EPG):
            for b in range(a + 1, EPG):
                ea.append(g * EPG + a)
                eb.append(g * EPG + b)
    return np.asarray(ea, np.int32), np.asarray(eb, np.int32)


def _split_hi_lo(w):
    hi = w.astype(MXU_DTYPE)
    lo = (w - hi.astype(F32)).astype(MXU_DTYPE)
    return hi, lo


def kernel(x_prompt, x_sample, c_prompt, c_sample, w_ada, b_ada, w_in, v_norm_g, v_norm_b, w_spatial, b_spatial,
           w_pool, pool_scale, w_out, ln1_g, ln1_b, w_route_group, b_route_group, w_route_expert, b_route_expert,
           w_gate_up, w_down, ln2_g, ln2_b):
    assert w_ada.shape[0] == 1, "single-layer kernel"
    bp, sp, d = x_prompt.shape
    bs, ss, _ = x_sample.shape
    assert d == D_MODEL and sp % TS == 0 and ss % TS == 0
    n_prompt, n_sample = bp * sp, bs * ss
    n_tok = n_prompt + n_sample
    assert n_tok % BL == 0 and n_prompt % TD == 0

    c_all = jnp.concatenate([c_prompt, c_sample], axis=0)
    mod = _modulation(c_all, w_ada[0], b_ada).reshape(bp + bs, N_MOD, d)

    wr = jnp.concatenate([w_route_group[0], w_route_expert[0]], axis=1).T
    wr = jnp.pad(wr, ((0, ROUTE_PAD - N_ROUTE), (0, 0)))
    wr_hi, wr_lo = _split_hi_lo(wr)
    rbias = jnp.concatenate([b_route_group[0], b_route_expert[0].reshape(-1),
                             jnp.zeros((ROUTE_PAD - N_ROUTE,), F32)]).reshape(ROUTE_PAD, 1)
    row = lambda a: a.reshape(1, -1)
    weights = (w_in[0].astype(MXU_DTYPE), row(v_norm_g[0]), row(v_norm_b[0]), w_spatial[0].astype(MXU_DTYPE),
               b_spatial[0].T, jnp.asarray(_band_matrices(), MXU_DTYPE), w_pool[0].astype(MXU_DTYPE),
               row(pool_scale[0]), w_out[0].astype(MXU_DTYPE), row(ln1_g[0]), row(ln1_b[0]),
               jnp.concatenate([wr_hi, wr_lo], axis=0), rbias)

    x1, h2, rid, gates = _mixer_call(x_prompt, x_sample, mod, weights)

    nblk = (n_tok + N_CLASSES * (RB - 1)) // RB
    nb_pad = -(-nblk // 128) * 128
    u_mat = jnp.asarray(np.triu(np.ones((SUB, SUB), np.float32)), MXU_DTYPE)
    tri = jnp.asarray(np.tril(np.ones((CLS_PAD, CLS_PAD), np.float32)))
    dest, meta = _rank_call(rid, u_mat, tri, nb_pad)
    dest2d = dest.reshape(n_tok // TD, TD)
    tab_a, tab_b = _class_tables()
    blk_cls = meta[0, :nblk]
    blk_ea = jnp.asarray(tab_a)[blk_cls]
    blk_eb = jnp.asarray(tab_b)[blk_cls]
    nused = meta[1, 0:1]
    cum_cls = meta[2, :CLS_PAD]
    nblk_cls = meta[3, :CLS_PAD]

    blk_valid = meta[4, :nblk]
    sorted_rows, wgu, wd = _dispatch_call(cum_cls, nblk_cls, nused, blk_valid, dest2d, h2, gates, nblk * RB,
                                          w_gate_up[0], w_down[0])
    y_rows = _expert_call(blk_ea, blk_eb, nused, blk_valid, sorted_rows, wgu, wd)

    ln2g, ln2b = row(ln2_g[0]), row(ln2_b[0])
    y_prompt = _combine_call(dest2d, y_rows, x1, mod, 0, 0, ln2g, ln2b, bp, sp)
    y_sample = _combine_call(dest2d, y_rows, x1, mod, bp, n_prompt // TD, ln2g, ln2b, bs, ss)
    return (y_prompt, y_sample)
```

```python
import functools

import numpy as np
import jax
import jax.numpy as jnp
from jax import lax
from jax.experimental import pallas as pl
from jax.experimental.pallas import tpu as pltpu

F32 = jnp.float32
I32 = jnp.int32
MXU_DTYPE = jnp.bfloat16

D_MODEL = 1024
A_WIDTH = 512
B_WIDTH = 512
CHUNK = 128
A_HEADS = 4
HEAD_DIM = A_WIDTH // A_HEADS
POOL_WINDOWS = (2, 4, 8, 16)
POOL_HALO = 8
GROUP_DIM = B_WIDTH // len(POOL_WINDOWS)
N_GROUPS = 4
EPG = 8
N_EXPERTS = N_GROUPS * EPG
D_EXPERT = 512
N_MOD = 6
LN_EPS = 1e-5
ALPHA = 2.0 ** 0.25

PAIRS = EPG * (EPG - 1) // 2
N_CLASSES = N_GROUPS * PAIRS
CLS_PAD = 128
N_ROUTE = N_GROUPS + N_EXPERTS
ROUTE_PAD = 40
U32 = jnp.uint32
LANES = 128
SUBLANES = 8
PACK_ROWS = D_MODEL // (2 * LANES)
Y_ROWS = D_MODEL // LANES
assert PACK_ROWS < SUBLANES and Y_ROWS == SUBLANES
HI_MASK = 0xFFFF0000

TS = 512
TD = 1024
RB = 512
PACK_TILE = 128
LN_CHUNK = 128
ZERO_ROWS = 128
ROW_UNROLL = 8
BL = 2048
SUB = 256
VMEM_LIMIT = 56 * 1024 * 1024

_NT = (((1,), (1,)), ((), ()))

_GELU_K1 = -2.0 * (2.0 / np.pi) ** 0.5
_GELU_K3 = _GELU_K1 * 0.044715


def _gelu_tanh(x):
    return x / (1.0 + jnp.exp(x * (_GELU_K1 + _GELU_K3 * (x * x))))


def _layer_norm(x, g, b):
    mu = jnp.mean(x, axis=-1, keepdims=True)
    xc = x - mu
    var = jnp.mean(xc * xc, axis=-1, keepdims=True)
    return xc * lax.rsqrt(var + LN_EPS) * g + b


def _mdot(a, b):
    return jnp.dot(a, b, preferred_element_type=F32)


def _mod_kernel(c_ref, w_ref, b_ref, o_ref):
    a = jax.nn.silu(c_ref[...])
    o_ref[...] = jnp.dot(a, w_ref[...], precision=lax.Precision.HIGHEST,
                         preferred_element_type=F32) + b_ref[...]


def _modulation(c_all, w_ada, b_ada):
    nb, d = c_all.shape
    n = w_ada.shape[1]
    bn = 1536
    return pl.pallas_call(
        _mod_kernel,
        out_shape=jax.ShapeDtypeStruct((nb, n), F32),
        grid=(n // bn,),
        in_specs=[pl.BlockSpec((nb, d), lambda j: (0, 0)),
                  pl.BlockSpec((d, bn), lambda j: (0, j)),
                  pl.BlockSpec((1, bn), lambda j: (0, j))],
        out_specs=pl.BlockSpec((nb, bn), lambda j: (0, j)),
        compiler_params=pltpu.CompilerParams(dimension_semantics=("arbitrary",),
                                             vmem_limit_bytes=VMEM_LIMIT),
        name="adaln_mod",
    )(c_all, w_ada, b_ada)


def _mixer_kernel(xp_ref, xpp_ref, xpn_ref, xs_ref, xsp_ref, xsn_ref, mod_ref, modp_ref, inv_ref, win_ref, vng_ref,
                  vnb_ref, ws_ref, bst_ref, band_ref, wpool_ref, pscale_ref, wout_ref, ln1g_ref, ln1b_ref, wr_ref,
                  rb_ref, x1_ref, h2_ref, rid_ref, gate_ref, res_ref, *, prompt_tiles, prompt_seq, sample_seq):
    ts = xp_ref.shape[1]
    i = pl.program_id(0)
    slot = i % 2

    @pl.when(i == 0)
    def _():
        res_ref[1] = jnp.zeros(res_ref.shape[1:], F32)

    is_p = i < prompt_tiles
    ns = jnp.where(is_p, prompt_seq // ts, sample_seq // ts)
    s = jnp.where(is_p, i, i - prompt_tiles) % ns
    md = mod_ref[0]
    sh1, sc1, g1 = md[0:1], md[1:2], md[2:3]

    xt = jnp.where(is_p, xp_ref[0], xs_ref[0])
    h = xt * (1.0 + sc1) + sh1
    hp = jnp.where(s > 0, jnp.where(is_p, xpp_ref[0], xsp_ref[0]) * (1.0 + sc1) + sh1, 0.0)
    hn = jnp.where(s < ns - 1, jnp.where(is_p, xpn_ref[0], xsn_ref[0]) * (1.0 + sc1) + sh1, 0.0)
    hext = jnp.concatenate([h, hp, hn], axis=0).astype(MXU_DTYPE)
    zext = _mdot(hext, win_ref[...])
    z = zext[:ts]

    za = _gelu_tanh(z[:, :2 * A_WIDTH])
    u = za[:, :A_WIDTH]
    v = _layer_norm(za[:, A_WIDTH:], vng_ref[...], vnb_ref[...]).astype(MXU_DTYPE)
    bst = bst_ref[...]
    n_chunks = ts // CHUNK

    def chunks_on_lanes(a, rows, col0, width):
        return jnp.concatenate([a[c * CHUNK:c * CHUNK + rows, col0:col0 + width] for c in range(n_chunks)], axis=1)

    def chunks_on_rows(a, width):
        return jnp.concatenate([a[:, c * width:(c + 1) * width] for c in range(n_chunks)], axis=0)

    heads = [_mdot(ws_ref[hh], chunks_on_lanes(v, CHUNK, hh * HEAD_DIM, HEAD_DIM)) + bst[:, hh:hh + 1]
             for hh in range(A_HEADS)]
    a_out = u * jnp.concatenate([chunks_on_rows(o, HEAD_DIM) for o in heads], axis=1)

    p = z[:, 2 * A_WIDTH:]
    pall = jnp.concatenate([zext[ts:ts + POOL_HALO, 2 * A_WIDTH:], p,
                            zext[ts + POOL_HALO:ts + 2 * POOL_HALO, 2 * A_WIDTH:]], axis=0)
    p_hi = pall.astype(MXU_DTYPE)
    p_lo = (pall - p_hi.astype(F32)).astype(MXU_DTYPE)
    b_cols = []
    for g in range(len(POOL_WINDOWS)):
        seg_rows = CHUNK + 2 * POOL_HALO
        win = (_mdot(band_ref[g], chunks_on_lanes(p_hi, seg_rows, g * GROUP_DIM, GROUP_DIM))
               + _mdot(band_ref[g], chunks_on_lanes(p_lo, seg_rows, g * GROUP_DIM, GROUP_DIM)))
        pooled = chunks_on_rows(win, GROUP_DIM) * inv_ref[0, g] - p[:, g * GROUP_DIM:(g + 1) * GROUP_DIM]
        b_cols.append(_mdot(pooled.astype(MXU_DTYPE), wpool_ref[g]))
    b_out = jnp.concatenate(b_cols, axis=1) * pscale_ref[...]

    mix_in = jnp.concatenate([a_out, b_out], axis=1).astype(MXU_DTYPE)
    mdp = modp_ref[0]
    h2_prev, lt_prev = _ln1_router(res_ref[1 - slot], mdp[3:4], mdp[4:5], ln1g_ref, ln1b_ref, wr_ref, rb_ref,
                                   x1_ref)
    _route(h2_prev, lt_prev, h2_ref, rid_ref, gate_ref)

    mix = _mdot(mix_in, wout_ref[...])
    res_ref[slot] = ALPHA * xt + g1 * mix


def _ln1_router(res, sh2, sc2, ln1g_ref, ln1b_ref, wr_ref, rb_ref, x1_ref):
    x1 = _layer_norm(res, ln1g_ref[...], ln1b_ref[...])
    x1_ref[...] = x1
    h2 = x1 * (1.0 + sc2) + sh2
    h2_hi = h2.astype(MXU_DTYPE)
    h2_lo = (h2 - h2_hi.astype(F32)).astype(MXU_DTYPE)
    l1 = lax.dot_general(wr_ref[...], h2_hi, _NT, preferred_element_type=F32)
    l2 = lax.dot_general(wr_ref[0:ROUTE_PAD], h2_lo, _NT, preferred_element_type=F32)
    return h2, l1[:ROUTE_PAD] + l1[ROUTE_PAD:] + l2 + rb_ref[...]


def _route(h2, lt, h2_ref, rid_ref, gate_ref):
    ts = h2.shape[0]

    def row(r):
        return lt[r:r + 1, :]

    gl = [row(r) for r in range(N_GROUPS)]
    gmax = jnp.maximum(jnp.maximum(gl[0], gl[1]), jnp.maximum(gl[2], gl[3]))
    gidx = jnp.where(gl[0] == gmax, 0, jnp.where(gl[1] == gmax, 1, jnp.where(gl[2] == gmax, 2, 3)))
    gsum = (jnp.exp(gl[0] - gmax) + jnp.exp(gl[1] - gmax)) + (jnp.exp(gl[2] - gmax) + jnp.exp(gl[3] - gmax))
    gw = 1.0 / gsum
    ev = [jnp.where(gidx == 0, row(N_GROUPS + j),
                    jnp.where(gidx == 1, row(N_GROUPS + EPG + j),
                              jnp.where(gidx == 2, row(N_GROUPS + 2 * EPG + j), row(N_GROUPS + 3 * EPG + j))))
          for j in range(EPG)]

    def top1(vals):
        m = vals[0]
        for t in vals[1:]:
            m = jnp.maximum(m, t)
        idx = jnp.full(m.shape, EPG - 1, I32)
        for j in range(EPG - 2, -1, -1):
            idx = jnp.where(vals[j] == m, j, idx)
        return m, idx

    v1, j1 = top1(ev)
    v2, j2 = top1([jnp.where(j1 == j, -jnp.inf, ev[j]) for j in range(EPG)])
    t2 = jnp.exp(v2 - v1)
    den = 1.0 + t2
    w1 = (1.0 / den) * gw
    w2 = (t2 / den) * gw
    first = j1 < j2
    ea = jnp.minimum(j1, j2)
    eb = jnp.maximum(j1, j2)
    wa = jnp.where(first, w1, w2)
    wb = jnp.where(first, w2, w1)
    cls = gidx * PAIRS + jnp.right_shift(ea * (2 * EPG - 1 - ea), 1) + (eb - ea - 1)

    r8 = lax.broadcasted_iota(I32, (8, ts), 0)
    rid_ref[...] = jnp.where(r8 == 0, cls, jnp.where(r8 == 1, gidx * EPG + ea,
                                                      jnp.where(r8 == 2, gidx * EPG + eb, 0)))
    gate_ref[...] = jnp.where(r8 == 0, wa, jnp.where(r8 == 1, wb, 0.0))
    h2_ref[...] = h2.astype(jnp.bfloat16)


def _pack_rows(h2, gates, rows_ref):
    n = h2.shape[0]
    bits = lax.bitcast_convert_type(h2.astype(F32), U32)
    half = D_MODEL // 2
    for k in range(PACK_ROWS):
        lo = jnp.right_shift(bits[:, k * LANES:(k + 1) * LANES], 16)
        hi = jnp.bitwise_and(bits[:, half + k * LANES:half + (k + 1) * LANES], jnp.uint32(HI_MASK))
        rows_ref[pl.ds(k, n, stride=SUBLANES), :] = jnp.bitwise_or(lo, hi)
    g128 = jnp.concatenate([gates, jnp.zeros((LANES - gates.shape[0], n), F32)], axis=0)
    rows_ref[pl.ds(PACK_ROWS, n, stride=SUBLANES), :] = lax.bitcast_convert_type(g128.T, U32)
    for k in range(PACK_ROWS + 1, SUBLANES):
        rows_ref[pl.ds(k, n, stride=SUBLANES), :] = jnp.zeros((n, LANES), U32)


def _inv_population_table():
    r = np.arange(TS)
    out = np.empty((3, len(POOL_WINDOWS), TS, LANES), np.float32)
    for case in range(3):
        for g, w in enumerate(POOL_WINDOWS):
            lo = np.maximum(r - w // 2, 0) if case == 0 else r - w // 2
            hi = np.minimum(r + w // 2, TS) if case == 2 else r + w // 2
            out[case, g] = (1.0 / (hi - lo).astype(np.float64)).astype(np.float32)[:, None]
    return out


def _mixer_call(x_prompt, x_sample, mod, weights):
    bp, sp, d = x_prompt.shape
    bs, ss, _ = x_sample.shape
    nsp, nss = sp // TS, ss // TS
    assert nsp >= 2 and nss >= 2, "a tile is the first or the last of its sequence, not both"
    ntp, nts = bp * nsp, bs * nss
    n_tok = bp * sp + bs * ss
    hb = TS // POOL_HALO

    def p_tile(i):
        t = jnp.minimum(i, ntp - 1)
        return t // nsp, t % nsp

    def s_tile(i):
        t = jnp.clip(i - ntp, 0, nts - 1)
        return t // nss, t % nss

    def specs(tile_fn, seq):
        def cur(i):
            b, s = tile_fn(i)
            return (b, s, 0)

        def prev(i):
            b, s = tile_fn(i)
            return (b, jnp.maximum(s * hb - 1, 0), 0)

        def nxt(i):
            b, s = tile_fn(i)
            return (b, jnp.minimum((s + 1) * hb, seq // POOL_HALO - 1), 0)

        return [pl.BlockSpec((1, TS, d), cur), pl.BlockSpec((1, POOL_HALO, d), prev),
                pl.BlockSpec((1, POOL_HALO, d), nxt)]

    def mod_map(i):
        return (jnp.where(i < ntp, p_tile(i)[0], bp + s_tile(i)[0]), 0, 0)

    def inv_map(i):
        s = jnp.where(i < ntp, p_tile(i)[1], s_tile(i)[1])
        last = jnp.where(i < ntp, nsp - 1, nss - 1)
        return (jnp.where(s == 0, 0, jnp.where(s == last, 2, 1)), 0, 0, 0)

    def const(w):
        return pl.BlockSpec(w.shape, lambda i, nd=w.ndim: (0,) * nd)

    def prev_tile(i):
        return jnp.maximum(i - 1, 0)

    inv_tab = jnp.asarray(_inv_population_table())
    in_specs = (specs(p_tile, sp) + specs(s_tile, ss)
                + [pl.BlockSpec((1, N_MOD, d), mod_map),
                   pl.BlockSpec((1, N_MOD, d), lambda i: mod_map(prev_tile(i))),
                   pl.BlockSpec((1,) + inv_tab.shape[1:], inv_map)]
                + [const(w) for w in weights])
    out_shape = (jax.ShapeDtypeStruct((n_tok, d), F32),
                 jax.ShapeDtypeStruct((n_tok, d), jnp.bfloat16),
                 jax.ShapeDtypeStruct((8, n_tok), I32),
                 jax.ShapeDtypeStruct((8, n_tok), F32))
    out_specs = (pl.BlockSpec((TS, d), lambda i: (prev_tile(i), 0)),
                 pl.BlockSpec((TS, d), lambda i: (prev_tile(i), 0)),
                 pl.BlockSpec((8, TS), lambda i: (0, prev_tile(i))),
                 pl.BlockSpec((8, TS), lambda i: (0, prev_tile(i))))
    return pl.pallas_call(
        functools.partial(_mixer_kernel, prompt_tiles=ntp, prompt_seq=sp, sample_seq=ss),
        out_shape=out_shape,
        grid=(ntp + nts + 1,),
        in_specs=in_specs,
        out_specs=out_specs,
        scratch_shapes=[pltpu.VMEM((2, TS, d), F32)],
        compiler_params=pltpu.CompilerParams(dimension_semantics=("arbitrary",),
                                             vmem_limit_bytes=VMEM_LIMIT),
        name="mixer_ln1_route",
    )(x_prompt, x_prompt, x_prompt, x_sample, x_sample, x_sample, mod, mod, inv_tab, *weights)


def _rank_kernel(rid_ref, u_ref, tri_ref, dest_ref, meta_ref, cnt_ref, base_ref, *, nb_pad):
    phase = pl.program_id(0)
    j = pl.program_id(1)
    cls_iota = lax.broadcasted_iota(I32, (CLS_PAD, SUB), 0)

    @pl.when((phase == 0) & (j == 0))
    def _():
        cnt_ref[...] = jnp.zeros_like(cnt_ref)

    @pl.when(phase == 0)
    def _():
        acc = cnt_ref[...]
        for sb in range(BL // SUB):
            ids = rid_ref[0:1, sb * SUB:(sb + 1) * SUB]
            acc = acc + jnp.sum((cls_iota == ids).astype(F32), axis=1, keepdims=True)
        cnt_ref[...] = acc

    @pl.when((phase == 1) & (j == 0))
    def _():
        cnt = jnp.broadcast_to(cnt_ref[...], (CLS_PAD, CLS_PAD))
        nblk = jnp.floor((cnt + (RB - 1)) * (1.0 / RB))
        cum = jnp.dot(tri_ref[...], nblk, precision=lax.Precision.HIGHEST, preferred_element_type=F32)
        base_ref[...] = cum[:, 0:1] * RB - cnt[:, 0:1]
        blk = lax.broadcasted_iota(I32, (CLS_PAD, nb_pad), 1).astype(F32)
        bcls = jnp.sum((cum[:, 0:1] <= blk).astype(F32), axis=0, keepdims=True)
        bcls = jnp.minimum(bcls, N_CLASSES - 1).astype(I32)
        nused = jnp.broadcast_to(cum[CLS_PAD - 1:CLS_PAD, 0:1], (1, nb_pad)).astype(I32)
        first = cum[:, 0:1] - nblk[:, 0:1]
        inside = (first <= blk) & (blk < cum[:, 0:1])
        fill = jnp.where(blk == first, cnt[:, 0:1] - (nblk[:, 0:1] - 1.0) * RB, float(RB))
        valid = jnp.sum(jnp.where(inside, fill, 0.0), axis=0, keepdims=True).astype(I32)
        cum_l = jnp.transpose(cum)[0:1].astype(I32)
        nblk_l = jnp.transpose(nblk)[0:1].astype(I32)
        pad = jnp.zeros((1, nb_pad - CLS_PAD), I32)
        r8 = lax.broadcasted_iota(I32, (8, nb_pad), 0)
        cum_row = jnp.concatenate([cum_l, pad], axis=1)
        nblk_row = jnp.concatenate([nblk_l, pad], axis=1)
        meta_ref[...] = jnp.where(r8 == 0, bcls, jnp.where(r8 == 1, nused,
                                  jnp.where(r8 == 2, cum_row, jnp.where(r8 == 3, nblk_row,
                                                                        jnp.where(r8 == 4, valid, 0)))))

    @pl.when(phase == 1)
    def _():
        base = base_ref[...]
        for sb in range(BL // SUB):
            ids = rid_ref[0:1, sb * SUB:(sb + 1) * SUB]
            hit = cls_iota == ids
            incl = _mdot(hit.astype(MXU_DTYPE), u_ref[...])
            slot = jnp.sum(jnp.where(hit, base + incl - 1.0, 0.0), axis=0, keepdims=True)
            dest_ref[0:1, sb * SUB:(sb + 1) * SUB] = slot.astype(I32)
            base = base + incl[:, SUB - 1:SUB]
        base_ref[...] = base


def _rank_call(rid, u_mat, tri, nb_pad):
    n_tok = rid.shape[1]
    nj = n_tok // BL
    return pl.pallas_call(
        functools.partial(_rank_kernel, nb_pad=nb_pad),
        out_shape=(jax.ShapeDtypeStruct((1, n_tok), I32), jax.ShapeDtypeStruct((8, nb_pad), I32)),
        grid=(2, nj),
        in_specs=[pl.BlockSpec((8, BL), lambda p, j: (0, j)),
                  pl.BlockSpec(u_mat.shape, lambda p, j: (0, 0)),
                  pl.BlockSpec(tri.shape, lambda p, j: (0, 0))],
        out_specs=(pl.BlockSpec((1, BL), lambda p, j: (0, p * j)),
                   pl.BlockSpec((8, nb_pad), lambda p, j: (0, 0))),
        scratch_shapes=[pltpu.VMEM((CLS_PAD, 1), F32), pltpu.VMEM((CLS_PAD, 1), F32)],
        compiler_params=pltpu.CompilerParams(dimension_semantics=("arbitrary", "arbitrary"),
                                             vmem_limit_bytes=VMEM_LIMIT),
        name="rank_tokens",
    )(rid, u_mat, tri)


def _dispatch_kernel(cum_ref, nblk_ref, nused_ref, valid_ref, dest_hbm, h2_ref, gate_ref, wgu_ref, wd_ref, out_hbm,
                     wgu_o_ref, wd_o_ref, idx_smem, rows_buf, zero_ref, sem_idx, sem_row, sem_zero, *, gu_steps):
    i = pl.program_id(0)
    n_steps = pl.num_programs(0)

    @pl.when(i < gu_steps)
    def _():
        wgu_o_ref[...] = wgu_ref[...].astype(MXU_DTYPE)

    @pl.when(i >= gu_steps)
    def _():
        wd_o_ref[...] = wd_ref[...].astype(MXU_DTYPE)

    slot = i % 2
    tile_rows = TD * SUBLANES
    part_rows = ZERO_ROWS * SUBLANES
    parts = RB // ZERO_ROWS
    n_blocks = out_hbm.shape[0] // (RB * SUBLANES)

    def idx_copy(step, s):
        return pltpu.make_async_copy(dest_hbm.at[step], idx_smem.at[s], sem_idx.at[s])

    def scattered(s):
        return pltpu.make_async_copy(rows_buf.at[s], out_hbm.at[pl.ds(0, tile_rows)], sem_row.at[s])

    def zero_part(part):
        start = pl.multiple_of(part * part_rows, part_rows)
        return pltpu.make_async_copy(zero_ref, out_hbm.at[pl.ds(start, part_rows)], sem_zero)

    @pl.when(i == 0)
    def _():
        idx_copy(0, 0).start()
        zero_ref[...] = jnp.zeros_like(zero_ref)

        def class_parts(c, fn):
            first = cum_ref[c] - nblk_ref[c]
            padding = RB - valid_ref[jnp.maximum(first, 0)]
            for q in range(parts):
                @pl.when((nblk_ref[c] > 0) & (q * ZERO_ROWS < padding))
                def _(q=q):
                    fn(zero_part(first * parts + q))

        def start(c, carry):
            class_parts(c, lambda cp: cp.start())
            return carry

        def wait(c, carry):
            class_parts(c, lambda cp: cp.wait())
            return carry

        def start_tail(part, carry):
            zero_part(part).start()
            return carry

        def wait_tail(part, carry):
            zero_part(part).wait()
            return carry

        lax.fori_loop(0, N_CLASSES, start, 0)
        lax.fori_loop(nused_ref[0] * parts, n_blocks * parts, start_tail, 0)
        lax.fori_loop(0, N_CLASSES, wait, 0)
        lax.fori_loop(nused_ref[0] * parts, n_blocks * parts, wait_tail, 0)

    @pl.when(i >= 2)
    def _():
        scattered(slot).wait()

    idx_copy(i, slot).wait()

    @pl.when(i + 1 < n_steps)
    def _():
        idx_copy(i + 1, 1 - slot).start()

    for j in range(TD // PACK_TILE):
        r0 = j * PACK_TILE
        _pack_rows(h2_ref[r0:r0 + PACK_TILE, :], gate_ref[:, r0:r0 + PACK_TILE],
                   rows_buf.at[slot, pl.ds(r0 * SUBLANES, PACK_TILE * SUBLANES)])

        def scatter(g, carry, r0=r0):
            for u in range(ROW_UNROLL):
                t = r0 + g * ROW_UNROLL + u
                src = pl.multiple_of(t * SUBLANES, SUBLANES)
                dst = pl.multiple_of(idx_smem[slot, t] * SUBLANES, SUBLANES)
                pltpu.make_async_copy(rows_buf.at[slot, pl.ds(src, SUBLANES)], out_hbm.at[pl.ds(dst, SUBLANES)],
                                      sem_row.at[slot]).start(priority=u % 2)
            return carry

        lax.fori_loop(0, PACK_TILE // ROW_UNROLL, scatter, 0)

    @pl.when(i + 1 == n_steps)
    def _():
        @pl.when(n_steps > 1)
        def _():
            scattered(1 - slot).wait()

        scattered(slot).wait()


def _dispatch_call(cum_cls, nblk_cls, nused, blk_valid, dest2d, h2, gates, n_rows, w_gate_up, w_down):
    n_tok, d = h2.shape
    n_steps = n_tok // TD
    wgu2d = w_gate_up.reshape(-1, w_gate_up.shape[-1])
    wd2d = w_down.reshape(-1, w_down.shape[-1])
    assert wgu2d.shape[1] == wd2d.shape[1]
    slab = max(16, pl.next_power_of_2(pl.cdiv(wgu2d.shape[0] + wd2d.shape[0], n_steps)))
    assert wgu2d.shape[0] % slab == 0 and wd2d.shape[0] % slab == 0
    gu_steps, d_steps = wgu2d.shape[0] // slab, wd2d.shape[0] // slab
    assert gu_steps + d_steps <= n_steps

    def gu_map(i, *_):
        return (jnp.minimum(i, gu_steps - 1), 0)

    def d_map(i, *_):
        return (jnp.clip(i - gu_steps, 0, d_steps - 1), 0)

    wspec = lambda m: pl.BlockSpec((slab, wgu2d.shape[1]), m)
    grid_spec = pltpu.PrefetchScalarGridSpec(
        num_scalar_prefetch=4,
        grid=(n_steps,),
        in_specs=[pl.BlockSpec(memory_space=pl.ANY), pl.BlockSpec((TD, d), lambda i, *_: (i, 0)),
                  pl.BlockSpec((gates.shape[0], TD), lambda i, *_: (0, i)), wspec(gu_map), wspec(d_map)],
        out_specs=(pl.BlockSpec(memory_space=pl.ANY), wspec(gu_map), wspec(d_map)),
        scratch_shapes=[pltpu.SMEM((2, TD), I32), pltpu.VMEM((2, TD * SUBLANES, LANES), U32),
                        pltpu.VMEM((ZERO_ROWS * SUBLANES, LANES), U32),
                        pltpu.SemaphoreType.DMA((2,)), pltpu.SemaphoreType.DMA((2,)), pltpu.SemaphoreType.DMA],
    )
    sorted_rows, wgu_c, wd_c = pl.pallas_call(
        functools.partial(_dispatch_kernel, gu_steps=gu_steps),
        out_shape=(jax.ShapeDtypeStruct((n_rows * SUBLANES, LANES), U32),
                   jax.ShapeDtypeStruct(wgu2d.shape, MXU_DTYPE), jax.ShapeDtypeStruct(wd2d.shape, MXU_DTYPE)),
        grid_spec=grid_spec,
        compiler_params=pltpu.CompilerParams(dimension_semantics=("arbitrary",),
                                             vmem_limit_bytes=VMEM_LIMIT),
        name="dispatch_rows",
    )(cum_cls, nblk_cls, nused, blk_valid, dest2d, h2, gates, wgu2d, wd2d)
    return sorted_rows, wgu_c.reshape(w_gate_up.shape), wd_c.reshape(w_down.shape)


def _expert_kernel(ea_ref, eb_ref, nused_ref, valid_ref, rows_ref, wgu_a_ref, wd_a_ref, wgu_b_ref, wd_b_ref, y_ref):
    b = pl.program_id(0)
    live = b < nused_ref[0]
    half_rows = RB // 2

    def evaluate(m):
        row0 = (RB - m) * SUBLANES

        def tile_row(k):
            return rows_ref[pl.ds(row0 + k, m, stride=SUBLANES), :]

        words = [tile_row(k) for k in range(PACK_ROWS)]
        lo = [lax.bitcast_convert_type(jnp.left_shift(w, 16), F32) for w in words]
        hi = [lax.bitcast_convert_type(jnp.bitwise_and(w, jnp.uint32(HI_MASK)), F32) for w in words]
        x = jnp.concatenate(lo + hi, axis=1).astype(MXU_DTYPE)
        gate = lax.bitcast_convert_type(tile_row(PACK_ROWS), F32)

        def expert(wgu_ref, wd_ref):
            gu = _mdot(x, wgu_ref[0])
            act = jax.nn.silu(gu[:, :D_EXPERT]) * gu[:, D_EXPERT:]
            return _mdot(act.astype(MXU_DTYPE), wd_ref[0])

        y = expert(wgu_a_ref, wd_a_ref) * gate[:, 0:1] + expert(wgu_b_ref, wd_b_ref) * gate[:, 1:2]
        for k in range(Y_ROWS):
            y_ref[pl.ds(row0 + k, m, stride=SUBLANES), :] = y[:, k * LANES:(k + 1) * LANES]

    @pl.when(live & (valid_ref[b] > half_rows))
    def _():
        evaluate(RB)

    @pl.when(live & (valid_ref[b] <= half_rows))
    def _():
        evaluate(half_rows)
        y_ref[:half_rows * Y_ROWS, :] = jnp.zeros((half_rows * Y_ROWS, LANES), F32)

    @pl.when(jnp.logical_not(live))
    def _():
        y_ref[...] = jnp.zeros_like(y_ref)


def _expert_call(blk_ea, blk_eb, nused, blk_valid, rows, w_gate_up, w_down):
    n_rows = rows.shape[0] // SUBLANES
    nblk = n_rows // RB

    def live(b, nu):
        return jnp.maximum(jnp.minimum(b, nu[0] - 1), 0)

    grid_spec = pltpu.PrefetchScalarGridSpec(
        num_scalar_prefetch=4,
        grid=(nblk,),
        in_specs=[pl.BlockSpec((RB * SUBLANES, LANES), lambda b, ea, eb, nu, nv: (live(b, nu), 0)),
                  pl.BlockSpec((1,) + w_gate_up.shape[1:], lambda b, ea, eb, nu, nv: (ea[live(b, nu)], 0, 0)),
                  pl.BlockSpec((1,) + w_down.shape[1:], lambda b, ea, eb, nu, nv: (ea[live(b, nu)], 0, 0)),
                  pl.BlockSpec((1,) + w_gate_up.shape[1:], lambda b, ea, eb, nu, nv: (eb[live(b, nu)], 0, 0)),
                  pl.BlockSpec((1,) + w_down.shape[1:], lambda b, ea, eb, nu, nv: (eb[live(b, nu)], 0, 0))],
        out_specs=pl.BlockSpec((RB * Y_ROWS, LANES), lambda b, ea, eb, nu, nv: (b, 0)),
    )
    return pl.pallas_call(
        _expert_kernel,
        out_shape=jax.ShapeDtypeStruct((n_rows * Y_ROWS, LANES), F32),
        grid_spec=grid_spec,
        compiler_params=pltpu.CompilerParams(dimension_semantics=("arbitrary",),
                                             vmem_limit_bytes=VMEM_LIMIT),
        name="pair_experts",
    )(blk_ea, blk_eb, nused, blk_valid, rows, w_gate_up, w_down, w_gate_up, w_down)


def _combine_kernel(dest_hbm, y_hbm, x1_ref, mod_ref, g_ref, b_ref, o_ref, idx_smem, ybuf, sem_idx, sem_row,
                    *, tile_off):
    ns = pl.num_programs(1)
    i = pl.program_id(0) * ns + pl.program_id(1)
    n_steps = pl.num_programs(0) * ns
    slot = i % 2

    def idx_copy(step, s):
        return pltpu.make_async_copy(dest_hbm.at[tile_off + step], idx_smem.at[s], sem_idx.at[s])

    def issue_gather(s, row0, rows):
        def gather(g, carry):
            for u in range(ROW_UNROLL):
                t = row0 + g * ROW_UNROLL + u
                src = pl.multiple_of(idx_smem[s, t] * Y_ROWS, Y_ROWS)
                dst = pl.multiple_of(t * Y_ROWS, Y_ROWS)
                pltpu.make_async_copy(y_hbm.at[pl.ds(src, Y_ROWS)], ybuf.at[s, pl.ds(dst, Y_ROWS)],
                                      sem_row.at[s]).start(priority=u % 2)
            return carry

        lax.fori_loop(0, rows // ROW_UNROLL, gather, 0)

    @pl.when(i == 0)
    def _():
        idx_copy(0, 0).start()
        idx_copy(0, 0).wait()
        issue_gather(0, 0, TD)

        @pl.when(n_steps > 1)
        def _():
            idx_copy(1, 1).start()

    @pl.when(i + 1 < n_steps)
    def _():
        idx_copy(i + 1, 1 - slot).wait()

    @pl.when(i + 2 < n_steps)
    def _():
        idx_copy(i + 2, slot).start()

    pltpu.make_async_copy(y_hbm.at[pl.ds(0, TD * Y_ROWS)], ybuf.at[slot], sem_row.at[slot]).wait()

    g2 = mod_ref[0][5:6]
    for j in range(TD // LN_CHUNK):
        r0 = j * LN_CHUNK

        @pl.when(i + 1 < n_steps)
        def _(r0=r0):
            issue_gather(1 - slot, r0, LN_CHUNK)

        y = jnp.concatenate([ybuf[slot, pl.ds(r0 * Y_ROWS + k, LN_CHUNK, stride=Y_ROWS), :]
                             for k in range(Y_ROWS)], axis=1)
        o_ref[0, r0:r0 + LN_CHUNK, :] = _layer_norm(ALPHA * x1_ref[r0:r0 + LN_CHUNK, :] + g2 * y,
                                                    g_ref[...], b_ref[...])


def _combine_call(dest2d, y_rows, x1, mod, mod_off, tile_off, ln2g, ln2b, bsz, seq):
    d = x1.shape[1]
    ns = seq // TD
    return pl.pallas_call(
        functools.partial(_combine_kernel, tile_off=tile_off),
        out_shape=jax.ShapeDtypeStruct((bsz, seq, d), F32),
        grid=(bsz, ns),
        in_specs=[pl.BlockSpec(memory_space=pl.ANY),
                  pl.BlockSpec(memory_space=pl.ANY),
                  pl.BlockSpec((TD, d), lambda b, s: (tile_off + b * ns + s, 0)),
                  pl.BlockSpec((1, N_MOD, d), lambda b, s: (mod_off + b, 0, 0)),
                  pl.BlockSpec((1, d), lambda b, s: (0, 0)),
                  pl.BlockSpec((1, d), lambda b, s: (0, 0))],
        out_specs=pl.BlockSpec((1, TD, d), lambda b, s: (b, s, 0)),
        scratch_shapes=[pltpu.SMEM((2, TD), I32), pltpu.VMEM((2, TD * Y_ROWS, LANES), F32),
                        pltpu.SemaphoreType.DMA((2,)), pltpu.SemaphoreType.DMA((2,))],
        compiler_params=pltpu.CompilerParams(dimension_semantics=("arbitrary", "arbitrary"),
                                             vmem_limit_bytes=VMEM_LIMIT),
        name="combine_ln2",
    )(dest2d, y_rows, x1, mod, ln2g, ln2b)


def _band_matrices():
    rows = np.arange(CHUNK)[:, None]
    cols = np.arange(CHUNK + 2 * POOL_HALO)[None, :] - POOL_HALO
    return np.stack([((cols >= rows - w // 2) & (cols < rows + w // 2)) for w in POOL_WINDOWS]).astype(np.float32)


def _class_tables():
    ea, eb = [], []
    for g in range(N_GROUPS):
        for a in range(EPG):
            for b in range(a + 1, EPG):
                ea.append(g * EPG + a)
                eb.append(g * EPG + b)
    return np.asarray(ea, np.int32), np.asarray(eb, np.int32)


def _split_hi_lo(w):
    hi = w.astype(MXU_DTYPE)
    lo = (w - hi.astype(F32)).astype(MXU_DTYPE)
    return hi, lo


def kernel(x_prompt, x_sample, c_prompt, c_sample, w_ada, b_ada, w_in, v_norm_g, v_norm_b, w_spatial, b_spatial,
           w_pool, pool_scale, w_out, ln1_g, ln1_b, w_route_group, b_route_group, w_route_expert, b_route_expert,
           w_gate_up, w_down, ln2_g, ln2_b):
    assert w_ada.shape[0] == 1, "single-layer kernel"
    bp, sp, d = x_prompt.shape
    bs, ss, _ = x_sample.shape
    assert d == D_MODEL and sp % TS == 0 and ss % TS == 0
    n_prompt, n_sample = bp * sp, bs * ss
    n_tok = n_prompt + n_sample
    assert n_tok % BL == 0 and n_prompt % TD == 0

    c_all = jnp.concatenate([c_prompt, c_sample], axis=0)
    mod = _modulation(c_all, w_ada[0], b_ada).reshape(bp + bs, N_MOD, d)

    wr = jnp.concatenate([w_route_group[0], w_route_expert[0]], axis=1).T
    wr = jnp.pad(wr, ((0, ROUTE_PAD - N_ROUTE), (0, 0)))
    wr_hi, wr_lo = _split_hi_lo(wr)
    rbias = jnp.concatenate([b_route_group[0], b_route_expert[0].reshape(-1),
                             jnp.zeros((ROUTE_PAD - N_ROUTE,), F32)]).reshape(ROUTE_PAD, 1)
    row = lambda a: a.reshape(1, -1)
    weights = (w_in[0].astype(MXU_DTYPE), row(v_norm_g[0]), row(v_norm_b[0]), w_spatial[0].astype(MXU_DTYPE),
               b_spatial[0].T, jnp.asarray(_band_matrices(), MXU_DTYPE), w_pool[0].astype(MXU_DTYPE),
               row(pool_scale[0]), w_out[0].astype(MXU_DTYPE), row(ln1_g[0]), row(ln1_b[0]),
               jnp.concatenate([wr_hi, wr_lo], axis=0), rbias)

    x1, h2, rid, gates = _mixer_call(x_prompt, x_sample, mod, weights)

    nblk = (n_tok + N_CLASSES * (RB - 1)) // RB
    nb_pad = -(-nblk // 128) * 128
    u_mat = jnp.asarray(np.triu(np.ones((SUB, SUB), np.float32)), MXU_DTYPE)
    tri = jnp.asarray(np.tril(np.ones((CLS_PAD, CLS_PAD), np.float32)))
    dest, meta = _rank_call(rid, u_mat, tri, nb_pad)
    dest2d = dest.reshape(n_tok // TD, TD)
    tab_a, tab_b = _class_tables()
    blk_cls = meta[0, :nblk]
    blk_ea = jnp.asarray(tab_a)[blk_cls]
    blk_eb = jnp.asarray(tab_b)[blk_cls]
    nused = meta[1, 0:1]
    cum_cls = meta[2, :CLS_PAD]
    nblk_cls = meta[3, :CLS_PAD]

    blk_valid = meta[4, :nblk]
    sorted_rows, wgu, wd = _dispatch_call(cum_cls, nblk_cls, nused, blk_valid, dest2d, h2, gates, nblk * RB,
                                          w_gate_up[0], w_down[0])
    y_rows = _expert_call(blk_ea, blk_eb, nused, blk_valid, sorted_rows, wgu, wd)

    ln2g, ln2b = row(ln2_g[0]), row(ln2_b[0])
    y_prompt = _combine_call(dest2d, y_rows, x1, mod, 0, 0, ln2g, ln2b, bp, sp)
    y_sample = _combine_call(dest2d, y_rows, x1, mod, bp, n_prompt // TD, ln2g, ln2b, bs, ss)
    return (y_prompt, y_sample)
```

```python
import functools

import numpy as np
import jax
import jax.numpy as jnp
from jax import lax
from jax.experimental import pallas as pl
from jax.experimental.pallas import tpu as pltpu

F32 = jnp.float32
I32 = jnp.int32
MXU_DTYPE = jnp.bfloat16

D_MODEL = 1024
A_WIDTH = 512
B_WIDTH = 512
CHUNK = 128
A_HEADS = 4
HEAD_DIM = A_WIDTH // A_HEADS
POOL_WINDOWS = (2, 4, 8, 16)
POOL_HALO = 8
GROUP_DIM = B_WIDTH // len(POOL_WINDOWS)
N_GROUPS = 4
EPG = 8
N_EXPERTS = N_GROUPS * EPG
D_EXPERT = 512
N_MOD = 6
LN_EPS = 1e-5
ALPHA = 2.0 ** 0.25

PAIRS = EPG * (EPG - 1) // 2
N_CLASSES = N_GROUPS * PAIRS
CLS_PAD = 128
N_ROUTE = N_GROUPS + N_EXPERTS
ROUTE_PAD = 40
U32 = jnp.uint32
LANES = 128
SUBLANES = 8
PACK_ROWS = D_MODEL // (2 * LANES)
Y_ROWS = D_MODEL // LANES
assert PACK_ROWS < SUBLANES and Y_ROWS == SUBLANES
HI_MASK = 0xFFFF0000

TS = 512
TD = 1024
RB = 512
PACK_TILE = 32
assert LANES % PACK_TILE == 0 and PACK_TILE % 16 == 0
ZERO_ROWS = 128
ROW_UNROLL = 8
BL = 2048
SUB = 256
VMEM_LIMIT = 56 * 1024 * 1024

_NT = (((1,), (1,)), ((), ()))

_GELU_K1 = -2.0 * (2.0 / np.pi) ** 0.5
_GELU_K3 = _GELU_K1 * 0.044715


def _gelu_tanh(x):
    return x / (1.0 + jnp.exp(x * (_GELU_K1 + _GELU_K3 * (x * x))))


def _layer_norm(x, g, b):
    mu = jnp.mean(x, axis=-1, keepdims=True)
    xc = x - mu
    var = jnp.mean(xc * xc, axis=-1, keepdims=True)
    return xc * lax.rsqrt(var + LN_EPS) * g + b


def _mdot(a, b):
    return jnp.dot(a, b, preferred_element_type=F32)


def _mod_kernel(c_ref, w_ref, b_ref, o_ref):
    a = jax.nn.silu(c_ref[...])
    o_ref[...] = jnp.dot(a, w_ref[...], precision=lax.Precision.HIGHEST,
                         preferred_element_type=F32) + b_ref[...]


def _modulation(c_all, w_ada, b_ada):
    nb, d = c_all.shape
    n = w_ada.shape[1]
    bn = 1536
    return pl.pallas_call(
        _mod_kernel,
        out_shape=jax.ShapeDtypeStruct((nb, n), F32),
        grid=(n // bn,),
        in_specs=[pl.BlockSpec((nb, d), lambda j: (0, 0)),
                  pl.BlockSpec((d, bn), lambda j: (0, j)),
                  pl.BlockSpec((1, bn), lambda j: (0, j))],
        out_specs=pl.BlockSpec((nb, bn), lambda j: (0, j)),
        compiler_params=pltpu.CompilerParams(dimension_semantics=("arbitrary",),
                                             vmem_limit_bytes=VMEM_LIMIT),
        name="adaln_mod",
    )(c_all, w_ada, b_ada)


def _mixer_kernel(xp_ref, xpp_ref, xpn_ref, xs_ref, xsp_ref, xsn_ref, mod_ref, modp_ref, inv_ref, win_ref, vng_ref,
                  vnb_ref, ws_ref, bst_ref, band_ref, wpool_ref, pscale_ref, wout_ref, ln1g_ref, ln1b_ref, wr_ref,
                  rb_ref, x1_ref, h2_ref, rid_ref, gate_ref, res_ref, *, prompt_tiles, prompt_seq, sample_seq):
    ts = xp_ref.shape[1]
    i = pl.program_id(0)
    slot = i % 2

    @pl.when(i == 0)
    def _():
        res_ref[1] = jnp.zeros(res_ref.shape[1:], F32)

    is_p = i < prompt_tiles
    ns = jnp.where(is_p, prompt_seq // ts, sample_seq // ts)
    s = jnp.where(is_p, i, i - prompt_tiles) % ns
    md = mod_ref[0]
    sh1, sc1, g1 = md[0:1], md[1:2], md[2:3]

    xt = jnp.where(is_p, xp_ref[0], xs_ref[0])
    h = xt * (1.0 + sc1) + sh1
    hp = jnp.where(s > 0, jnp.where(is_p, xpp_ref[0], xsp_ref[0]) * (1.0 + sc1) + sh1, 0.0)
    hn = jnp.where(s < ns - 1, jnp.where(is_p, xpn_ref[0], xsn_ref[0]) * (1.0 + sc1) + sh1, 0.0)
    hext = jnp.concatenate([h, hp, hn], axis=0).astype(MXU_DTYPE)
    zext = _mdot(hext, win_ref[...])
    z = zext[:ts]

    za = _gelu_tanh(z[:, :2 * A_WIDTH])
    u = za[:, :A_WIDTH]
    v = _layer_norm(za[:, A_WIDTH:], vng_ref[...], vnb_ref[...]).astype(MXU_DTYPE)
    bst = bst_ref[...]
    n_chunks = ts // CHUNK

    def chunks_on_lanes(a, rows, col0, width):
        return jnp.concatenate([a[c * CHUNK:c * CHUNK + rows, col0:col0 + width] for c in range(n_chunks)], axis=1)

    def chunks_on_rows(a, width):
        return jnp.concatenate([a[:, c * width:(c + 1) * width] for c in range(n_chunks)], axis=0)

    heads = [_mdot(ws_ref[hh], chunks_on_lanes(v, CHUNK, hh * HEAD_DIM, HEAD_DIM)) + bst[:, hh:hh + 1]
             for hh in range(A_HEADS)]
    a_out = u * jnp.concatenate([chunks_on_rows(o, HEAD_DIM) for o in heads], axis=1)

    p = z[:, 2 * A_WIDTH:]
    pall = jnp.concatenate([zext[ts:ts + POOL_HALO, 2 * A_WIDTH:], p,
                            zext[ts + POOL_HALO:ts + 2 * POOL_HALO, 2 * A_WIDTH:]], axis=0)
    p_hi = pall.astype(MXU_DTYPE)
    p_lo = (pall - p_hi.astype(F32)).astype(MXU_DTYPE)
    b_cols = []
    for g in range(len(POOL_WINDOWS)):
        seg_rows = CHUNK + 2 * POOL_HALO
        win = (_mdot(band_ref[g], chunks_on_lanes(p_hi, seg_rows, g * GROUP_DIM, GROUP_DIM))
               + _mdot(band_ref[g], chunks_on_lanes(p_lo, seg_rows, g * GROUP_DIM, GROUP_DIM)))
        pooled = chunks_on_rows(win, GROUP_DIM) * inv_ref[0, g] - p[:, g * GROUP_DIM:(g + 1) * GROUP_DIM]
        b_cols.append(_mdot(pooled.astype(MXU_DTYPE), wpool_ref[g]))
    b_out = jnp.concatenate(b_cols, axis=1) * pscale_ref[...]

    mix_in = jnp.concatenate([a_out, b_out], axis=1).astype(MXU_DTYPE)
    mdp = modp_ref[0]
    h2_prev, lt_prev = _ln1_router(res_ref[1 - slot], mdp[3:4], mdp[4:5], ln1g_ref, ln1b_ref, wr_ref, rb_ref,
                                   x1_ref)
    _route(h2_prev, lt_prev, h2_ref, rid_ref, gate_ref)

    mix = _mdot(mix_in, wout_ref[...])
    res_ref[slot] = ALPHA * xt + g1 * mix


def _ln1_router(res, sh2, sc2, ln1g_ref, ln1b_ref, wr_ref, rb_ref, x1_ref):
    x1 = _layer_norm(res, ln1g_ref[...], ln1b_ref[...])
    x1_ref[...] = x1
    h2 = x1 * (1.0 + sc2) + sh2
    h2_hi = h2.astype(MXU_DTYPE)
    h2_lo = (h2 - h2_hi.astype(F32)).astype(MXU_DTYPE)
    l1 = lax.dot_general(wr_ref[...], h2_hi, _NT, preferred_element_type=F32)
    l2 = lax.dot_general(wr_ref[0:ROUTE_PAD], h2_lo, _NT, preferred_element_type=F32)
    return h2, l1[:ROUTE_PAD] + l1[ROUTE_PAD:] + l2 + rb_ref[...]


def _route(h2, lt, h2_ref, rid_ref, gate_ref):
    ts = h2.shape[0]

    def row(r):
        return lt[r:r + 1, :]

    gl = [row(r) for r in range(N_GROUPS)]
    gmax = jnp.maximum(jnp.maximum(gl[0], gl[1]), jnp.maximum(gl[2], gl[3]))
    gidx = jnp.where(gl[0] == gmax, 0, jnp.where(gl[1] == gmax, 1, jnp.where(gl[2] == gmax, 2, 3)))
    gsum = (jnp.exp(gl[0] - gmax) + jnp.exp(gl[1] - gmax)) + (jnp.exp(gl[2] - gmax) + jnp.exp(gl[3] - gmax))
    gw = 1.0 / gsum
    ev = [jnp.where(gidx == 0, row(N_GROUPS + j),
                    jnp.where(gidx == 1, row(N_GROUPS + EPG + j),
                              jnp.where(gidx == 2, row(N_GROUPS + 2 * EPG + j), row(N_GROUPS + 3 * EPG + j))))
          for j in range(EPG)]

    def top1(vals):
        m = vals[0]
        for t in vals[1:]:
            m = jnp.maximum(m, t)
        idx = jnp.full(m.shape, EPG - 1, I32)
        for j in range(EPG - 2, -1, -1):
            idx = jnp.where(vals[j] == m, j, idx)
        return m, idx

    v1, j1 = top1(ev)
    v2, j2 = top1([jnp.where(j1 == j, -jnp.inf, ev[j]) for j in range(EPG)])
    t2 = jnp.exp(v2 - v1)
    den = 1.0 + t2
    w1 = (1.0 / den) * gw
    w2 = (t2 / den) * gw
    first = j1 < j2
    ea = jnp.minimum(j1, j2)
    eb = jnp.maximum(j1, j2)
    wa = jnp.where(first, w1, w2)
    wb = jnp.where(first, w2, w1)
    cls = gidx * PAIRS + jnp.right_shift(ea * (2 * EPG - 1 - ea), 1) + (eb - ea - 1)

    r8 = lax.broadcasted_iota(I32, (8, ts), 0)
    rid_ref[...] = jnp.where(r8 == 0, cls, jnp.where(r8 == 1, gidx * EPG + ea,
                                                      jnp.where(r8 == 2, gidx * EPG + eb, 0)))
    gate_ref[...] = jnp.where(r8 == 0, wa, jnp.where(r8 == 1, wb, 0.0))
    h2_ref[...] = h2.astype(jnp.bfloat16)


def _gate_words(gates):
    n = gates.shape[1]
    g128 = jnp.concatenate([gates, jnp.zeros((LANES - gates.shape[0], n), F32)], axis=0)
    return lax.bitcast_convert_type(g128.T, U32)


def _pack_rows(h2, gate_words, rows_ref):
    n = h2.shape[0]
    bits = lax.bitcast_convert_type(h2.astype(F32), U32)
    half = D_MODEL // 2
    for k in range(PACK_ROWS):
        lo = jnp.right_shift(bits[:, k * LANES:(k + 1) * LANES], 16)
        hi = jnp.bitwise_and(bits[:, half + k * LANES:half + (k + 1) * LANES], jnp.uint32(HI_MASK))
        rows_ref[pl.ds(k, n, stride=SUBLANES), :] = jnp.bitwise_or(lo, hi)
    rows_ref[pl.ds(PACK_ROWS, n, stride=SUBLANES), :] = gate_words
    for k in range(PACK_ROWS + 1, SUBLANES):
        rows_ref[pl.ds(k, n, stride=SUBLANES), :] = jnp.zeros((n, LANES), U32)


def _inv_population_table():
    r = np.arange(TS)
    out = np.empty((3, len(POOL_WINDOWS), TS, LANES), np.float32)
    for case in range(3):
        for g, w in enumerate(POOL_WINDOWS):
            lo = np.maximum(r - w // 2, 0) if case == 0 else r - w // 2
            hi = np.minimum(r + w // 2, TS) if case == 2 else r + w // 2
            out[case, g] = (1.0 / (hi - lo).astype(np.float64)).astype(np.float32)[:, None]
    return out


def _mixer_call(x_prompt, x_sample, mod, weights):
    bp, sp, d = x_prompt.shape
    bs, ss, _ = x_sample.shape
    nsp, nss = sp // TS, ss // TS
    assert nsp >= 2 and nss >= 2, "a tile is the first or the last of its sequence, not both"
    ntp, nts = bp * nsp, bs * nss
    n_tok = bp * sp + bs * ss
    hb = TS // POOL_HALO

    def p_tile(i):
        t = jnp.minimum(i, ntp - 1)
        return t // nsp, t % nsp

    def s_tile(i):
        t = jnp.clip(i - ntp, 0, nts - 1)
        return t // nss, t % nss

    def specs(tile_fn, seq):
        def cur(i):
            b, s = tile_fn(i)
            return (b, s, 0)

        def prev(i):
            b, s = tile_fn(i)
            return (b, jnp.maximum(s * hb - 1, 0), 0)

        def nxt(i):
            b, s = tile_fn(i)
            return (b, jnp.minimum((s + 1) * hb, seq // POOL_HALO - 1), 0)

        return [pl.BlockSpec((1, TS, d), cur), pl.BlockSpec((1, POOL_HALO, d), prev),
                pl.BlockSpec((1, POOL_HALO, d), nxt)]

    def mod_map(i):
        return (jnp.where(i < ntp, p_tile(i)[0], bp + s_tile(i)[0]), 0, 0)

    def inv_map(i):
        s = jnp.where(i < ntp, p_tile(i)[1], s_tile(i)[1])
        last = jnp.where(i < ntp, nsp - 1, nss - 1)
        return (jnp.where(s == 0, 0, jnp.where(s == last, 2, 1)), 0, 0, 0)

    def const(w):
        return pl.BlockSpec(w.shape, lambda i, nd=w.ndim: (0,) * nd)

    def prev_tile(i):
        return jnp.maximum(i - 1, 0)

    inv_tab = jnp.asarray(_inv_population_table())
    in_specs = (specs(p_tile, sp) + specs(s_tile, ss)
                + [pl.BlockSpec((1, N_MOD, d), mod_map),
                   pl.BlockSpec((1, N_MOD, d), lambda i: mod_map(prev_tile(i))),
                   pl.BlockSpec((1,) + inv_tab.shape[1:], inv_map)]
                + [const(w) for w in weights])
    out_shape = (jax.ShapeDtypeStruct((n_tok, d), F32),
                 jax.ShapeDtypeStruct((n_tok, d), jnp.bfloat16),
                 jax.ShapeDtypeStruct((8, n_tok), I32),
                 jax.ShapeDtypeStruct((8, n_tok), F32))
    out_specs = (pl.BlockSpec((TS, d), lambda i: (prev_tile(i), 0)),
                 pl.BlockSpec((TS, d), lambda i: (prev_tile(i), 0)),
                 pl.BlockSpec((8, TS), lambda i: (0, prev_tile(i))),
                 pl.BlockSpec((8, TS), lambda i: (0, prev_tile(i))))
    return pl.pallas_call(
        functools.partial(_mixer_kernel, prompt_tiles=ntp, prompt_seq=sp, sample_seq=ss),
        out_shape=out_shape,
        grid=(ntp + nts + 1,),
        in_specs=in_specs,
        out_specs=out_specs,
        scratch_shapes=[pltpu.VMEM((2, TS, d), F32)],
        compiler_params=pltpu.CompilerParams(dimension_semantics=("arbitrary",),
                                             vmem_limit_bytes=VMEM_LIMIT),
        name="mixer_ln1_route",
    )(x_prompt, x_prompt, x_prompt, x_sample, x_sample, x_sample, mod, mod, inv_tab, *weights)


def _rank_kernel(rid_ref, u_ref, tri_ref, dest_ref, meta_ref, cnt_ref, base_ref, *, nb_pad):
    phase = pl.program_id(0)
    j = pl.program_id(1)
    cls_iota = lax.broadcasted_iota(I32, (CLS_PAD, SUB), 0)

    @pl.when((phase == 0) & (j == 0))
    def _():
        cnt_ref[...] = jnp.zeros_like(cnt_ref)

    @pl.when(phase == 0)
    def _():
        acc = cnt_ref[...]
        for sb in range(BL // SUB):
            ids = rid_ref[0:1, sb * SUB:(sb + 1) * SUB]
            acc = acc + jnp.sum((cls_iota == ids).astype(F32), axis=1, keepdims=True)
        cnt_ref[...] = acc

    @pl.when((phase == 1) & (j == 0))
    def _():
        cnt = jnp.broadcast_to(cnt_ref[...], (CLS_PAD, CLS_PAD))
        nblk = jnp.floor((cnt + (RB - 1)) * (1.0 / RB))
        cum = jnp.dot(tri_ref[...], nblk, precision=lax.Precision.HIGHEST, preferred_element_type=F32)
        base_ref[...] = cum[:, 0:1] * RB - cnt[:, 0:1]
        blk = lax.broadcasted_iota(I32, (CLS_PAD, nb_pad), 1).astype(F32)
        bcls = jnp.sum((cum[:, 0:1] <= blk).astype(F32), axis=0, keepdims=True)
        bcls = jnp.minimum(bcls, N_CLASSES - 1).astype(I32)
        nused = jnp.broadcast_to(cum[CLS_PAD - 1:CLS_PAD, 0:1], (1, nb_pad)).astype(I32)
        first = cum[:, 0:1] - nblk[:, 0:1]
        inside = (first <= blk) & (blk < cum[:, 0:1])
        fill = jnp.where(blk == first, cnt[:, 0:1] - (nblk[:, 0:1] - 1.0) * RB, float(RB))
        valid = jnp.sum(jnp.where(inside, fill, 0.0), axis=0, keepdims=True).astype(I32)
        cum_l = jnp.transpose(cum)[0:1].astype(I32)
        nblk_l = jnp.transpose(nblk)[0:1].astype(I32)
        pad = jnp.zeros((1, nb_pad - CLS_PAD), I32)
        r8 = lax.broadcasted_iota(I32, (8, nb_pad), 0)
        cum_row = jnp.concatenate([cum_l, pad], axis=1)
        nblk_row = jnp.concatenate([nblk_l, pad], axis=1)
        meta_ref[...] = jnp.where(r8 == 0, bcls, jnp.where(r8 == 1, nused,
                                  jnp.where(r8 == 2, cum_row, jnp.where(r8 == 3, nblk_row,
                                                                        jnp.where(r8 == 4, valid, 0)))))

    @pl.when(phase == 1)
    def _():
        base = base_ref[...]
        for sb in range(BL // SUB):
            ids = rid_ref[0:1, sb * SUB:(sb + 1) * SUB]
            hit = cls_iota == ids
            incl = _mdot(hit.astype(MXU_DTYPE), u_ref[...])
            slot = jnp.sum(jnp.where(hit, base + incl - 1.0, 0.0), axis=0, keepdims=True)
            dest_ref[0:1, sb * SUB:(sb + 1) * SUB] = slot.astype(I32)
            base = base + incl[:, SUB - 1:SUB]
        base_ref[...] = base


def _rank_call(rid, u_mat, tri, nb_pad):
    n_tok = rid.shape[1]
    nj = n_tok // BL
    return pl.pallas_call(
        functools.partial(_rank_kernel, nb_pad=nb_pad),
        out_shape=(jax.ShapeDtypeStruct((1, n_tok), I32), jax.ShapeDtypeStruct((8, nb_pad), I32)),
        grid=(2, nj),
        in_specs=[pl.BlockSpec((8, BL), lambda p, j: (0, j)),
                  pl.BlockSpec(u_mat.shape, lambda p, j: (0, 0)),
                  pl.BlockSpec(tri.shape, lambda p, j: (0, 0))],
        out_specs=(pl.BlockSpec((1, BL), lambda p, j: (0, p * j)),
                   pl.BlockSpec((8, nb_pad), lambda p, j: (0, 0))),
        scratch_shapes=[pltpu.VMEM((CLS_PAD, 1), F32), pltpu.VMEM((CLS_PAD, 1), F32)],
        compiler_params=pltpu.CompilerParams(dimension_semantics=("arbitrary", "arbitrary"),
                                             vmem_limit_bytes=VMEM_LIMIT),
        name="rank_tokens",
    )(rid, u_mat, tri)


def _dispatch_kernel(cum_ref, nblk_ref, nused_ref, valid_ref, dest_hbm, h2_ref, gate_ref, wgu_ref, wd_ref, out_hbm,
                     wgu_o_ref, wd_o_ref, idx_smem, rows_buf, zero_ref, sem_idx, sem_row, sem_zero, *, gu_steps):
    i = pl.program_id(0)
    n_steps = pl.num_programs(0)

    @pl.when(i < gu_steps)
    def _():
        wgu_o_ref[...] = wgu_ref[...].astype(MXU_DTYPE)

    @pl.when(i >= gu_steps)
    def _():
        wd_o_ref[...] = wd_ref[...].astype(MXU_DTYPE)

    slot = i % 2
    tile_rows = TD * SUBLANES
    part_rows = ZERO_ROWS * SUBLANES
    parts = RB // ZERO_ROWS
    n_blocks = out_hbm.shape[0] // (RB * SUBLANES)

    def idx_copy(step, s):
        return pltpu.make_async_copy(dest_hbm.at[step], idx_smem.at[s], sem_idx.at[s])

    def scattered(s):
        return pltpu.make_async_copy(rows_buf.at[s], out_hbm.at[pl.ds(0, tile_rows)], sem_row.at[s])

    def zero_part(part):
        start = pl.multiple_of(part * part_rows, part_rows)
        return pltpu.make_async_copy(zero_ref, out_hbm.at[pl.ds(start, part_rows)], sem_zero)

    @pl.when(i == 0)
    def _():
        idx_copy(0, 0).start()
        zero_ref[...] = jnp.zeros_like(zero_ref)

        def class_parts(c, fn):
            first = cum_ref[c] - nblk_ref[c]
            padding = RB - valid_ref[jnp.maximum(first, 0)]
            for q in range(parts):
                @pl.when((nblk_ref[c] > 0) & (q * ZERO_ROWS < padding))
                def _(q=q):
                    fn(zero_part(first * parts + q))

        def start(c, carry):
            class_parts(c, lambda cp: cp.start())
            return carry

        def wait(c, carry):
            class_parts(c, lambda cp: cp.wait())
            return carry

        def start_tail(part, carry):
            zero_part(part).start()
            return carry

        def wait_tail(part, carry):
            zero_part(part).wait()
            return carry

        lax.fori_loop(0, N_CLASSES, start, 0)
        lax.fori_loop(nused_ref[0] * parts, n_blocks * parts, start_tail, 0)
        lax.fori_loop(0, N_CLASSES, wait, 0)
        lax.fori_loop(nused_ref[0] * parts, n_blocks * parts, wait_tail, 0)

    @pl.when(i >= 2)
    def _():
        scattered(slot).wait()

    idx_copy(i, slot).wait()

    @pl.when(i + 1 < n_steps)
    def _():
        idx_copy(i + 1, 1 - slot).start()

    for j in range(TD // PACK_TILE):
        r0 = j * PACK_TILE
        if r0 % LANES == 0:
            gate_words = _gate_words(gate_ref[:, r0:r0 + LANES])
        g0 = r0 % LANES
        _pack_rows(h2_ref[r0:r0 + PACK_TILE, :], gate_words[g0:g0 + PACK_TILE],
                   rows_buf.at[slot, pl.ds(r0 * SUBLANES, PACK_TILE * SUBLANES)])

        def scatter(g, carry, r0=r0):
            for u in range(ROW_UNROLL):
                t = r0 + g * ROW_UNROLL + u
                src = pl.multiple_of(t * SUBLANES, SUBLANES)
                dst = pl.multiple_of(idx_smem[slot, t] * SUBLANES, SUBLANES)
                pltpu.make_async_copy(rows_buf.at[slot, pl.ds(src, SUBLANES)], out_hbm.at[pl.ds(dst, SUBLANES)],
                                      sem_row.at[slot]).start(priority=u % 2)
            return carry

        lax.fori_loop(0, PACK_TILE // ROW_UNROLL, scatter, 0)

    @pl.when(i + 1 == n_steps)
    def _():
        @pl.when(n_steps > 1)
        def _():
            scattered(1 - slot).wait()

        scattered(slot).wait()


def _dispatch_call(cum_cls, nblk_cls, nused, blk_valid, dest2d, h2, gates, n_rows, w_gate_up, w_down):
    n_tok, d = h2.shape
    n_steps = n_tok // TD
    wgu2d = w_gate_up.reshape(-1, w_gate_up.shape[-1])
    wd2d = w_down.reshape(-1, w_down.shape[-1])
    assert wgu2d.shape[1] == wd2d.shape[1]
    slab = max(16, pl.next_power_of_2(pl.cdiv(wgu2d.shape[0] + wd2d.shape[0], n_steps)))
    assert wgu2d.shape[0] % slab == 0 and wd2d.shape[0] % slab == 0
    gu_steps, d_steps = wgu2d.shape[0] // slab, wd2d.shape[0] // slab
    assert gu_steps + d_steps <= n_steps

    def gu_map(i, *_):
        return (jnp.minimum(i, gu_steps - 1), 0)

    def d_map(i, *_):
        return (jnp.clip(i - gu_steps, 0, d_steps - 1), 0)

    wspec = lambda m: pl.BlockSpec((slab, wgu2d.shape[1]), m)
    grid_spec = pltpu.PrefetchScalarGridSpec(
        num_scalar_prefetch=4,
        grid=(n_steps,),
        in_specs=[pl.BlockSpec(memory_space=pl.ANY), pl.BlockSpec((TD, d), lambda i, *_: (i, 0)),
                  pl.BlockSpec((gates.shape[0], TD), lambda i, *_: (0, i)), wspec(gu_map), wspec(d_map)],
        out_specs=(pl.BlockSpec(memory_space=pl.ANY), wspec(gu_map), wspec(d_map)),
        scratch_shapes=[pltpu.SMEM((2, TD), I32), pltpu.VMEM((2, TD * SUBLANES, LANES), U32),
                        pltpu.VMEM((ZERO_ROWS * SUBLANES, LANES), U32),
                        pltpu.SemaphoreType.DMA((2,)), pltpu.SemaphoreType.DMA((2,)), pltpu.SemaphoreType.DMA],
    )
    sorted_rows, wgu_c, wd_c = pl.pallas_call(
        functools.partial(_dispatch_kernel, gu_steps=gu_steps),
        out_shape=(jax.ShapeDtypeStruct((n_rows * SUBLANES, LANES), U32),
                   jax.ShapeDtypeStruct(wgu2d.shape, MXU_DTYPE), jax.ShapeDtypeStruct(wd2d.shape, MXU_DTYPE)),
        grid_spec=grid_spec,
        compiler_params=pltpu.CompilerParams(dimension_semantics=("arbitrary",),
                                             vmem_limit_bytes=VMEM_LIMIT),
        name="dispatch_rows",
    )(cum_cls, nblk_cls, nused, blk_valid, dest2d, h2, gates, wgu2d, wd2d)
    return sorted_rows, wgu_c.reshape(w_gate_up.shape), wd_c.reshape(w_down.shape)


def _expert_kernel(ea_ref, eb_ref, nused_ref, valid_ref, rows_ref, wgu_a_ref, wd_a_ref, wgu_b_ref, wd_b_ref, y_ref):
    b = pl.program_id(0)
    live = b < nused_ref[0]
    half_rows = RB // 2

    def evaluate(m):
        row0 = (RB - m) * SUBLANES

        def tile_row(k):
            return rows_ref[pl.ds(row0 + k, m, stride=SUBLANES), :]

        words = [tile_row(k) for k in range(PACK_ROWS)]
        lo = [lax.bitcast_convert_type(jnp.left_shift(w, 16), F32) for w in words]
        hi = [lax.bitcast_convert_type(jnp.bitwise_and(w, jnp.uint32(HI_MASK)), F32) for w in words]
        x = jnp.concatenate(lo + hi, axis=1).astype(MXU_DTYPE)
        gate = lax.bitcast_convert_type(tile_row(PACK_ROWS), F32)

        def expert(wgu_ref, wd_ref):
            gu = _mdot(x, wgu_ref[0])
            act = jax.nn.silu(gu[:, :D_EXPERT]) * gu[:, D_EXPERT:]
            return _mdot(act.astype(MXU_DTYPE), wd_ref[0])

        y = expert(wgu_a_ref, wd_a_ref) * gate[:, 0:1] + expert(wgu_b_ref, wd_b_ref) * gate[:, 1:2]
        for k in range(Y_ROWS):
            y_ref[pl.ds(row0 + k, m, stride=SUBLANES), :] = y[:, k * LANES:(k + 1) * LANES]

    @pl.when(live & (valid_ref[b] > half_rows))
    def _():
        evaluate(RB)

    @pl.when(live & (valid_ref[b] <= half_rows))
    def _():
        evaluate(half_rows)
        y_ref[:half_rows * Y_ROWS, :] = jnp.zeros((half_rows * Y_ROWS, LANES), F32)

    @pl.when(jnp.logical_not(live))
    def _():
        y_ref[...] = jnp.zeros_like(y_ref)


def _expert_call(blk_ea, blk_eb, nused, blk_valid, rows, w_gate_up, w_down):
    n_rows = rows.shape[0] // SUBLANES
    nblk = n_rows // RB

    def live(b, nu):
        return jnp.maximum(jnp.minimum(b, nu[0] - 1), 0)

    grid_spec = pltpu.PrefetchScalarGridSpec(
        num_scalar_prefetch=4,
        grid=(nblk,),
        in_specs=[pl.BlockSpec((RB * SUBLANES, LANES), lambda b, ea, eb, nu, nv: (live(b, nu), 0)),
                  pl.BlockSpec((1,) + w_gate_up.shape[1:], lambda b, ea, eb, nu, nv: (ea[live(b, nu)], 0, 0)),
                  pl.BlockSpec((1,) + w_down.shape[1:], lambda b, ea, eb, nu, nv: (ea[live(b, nu)], 0, 0)),
                  pl.BlockSpec((1,) + w_gate_up.shape[1:], lambda b, ea, eb, nu, nv: (eb[live(b, nu)], 0, 0)),
                  pl.BlockSpec((1,) + w_down.shape[1:], lambda b, ea, eb, nu, nv: (eb[live(b, nu)], 0, 0))],
        out_specs=pl.BlockSpec((RB * Y_ROWS, LANES), lambda b, ea, eb, nu, nv: (b, 0)),
    )
    return pl.pallas_call(
        _expert_kernel,
        out_shape=jax.ShapeDtypeStruct((n_rows * Y_ROWS, LANES), F32),
        grid_spec=grid_spec,
        compiler_params=pltpu.CompilerParams(dimension_semantics=("arbitrary",),
                                             vmem_limit_bytes=VMEM_LIMIT),
        name="pair_experts",
    )(blk_ea, blk_eb, nused, blk_valid, rows, w_gate_up, w_down, w_gate_up, w_down)


def _combine_kernel(dest_hbm, y_hbm, x1_ref, mod_ref, g_ref, b_ref, o_ref, idx_smem, ybuf, sem_idx, sem_row,
                    *, tile_off):
    ns = pl.num_programs(1)
    i = pl.program_id(0) * ns + pl.program_id(1)
    n_steps = pl.num_programs(0) * ns
    slot = i % 2

    def idx_copy(step, s):
        return pltpu.make_async_copy(dest_hbm.at[tile_off + step], idx_smem.at[s], sem_idx.at[s])

    def issue_gather(s):
        def gather(g, carry):
            for u in range(ROW_UNROLL):
                t = g * ROW_UNROLL + u
                src = pl.multiple_of(idx_smem[s, t] * Y_ROWS, Y_ROWS)
                dst = pl.multiple_of(t * Y_ROWS, Y_ROWS)
                pltpu.make_async_copy(y_hbm.at[pl.ds(src, Y_ROWS)], ybuf.at[s, pl.ds(dst, Y_ROWS)],
                                      sem_row.at[s]).start(priority=u % 2)
            return carry

        lax.fori_loop(0, TD // ROW_UNROLL, gather, 0)

    @pl.when(i == 0)
    def _():
        idx_copy(0, 0).start()
        idx_copy(0, 0).wait()
        issue_gather(0)

        @pl.when(n_steps > 1)
        def _():
            idx_copy(1, 1).start()

    @pl.when(i + 1 < n_steps)
    def _():
        idx_copy(i + 1, 1 - slot).wait()
        issue_gather(1 - slot)

    @pl.when(i + 2 < n_steps)
    def _():
        idx_copy(i + 2, slot).start()

    pltpu.make_async_copy(y_hbm.at[pl.ds(0, TD * Y_ROWS)], ybuf.at[slot], sem_row.at[slot]).wait()

    y = jnp.concatenate([ybuf[slot, pl.ds(k, TD, stride=Y_ROWS), :] for k in range(Y_ROWS)], axis=1)
    g2 = mod_ref[0][5:6]
    o_ref[0] = _layer_norm(ALPHA * x1_ref[...] + g2 * y, g_ref[...], b_ref[...])


def _combine_call(dest2d, y_rows, x1, mod, mod_off, tile_off, ln2g, ln2b, bsz, seq):
    d = x1.shape[1]
    ns = seq // TD
    return pl.pallas_call(
        functools.partial(_combine_kernel, tile_off=tile_off),
        out_shape=jax.ShapeDtypeStruct((bsz, seq, d), F32),
        grid=(bsz, ns),
        in_specs=[pl.BlockSpec(memory_space=pl.ANY),
                  pl.BlockSpec(memory_space=pl.ANY),
                  pl.BlockSpec((TD, d), lambda b, s: (tile_off + b * ns + s, 0)),
                  pl.BlockSpec((1, N_MOD, d), lambda b, s: (mod_off + b, 0, 0)),
                  pl.BlockSpec((1, d), lambda b, s: (0, 0)),
                  pl.BlockSpec((1, d), lambda b, s: (0, 0))],
        out_specs=pl.BlockSpec((1, TD, d), lambda b, s: (b, s, 0)),
        scratch_shapes=[pltpu.SMEM((2, TD), I32), pltpu.VMEM((2, TD * Y_ROWS, LANES), F32),
                        pltpu.SemaphoreType.DMA((2,)), pltpu.SemaphoreType.DMA((2,))],
        compiler_params=pltpu.CompilerParams(dimension_semantics=("arbitrary", "arbitrary"),
                                             vmem_limit_bytes=VMEM_LIMIT),
        name="combine_ln2",
    )(dest2d, y_rows, x1, mod, ln2g, ln2b)


def _band_matrices():
    rows = np.arange(CHUNK)[:, None]
    cols = np.arange(CHUNK + 2 * POOL_HALO)[None, :] - POOL_HALO
    return np.stack([((cols >= rows - w // 2) & (cols < rows + w // 2)) for w in POOL_WINDOWS]).astype(np.float32)


def _class_tables():
    ea, eb = [], []
    for g in range(N_GROUPS):
        for a in range(EPG):
            for b in range(a + 1, EPG):
                ea.append(g * EPG + a)
                eb.append(g * EPG + b)
    return np.asarray(ea, np.int32), np.asarray(eb, np.int32)


def _split_hi_lo(w):
    hi = w.astype(MXU_DTYPE)
    lo = (w - hi.astype(F32)).astype(MXU_DTYPE)
    return hi, lo


def kernel(x_prompt, x_sample, c_prompt, c_sample, w_ada, b_ada, w_in, v_norm_g, v_norm_b, w_spatial, b_spatial,
           w_pool, pool_scale, w_out, ln1_g, ln1_b, w_route_group, b_route_group, w_route_expert, b_route_expert,
           w_gate_up, w_down, ln2_g, ln2_b):
    assert w_ada.shape[0] == 1, "single-layer kernel"
    bp, sp, d = x_prompt.shape
    bs, ss, _ = x_sample.shape
    assert d == D_MODEL and sp % TS == 0 and ss % TS == 0
    n_prompt, n_sample = bp * sp, bs * ss
    n_tok = n_prompt + n_sample
    assert n_tok % BL == 0 and n_prompt % TD == 0

    c_all = jnp.concatenate([c_prompt, c_sample], axis=0)
    mod = _modulation(c_all, w_ada[0], b_ada).reshape(bp + bs, N_MOD, d)

    wr = jnp.concatenate([w_route_group[0], w_route_expert[0]], axis=1).T
    wr = jnp.pad(wr, ((0, ROUTE_PAD - N_ROUTE), (0, 0)))
    wr_hi, wr_lo = _split_hi_lo(wr)
    rbias = jnp.concatenate([b_route_group[0], b_route_expert[0].reshape(-1),
                             jnp.zeros((ROUTE_PAD - N_ROUTE,), F32)]).reshape(ROUTE_PAD, 1)
    row = lambda a: a.reshape(1, -1)
    weights = (w_in[0].astype(MXU_DTYPE), row(v_norm_g[0]), row(v_norm_b[0]), w_spatial[0].astype(MXU_DTYPE),
               b_spatial[0].T, jnp.asarray(_band_matrices(), MXU_DTYPE), w_pool[0].astype(MXU_DTYPE),
               row(pool_scale[0]), w_out[0].astype(MXU_DTYPE), row(ln1_g[0]), row(ln1_b[0]),
               jnp.concatenate([wr_hi, wr_lo], axis=0), rbias)

    x1, h2, rid, gates = _mixer_call(x_prompt, x_sample, mod, weights)

    nblk = (n_tok + N_CLASSES * (RB - 1)) // RB
    nb_pad = -(-nblk // 128) * 128
    u_mat = jnp.asarray(np.triu(np.ones((SUB, SUB), np.float32)), MXU_DTYPE)
    tri = jnp.asarray(np.tril(np.ones((CLS_PAD, CLS_PAD), np.float32)))
    dest, meta = _rank_call(rid, u_mat, tri, nb_pad)
    dest2d = dest.reshape(n_tok // TD, TD)
    tab_a, tab_b = _class_tables()
    blk_cls = meta[0, :nblk]
    blk_ea = jnp.asarray(tab_a)[blk_cls]
    blk_eb = jnp.asarray(tab_b)[blk_cls]
    nused = meta[1, 0:1]
    cum_cls = meta[2, :CLS_PAD]
    nblk_cls = meta[3, :CLS_PAD]

    blk_valid = meta[4, :nblk]
    sorted_rows, wgu, wd = _dispatch_call(cum_cls, nblk_cls, nused, blk_valid, dest2d, h2, gates, nblk * RB,
                                          w_gate_up[0], w_down[0])
    y_rows = _expert_call(blk_ea, blk_eb, nused, blk_valid, sorted_rows, wgu, wd)

    ln2g, ln2b = row(ln2_g[0]), row(ln2_b[0])
    y_prompt = _combine_call(dest2d, y_rows, x1, mod, 0, 0, ln2g, ln2b, bp, sp)
    y_sample = _combine_call(dest2d, y_rows, x1, mod, bp, n_prompt // TD, ln2g, ln2b, bs, ss)
    return (y_prompt, y_sample)
```

```python
import functools

import numpy as np
import jax
import jax.numpy as jnp
from jax import lax
from jax.experimental import pallas as pl
from jax.experimental.pallas import tpu as pltpu

F32 = jnp.float32
I32 = jnp.int32
MXU_DTYPE = jnp.bfloat16

D_MODEL = 1024
A_WIDTH = 512
B_WIDTH = 512
CHUNK = 128
A_HEADS = 4
HEAD_DIM = A_WIDTH // A_HEADS
POOL_WINDOWS = (2, 4, 8, 16)
POOL_HALO = 8
GROUP_DIM = B_WIDTH // len(POOL_WINDOWS)
N_GROUPS = 4
EPG = 8
N_EXPERTS = N_GROUPS * EPG
D_EXPERT = 512
N_MOD = 6
LN_EPS = 1e-5
ALPHA = 2.0 ** 0.25

PAIRS = EPG * (EPG - 1) // 2
N_CLASSES = N_GROUPS * PAIRS
CLS_PAD = 128
N_ROUTE = N_GROUPS + N_EXPERTS
ROUTE_PAD = 40
U32 = jnp.uint32
LANES = 128
SUBLANES = 8
PACK_ROWS = D_MODEL // (2 * LANES)
Y_ROWS = D_MODEL // LANES
assert PACK_ROWS < SUBLANES and Y_ROWS == SUBLANES
HI_MASK = 0xFFFF0000

TS = 512
TD = 1024
RB = 512
PACK_TILE = 32
assert LANES % PACK_TILE == 0 and PACK_TILE % 16 == 0
ZERO_ROWS = 128
ROW_UNROLL = 8
BL = 8192
SUB = 256
VMEM_LIMIT = 56 * 1024 * 1024

_NT = (((1,), (1,)), ((), ()))

_GELU_K1 = -2.0 * (2.0 / np.pi) ** 0.5
_GELU_K3 = _GELU_K1 * 0.044715


def _gelu_tanh(x):
    return x / (1.0 + jnp.exp(x * (_GELU_K1 + _GELU_K3 * (x * x))))


def _layer_norm(x, g, b):
    mu = jnp.mean(x, axis=-1, keepdims=True)
    xc = x - mu
    var = jnp.mean(xc * xc, axis=-1, keepdims=True)
    return xc * lax.rsqrt(var + LN_EPS) * g + b


def _mdot(a, b):
    return jnp.dot(a, b, preferred_element_type=F32)


def _mod_kernel(c_ref, w_ref, b_ref, o_ref):
    a = jax.nn.silu(c_ref[...])
    o_ref[...] = jnp.dot(a, w_ref[...], precision=lax.Precision.HIGHEST,
                         preferred_element_type=F32) + b_ref[...]


def _modulation(c_all, w_ada, b_ada):
    nb, d = c_all.shape
    n = w_ada.shape[1]
    bn = 3072
    return pl.pallas_call(
        _mod_kernel,
        out_shape=jax.ShapeDtypeStruct((nb, n), F32),
        grid=(n // bn,),
        in_specs=[pl.BlockSpec((nb, d), lambda j: (0, 0)),
                  pl.BlockSpec((d, bn), lambda j: (0, j)),
                  pl.BlockSpec((1, bn), lambda j: (0, j))],
        out_specs=pl.BlockSpec((nb, bn), lambda j: (0, j)),
        compiler_params=pltpu.CompilerParams(dimension_semantics=("arbitrary",),
                                             vmem_limit_bytes=VMEM_LIMIT),
        name="adaln_mod",
    )(c_all, w_ada, b_ada)


def _mixer_kernel(xp_ref, xpp_ref, xpn_ref, xs_ref, xsp_ref, xsn_ref, mod_ref, modp_ref, inv_ref, win_ref, vng_ref,
                  vnb_ref, ws_ref, bst_ref, band_ref, wpool_ref, pscale_ref, wout_ref, ln1g_ref, ln1b_ref, wr_ref,
                  rb_ref, x1_ref, h2_ref, rid_ref, gate_ref, res_ref, *, prompt_tiles, prompt_seq, sample_seq):
    ts = xp_ref.shape[1]
    i = pl.program_id(0)
    slot = i % 2

    @pl.when(i == 0)
    def _():
        res_ref[1] = jnp.zeros(res_ref.shape[1:], F32)

    is_p = i < prompt_tiles
    ns = jnp.where(is_p, prompt_seq // ts, sample_seq // ts)
    s = jnp.where(is_p, i, i - prompt_tiles) % ns
    md = mod_ref[0]
    sh1, sc1, g1 = md[0:1], md[1:2], md[2:3]

    xt = jnp.where(is_p, xp_ref[0], xs_ref[0])
    h = xt * (1.0 + sc1) + sh1
    hp = jnp.where(s > 0, jnp.where(is_p, xpp_ref[0], xsp_ref[0]) * (1.0 + sc1) + sh1, 0.0)
    hn = jnp.where(s < ns - 1, jnp.where(is_p, xpn_ref[0], xsn_ref[0]) * (1.0 + sc1) + sh1, 0.0)
    hext = jnp.concatenate([h, hp, hn], axis=0).astype(MXU_DTYPE)
    zext = _mdot(hext, win_ref[...])
    z = zext[:ts]

    za = _gelu_tanh(z[:, :2 * A_WIDTH])
    u = za[:, :A_WIDTH]
    v = _layer_norm(za[:, A_WIDTH:], vng_ref[...], vnb_ref[...]).astype(MXU_DTYPE)
    bst = bst_ref[...]
    n_chunks = ts // CHUNK

    def chunks_on_lanes(a, rows, col0, width):
        return jnp.concatenate([a[c * CHUNK:c * CHUNK + rows, col0:col0 + width] for c in range(n_chunks)], axis=1)

    def chunks_on_rows(a, width):
        return jnp.concatenate([a[:, c * width:(c + 1) * width] for c in range(n_chunks)], axis=0)

    heads = [_mdot(ws_ref[hh], chunks_on_lanes(v, CHUNK, hh * HEAD_DIM, HEAD_DIM)) + bst[:, hh:hh + 1]
             for hh in range(A_HEADS)]
    a_out = u * jnp.concatenate([chunks_on_rows(o, HEAD_DIM) for o in heads], axis=1)

    p = z[:, 2 * A_WIDTH:]
    pall = jnp.concatenate([zext[ts:ts + POOL_HALO, 2 * A_WIDTH:], p,
                            zext[ts + POOL_HALO:ts + 2 * POOL_HALO, 2 * A_WIDTH:]], axis=0)
    p_hi = pall.astype(MXU_DTYPE)
    p_lo = (pall - p_hi.astype(F32)).astype(MXU_DTYPE)
    b_cols = []
    for g in range(len(POOL_WINDOWS)):
        seg_rows = CHUNK + 2 * POOL_HALO
        win = (_mdot(band_ref[g], chunks_on_lanes(p_hi, seg_rows, g * GROUP_DIM, GROUP_DIM))
               + _mdot(band_ref[g], chunks_on_lanes(p_lo, seg_rows, g * GROUP_DIM, GROUP_DIM)))
        pooled = chunks_on_rows(win, GROUP_DIM) * inv_ref[0, g] - p[:, g * GROUP_DIM:(g + 1) * GROUP_DIM]
        b_cols.append(_mdot(pooled.astype(MXU_DTYPE), wpool_ref[g]))
    b_out = jnp.concatenate(b_cols, axis=1) * pscale_ref[...]

    mix_in = jnp.concatenate([a_out, b_out], axis=1).astype(MXU_DTYPE)
    mdp = modp_ref[0]
    h2_prev, lt_prev = _ln1_router(res_ref[1 - slot], mdp[3:4], mdp[4:5], ln1g_ref, ln1b_ref, wr_ref, rb_ref,
                                   x1_ref)
    _route(h2_prev, lt_prev, h2_ref, rid_ref, gate_ref)

    mix = _mdot(mix_in, wout_ref[...])
    res_ref[slot] = ALPHA * xt + g1 * mix


def _ln1_router(res, sh2, sc2, ln1g_ref, ln1b_ref, wr_ref, rb_ref, x1_ref):
    x1 = _layer_norm(res, ln1g_ref[...], ln1b_ref[...])
    x1_ref[...] = x1
    h2 = x1 * (1.0 + sc2) + sh2
    h2_hi = h2.astype(MXU_DTYPE)
    h2_lo = (h2 - h2_hi.astype(F32)).astype(MXU_DTYPE)
    l1 = lax.dot_general(wr_ref[...], h2_hi, _NT, preferred_element_type=F32)
    l2 = lax.dot_general(wr_ref[0:ROUTE_PAD], h2_lo, _NT, preferred_element_type=F32)
    return h2, l1[:ROUTE_PAD] + l1[ROUTE_PAD:] + l2 + rb_ref[...]


def _route(h2, lt, h2_ref, rid_ref, gate_ref):
    ts = h2.shape[0]

    def row(r):
        return lt[r:r + 1, :]

    gl = [row(r) for r in range(N_GROUPS)]
    gmax = jnp.maximum(jnp.maximum(gl[0], gl[1]), jnp.maximum(gl[2], gl[3]))
    gidx = jnp.where(gl[0] == gmax, 0, jnp.where(gl[1] == gmax, 1, jnp.where(gl[2] == gmax, 2, 3)))
    gsum = (jnp.exp(gl[0] - gmax) + jnp.exp(gl[1] - gmax)) + (jnp.exp(gl[2] - gmax) + jnp.exp(gl[3] - gmax))
    gw = 1.0 / gsum
    ev = [jnp.where(gidx == 0, row(N_GROUPS + j),
                    jnp.where(gidx == 1, row(N_GROUPS + EPG + j),
                              jnp.where(gidx == 2, row(N_GROUPS + 2 * EPG + j), row(N_GROUPS + 3 * EPG + j))))
          for j in range(EPG)]

    def top1(vals):
        m = vals[0]
        for t in vals[1:]:
            m = jnp.maximum(m, t)
        idx = jnp.full(m.shape, EPG - 1, I32)
        for j in range(EPG - 2, -1, -1):
            idx = jnp.where(vals[j] == m, j, idx)
        return m, idx

    v1, j1 = top1(ev)
    v2, j2 = top1([jnp.where(j1 == j, -jnp.inf, ev[j]) for j in range(EPG)])
    t2 = jnp.exp(v2 - v1)
    den = 1.0 + t2
    w1 = (1.0 / den) * gw
    w2 = (t2 / den) * gw
    first = j1 < j2
    ea = jnp.minimum(j1, j2)
    eb = jnp.maximum(j1, j2)
    wa = jnp.where(first, w1, w2)
    wb = jnp.where(first, w2, w1)
    cls = gidx * PAIRS + jnp.right_shift(ea * (2 * EPG - 1 - ea), 1) + (eb - ea - 1)

    r8 = lax.broadcasted_iota(I32, (8, ts), 0)
    rid_ref[...] = jnp.where(r8 == 0, cls, jnp.where(r8 == 1, gidx * EPG + ea,
                                                      jnp.where(r8 == 2, gidx * EPG + eb, 0)))
    gate_ref[...] = jnp.where(r8 == 0, wa, jnp.where(r8 == 1, wb, 0.0))
    h2_ref[...] = h2.astype(jnp.bfloat16)


def _gate_words(gates):
    n = gates.shape[1]
    g128 = jnp.concatenate([gates, jnp.zeros((LANES - gates.shape[0], n), F32)], axis=0)
    return lax.bitcast_convert_type(g128.T, U32)


def _pack_rows(h2, gate_words, rows_ref):
    n = h2.shape[0]
    bits = lax.bitcast_convert_type(h2.astype(F32), U32)
    half = D_MODEL // 2
    for k in range(PACK_ROWS):
        lo = jnp.right_shift(bits[:, k * LANES:(k + 1) * LANES], 16)
        hi = jnp.bitwise_and(bits[:, half + k * LANES:half + (k + 1) * LANES], jnp.uint32(HI_MASK))
        rows_ref[pl.ds(k, n, stride=SUBLANES), :] = jnp.bitwise_or(lo, hi)
    rows_ref[pl.ds(PACK_ROWS, n, stride=SUBLANES), :] = gate_words
    for k in range(PACK_ROWS + 1, SUBLANES):
        rows_ref[pl.ds(k, n, stride=SUBLANES), :] = jnp.zeros((n, LANES), U32)


def _inv_population_table():
    r = np.arange(TS)
    out = np.empty((3, len(POOL_WINDOWS), TS, LANES), np.float32)
    for case in range(3):
        for g, w in enumerate(POOL_WINDOWS):
            lo = np.maximum(r - w // 2, 0) if case == 0 else r - w // 2
            hi = np.minimum(r + w // 2, TS) if case == 2 else r + w // 2
            out[case, g] = (1.0 / (hi - lo).astype(np.float64)).astype(np.float32)[:, None]
    return out


def _mixer_call(x_prompt, x_sample, mod, weights):
    bp, sp, d = x_prompt.shape
    bs, ss, _ = x_sample.shape
    nsp, nss = sp // TS, ss // TS
    assert nsp >= 2 and nss >= 2, "a tile is the first or the last of its sequence, not both"
    ntp, nts = bp * nsp, bs * nss
    n_tok = bp * sp + bs * ss
    hb = TS // POOL_HALO

    def p_tile(i):
        t = jnp.minimum(i, ntp - 1)
        return t // nsp, t % nsp

    def s_tile(i):
        t = jnp.clip(i - ntp, 0, nts - 1)
        return t // nss, t % nss

    def specs(tile_fn, seq):
        def cur(i):
            b, s = tile_fn(i)
            return (b, s, 0)

        def prev(i):
            b, s = tile_fn(i)
            return (b, jnp.maximum(s * hb - 1, 0), 0)

        def nxt(i):
            b, s = tile_fn(i)
            return (b, jnp.minimum((s + 1) * hb, seq // POOL_HALO - 1), 0)

        return [pl.BlockSpec((1, TS, d), cur), pl.BlockSpec((1, POOL_HALO, d), prev),
                pl.BlockSpec((1, POOL_HALO, d), nxt)]

    def mod_map(i):
        return (jnp.where(i < ntp, p_tile(i)[0], bp + s_tile(i)[0]), 0, 0)

    def inv_map(i):
        s = jnp.where(i < ntp, p_tile(i)[1], s_tile(i)[1])
        last = jnp.where(i < ntp, nsp - 1, nss - 1)
        return (jnp.where(s == 0, 0, jnp.where(s == last, 2, 1)), 0, 0, 0)

    def const(w):
        return pl.BlockSpec(w.shape, lambda i, nd=w.ndim: (0,) * nd)

    def prev_tile(i):
        return jnp.maximum(i - 1, 0)

    inv_tab = jnp.asarray(_inv_population_table())
    in_specs = (specs(p_tile, sp) + specs(s_tile, ss)
                + [pl.BlockSpec((1, N_MOD, d), mod_map),
                   pl.BlockSpec((1, N_MOD, d), lambda i: mod_map(prev_tile(i))),
                   pl.BlockSpec((1,) + inv_tab.shape[1:], inv_map)]
                + [const(w) for w in weights])
    out_shape = (jax.ShapeDtypeStruct((n_tok, d), F32),
                 jax.ShapeDtypeStruct((n_tok, d), jnp.bfloat16),
                 jax.ShapeDtypeStruct((8, n_tok), I32),
                 jax.ShapeDtypeStruct((8, n_tok), F32))
    out_specs = (pl.BlockSpec((TS, d), lambda i: (prev_tile(i), 0)),
                 pl.BlockSpec((TS, d), lambda i: (prev_tile(i), 0)),
                 pl.BlockSpec((8, TS), lambda i: (0, prev_tile(i))),
                 pl.BlockSpec((8, TS), lambda i: (0, prev_tile(i))))
    return pl.pallas_call(
        functools.partial(_mixer_kernel, prompt_tiles=ntp, prompt_seq=sp, sample_seq=ss),
        out_shape=out_shape,
        grid=(ntp + nts + 1,),
        in_specs=in_specs,
        out_specs=out_specs,
        scratch_shapes=[pltpu.VMEM((2, TS, d), F32)],
        compiler_params=pltpu.CompilerParams(dimension_semantics=("arbitrary",),
                                             vmem_limit_bytes=VMEM_LIMIT),
        name="mixer_ln1_route",
    )(x_prompt, x_prompt, x_prompt, x_sample, x_sample, x_sample, mod, mod, inv_tab, *weights)


def _rank_kernel(rid_ref, u_ref, tri_ref, dest_ref, meta_ref, cnt_ref, base_ref, *, nb_pad):
    phase = pl.program_id(0)
    j = pl.program_id(1)
    cls_iota = lax.broadcasted_iota(I32, (CLS_PAD, SUB), 0)

    @pl.when((phase == 0) & (j == 0))
    def _():
        cnt_ref[...] = jnp.zeros_like(cnt_ref)

    @pl.when(phase == 0)
    def _():
        acc = cnt_ref[...]
        for sb in range(BL // SUB):
            ids = rid_ref[0:1, sb * SUB:(sb + 1) * SUB]
            acc = acc + jnp.sum((cls_iota == ids).astype(F32), axis=1, keepdims=True)
        cnt_ref[...] = acc

    @pl.when((phase == 1) & (j == 0))
    def _():
        cnt = jnp.broadcast_to(cnt_ref[...], (CLS_PAD, CLS_PAD))
        nblk = jnp.floor((cnt + (RB - 1)) * (1.0 / RB))
        cum = jnp.dot(tri_ref[...], nblk, precision=lax.Precision.HIGHEST, preferred_element_type=F32)
        base_ref[...] = cum[:, 0:1] * RB - cnt[:, 0:1]
        blk = lax.broadcasted_iota(I32, (CLS_PAD, nb_pad), 1).astype(F32)
        bcls = jnp.sum((cum[:, 0:1] <= blk).astype(F32), axis=0, keepdims=True)
        bcls = jnp.minimum(bcls, N_CLASSES - 1).astype(I32)
        nused = jnp.broadcast_to(cum[CLS_PAD - 1:CLS_PAD, 0:1], (1, nb_pad)).astype(I32)
        first = cum[:, 0:1] - nblk[:, 0:1]
        inside = (first <= blk) & (blk < cum[:, 0:1])
        fill = jnp.where(blk == first, cnt[:, 0:1] - (nblk[:, 0:1] - 1.0) * RB, float(RB))
        valid = jnp.sum(jnp.where(inside, fill, 0.0), axis=0, keepdims=True).astype(I32)
        cum_l = jnp.transpose(cum)[0:1].astype(I32)
        nblk_l = jnp.transpose(nblk)[0:1].astype(I32)
        pad = jnp.zeros((1, nb_pad - CLS_PAD), I32)
        r8 = lax.broadcasted_iota(I32, (8, nb_pad), 0)
        cum_row = jnp.concatenate([cum_l, pad], axis=1)
        nblk_row = jnp.concatenate([nblk_l, pad], axis=1)
        meta_ref[...] = jnp.where(r8 == 0, bcls, jnp.where(r8 == 1, nused,
                                  jnp.where(r8 == 2, cum_row, jnp.where(r8 == 3, nblk_row,
                                                                        jnp.where(r8 == 4, valid, 0)))))

    @pl.when(phase == 1)
    def _():
        base = base_ref[...]
        for sb in range(BL // SUB):
            ids = rid_ref[0:1, sb * SUB:(sb + 1) * SUB]
            hit = cls_iota == ids
            incl = _mdot(hit.astype(MXU_DTYPE), u_ref[...])
            slot = jnp.sum(jnp.where(hit, base + incl - 1.0, 0.0), axis=0, keepdims=True)
            dest_ref[0:1, sb * SUB:(sb + 1) * SUB] = slot.astype(I32)
            base = base + incl[:, SUB - 1:SUB]
        base_ref[...] = base


def _rank_call(rid, u_mat, tri, nb_pad):
    n_tok = rid.shape[1]
    nj = n_tok // BL
    return pl.pallas_call(
        functools.partial(_rank_kernel, nb_pad=nb_pad),
        out_shape=(jax.ShapeDtypeStruct((1, n_tok), I32), jax.ShapeDtypeStruct((8, nb_pad), I32)),
        grid=(2, nj),
        in_specs=[pl.BlockSpec((8, BL), lambda p, j: (0, j)),
                  pl.BlockSpec(u_mat.shape, lambda p, j: (0, 0)),
                  pl.BlockSpec(tri.shape, lambda p, j: (0, 0))],
        out_specs=(pl.BlockSpec((1, BL), lambda p, j: (0, p * j)),
                   pl.BlockSpec((8, nb_pad), lambda p, j: (0, 0))),
        scratch_shapes=[pltpu.VMEM((CLS_PAD, 1), F32), pltpu.VMEM((CLS_PAD, 1), F32)],
        compiler_params=pltpu.CompilerParams(dimension_semantics=("arbitrary", "arbitrary"),
                                             vmem_limit_bytes=VMEM_LIMIT),
        name="rank_tokens",
    )(rid, u_mat, tri)


def _dispatch_kernel(cum_ref, nblk_ref, nused_ref, valid_ref, dest_hbm, h2_ref, gate_ref, wgu_ref, wd_ref, out_hbm,
                     wgu_o_ref, wd_o_ref, idx_smem, rows_buf, zero_ref, sem_idx, sem_row, sem_zero, *, gu_steps):
    i = pl.program_id(0)
    n_steps = pl.num_programs(0)

    @pl.when(i < gu_steps)
    def _():
        wgu_o_ref[...] = wgu_ref[...].astype(MXU_DTYPE)

    @pl.when(i >= gu_steps)
    def _():
        wd_o_ref[...] = wd_ref[...].astype(MXU_DTYPE)

    slot = i % 2
    tile_rows = TD * SUBLANES
    part_rows = ZERO_ROWS * SUBLANES
    parts = RB // ZERO_ROWS
    n_blocks = out_hbm.shape[0] // (RB * SUBLANES)

    def idx_copy(step, s):
        return pltpu.make_async_copy(dest_hbm.at[step], idx_smem.at[s], sem_idx.at[s])

    def scattered(s):
        return pltpu.make_async_copy(rows_buf.at[s], out_hbm.at[pl.ds(0, tile_rows)], sem_row.at[s])

    def zero_part(part):
        start = pl.multiple_of(part * part_rows, part_rows)
        return pltpu.make_async_copy(zero_ref, out_hbm.at[pl.ds(start, part_rows)], sem_zero)

    @pl.when(i == 0)
    def _():
        idx_copy(0, 0).start()
        zero_ref[...] = jnp.zeros_like(zero_ref)

        def class_parts(c, fn):
            first = cum_ref[c] - nblk_ref[c]
            padding = RB - valid_ref[jnp.maximum(first, 0)]
            for q in range(parts):
                @pl.when((nblk_ref[c] > 0) & (q * ZERO_ROWS < padding))
                def _(q=q):
                    fn(zero_part(first * parts + q))

        def start(c, carry):
            class_parts(c, lambda cp: cp.start())
            return carry

        def wait(c, carry):
            class_parts(c, lambda cp: cp.wait())
            return carry

        def start_tail(part, carry):
            zero_part(part).start()
            return carry

        def wait_tail(part, carry):
            zero_part(part).wait()
            return carry

        lax.fori_loop(0, N_CLASSES, start, 0)
        lax.fori_loop(nused_ref[0] * parts, n_blocks * parts, start_tail, 0)
        lax.fori_loop(0, N_CLASSES, wait, 0)
        lax.fori_loop(nused_ref[0] * parts, n_blocks * parts, wait_tail, 0)

    @pl.when(i >= 2)
    def _():
        scattered(slot).wait()

    idx_copy(i, slot).wait()

    @pl.when(i + 1 < n_steps)
    def _():
        idx_copy(i + 1, 1 - slot).start()

    for j in range(TD // PACK_TILE):
        r0 = j * PACK_TILE
        if r0 % LANES == 0:
            gate_words = _gate_words(gate_ref[:, r0:r0 + LANES])
        g0 = r0 % LANES
        _pack_rows(h2_ref[r0:r0 + PACK_TILE, :], gate_words[g0:g0 + PACK_TILE],
                   rows_buf.at[slot, pl.ds(r0 * SUBLANES, PACK_TILE * SUBLANES)])

        def scatter(g, carry, r0=r0):
            for u in range(ROW_UNROLL):
                t = r0 + g * ROW_UNROLL + u
                src = pl.multiple_of(t * SUBLANES, SUBLANES)
                dst = pl.multiple_of(idx_smem[slot, t] * SUBLANES, SUBLANES)
                pltpu.make_async_copy(rows_buf.at[slot, pl.ds(src, SUBLANES)], out_hbm.at[pl.ds(dst, SUBLANES)],
                                      sem_row.at[slot]).start(priority=u % 2)
            return carry

        lax.fori_loop(0, PACK_TILE // ROW_UNROLL, scatter, 0)

    @pl.when(i + 1 == n_steps)
    def _():
        @pl.when(n_steps > 1)
        def _():
            scattered(1 - slot).wait()

        scattered(slot).wait()


def _dispatch_call(cum_cls, nblk_cls, nused, blk_valid, dest2d, h2, gates, n_rows, w_gate_up, w_down):
    n_tok, d = h2.shape
    n_steps = n_tok // TD
    wgu2d = w_gate_up.reshape(-1, w_gate_up.shape[-1])
    wd2d = w_down.reshape(-1, w_down.shape[-1])
    assert wgu2d.shape[1] == wd2d.shape[1]
    slab = max(16, pl.next_power_of_2(pl.cdiv(wgu2d.shape[0] + wd2d.shape[0], n_steps)))
    assert wgu2d.shape[0] % slab == 0 and wd2d.shape[0] % slab == 0
    gu_steps, d_steps = wgu2d.shape[0] // slab, wd2d.shape[0] // slab
    assert gu_steps + d_steps <= n_steps

    def gu_map(i, *_):
        return (jnp.minimum(i, gu_steps - 1), 0)

    def d_map(i, *_):
        return (jnp.clip(i - gu_steps, 0, d_steps - 1), 0)

    wspec = lambda m: pl.BlockSpec((slab, wgu2d.shape[1]), m)
    grid_spec = pltpu.PrefetchScalarGridSpec(
        num_scalar_prefetch=4,
        grid=(n_steps,),
        in_specs=[pl.BlockSpec(memory_space=pl.ANY), pl.BlockSpec((TD, d), lambda i, *_: (i, 0)),
                  pl.BlockSpec((gates.shape[0], TD), lambda i, *_: (0, i)), wspec(gu_map), wspec(d_map)],
        out_specs=(pl.BlockSpec(memory_space=pl.ANY), wspec(gu_map), wspec(d_map)),
        scratch_shapes=[pltpu.SMEM((2, TD), I32), pltpu.VMEM((2, TD * SUBLANES, LANES), U32),
                        pltpu.VMEM((ZERO_ROWS * SUBLANES, LANES), U32),
                        pltpu.SemaphoreType.DMA((2,)), pltpu.SemaphoreType.DMA((2,)), pltpu.SemaphoreType.DMA],
    )
    sorted_rows, wgu_c, wd_c = pl.pallas_call(
        functools.partial(_dispatch_kernel, gu_steps=gu_steps),
        out_shape=(jax.ShapeDtypeStruct((n_rows * SUBLANES, LANES), U32),
                   jax.ShapeDtypeStruct(wgu2d.shape, MXU_DTYPE), jax.ShapeDtypeStruct(wd2d.shape, MXU_DTYPE)),
        grid_spec=grid_spec,
        compiler_params=pltpu.CompilerParams(dimension_semantics=("arbitrary",),
                                             vmem_limit_bytes=VMEM_LIMIT),
        name="dispatch_rows",
    )(cum_cls, nblk_cls, nused, blk_valid, dest2d, h2, gates, wgu2d, wd2d)
    return sorted_rows, wgu_c.reshape(w_gate_up.shape), wd_c.reshape(w_down.shape)


def _expert_kernel(ea_ref, eb_ref, nused_ref, valid_ref, rows_ref, wgu_a_ref, wd_a_ref, wgu_b_ref, wd_b_ref, y_ref):
    b = pl.program_id(0)
    live = b < nused_ref[0]
    half_rows = RB // 2

    def evaluate(m):
        row0 = (RB - m) * SUBLANES

        def tile_row(k):
            return rows_ref[pl.ds(row0 + k, m, stride=SUBLANES), :]

        words = [tile_row(k) for k in range(PACK_ROWS)]
        lo = [lax.bitcast_convert_type(jnp.left_shift(w, 16), F32) for w in words]
        hi = [lax.bitcast_convert_type(jnp.bitwise_and(w, jnp.uint32(HI_MASK)), F32) for w in words]
        x = jnp.concatenate(lo + hi, axis=1).astype(MXU_DTYPE)
        gate = lax.bitcast_convert_type(tile_row(PACK_ROWS), F32)

        def swiglu(gu):
            return (jax.nn.silu(gu[:, :D_EXPERT]) * gu[:, D_EXPERT:]).astype(MXU_DTYPE)

        gu_a = _mdot(x, wgu_a_ref[0])
        gu_b = _mdot(x, wgu_b_ref[0])
        y = (_mdot(swiglu(gu_a), wd_a_ref[0]) * gate[:, 0:1]
             + _mdot(swiglu(gu_b), wd_b_ref[0]) * gate[:, 1:2])
        for k in range(Y_ROWS):
            y_ref[pl.ds(row0 + k, m, stride=SUBLANES), :] = y[:, k * LANES:(k + 1) * LANES]

    @pl.when(live & (valid_ref[b] > half_rows))
    def _():
        evaluate(RB)

    @pl.when(live & (valid_ref[b] <= half_rows))
    def _():
        evaluate(half_rows)
        y_ref[:half_rows * Y_ROWS, :] = jnp.zeros((half_rows * Y_ROWS, LANES), F32)

    @pl.when(jnp.logical_not(live))
    def _():
        y_ref[...] = jnp.zeros_like(y_ref)


def _expert_call(blk_ea, blk_eb, nused, blk_valid, rows, w_gate_up, w_down):
    n_rows = rows.shape[0] // SUBLANES
    nblk = n_rows // RB

    def live(b, nu):
        return jnp.maximum(jnp.minimum(b, nu[0] - 1), 0)

    grid_spec = pltpu.PrefetchScalarGridSpec(
        num_scalar_prefetch=4,
        grid=(nblk,),
        in_specs=[pl.BlockSpec((RB * SUBLANES, LANES), lambda b, ea, eb, nu, nv: (live(b, nu), 0)),
                  pl.BlockSpec((1,) + w_gate_up.shape[1:], lambda b, ea, eb, nu, nv: (ea[live(b, nu)], 0, 0)),
                  pl.BlockSpec((1,) + w_down.shape[1:], lambda b, ea, eb, nu, nv: (ea[live(b, nu)], 0, 0)),
                  pl.BlockSpec((1,) + w_gate_up.shape[1:], lambda b, ea, eb, nu, nv: (eb[live(b, nu)], 0, 0)),
                  pl.BlockSpec((1,) + w_down.shape[1:], lambda b, ea, eb, nu, nv: (eb[live(b, nu)], 0, 0))],
        out_specs=pl.BlockSpec((RB * Y_ROWS, LANES), lambda b, ea, eb, nu, nv: (b, 0)),
    )
    return pl.pallas_call(
        _expert_kernel,
        out_shape=jax.ShapeDtypeStruct((n_rows * Y_ROWS, LANES), F32),
        grid_spec=grid_spec,
        compiler_params=pltpu.CompilerParams(dimension_semantics=("arbitrary",),
                                             vmem_limit_bytes=VMEM_LIMIT),
        name="pair_experts",
    )(blk_ea, blk_eb, nused, blk_valid, rows, w_gate_up, w_down, w_gate_up, w_down)


def _combine_kernel(dest_hbm, y_hbm, x1_ref, mod_ref, g_ref, b_ref, o_ref, idx_smem, ybuf, sem_idx, sem_row,
                    *, tile_off):
    ns = pl.num_programs(1)
    i = pl.program_id(0) * ns + pl.program_id(1)
    n_steps = pl.num_programs(0) * ns
    slot = i % 2

    def idx_copy(step, s):
        return pltpu.make_async_copy(dest_hbm.at[tile_off + step], idx_smem.at[s], sem_idx.at[s])

    def issue_gather(s):
        def gather(g, carry):
            for u in range(ROW_UNROLL):
                t = g * ROW_UNROLL + u
                src = pl.multiple_of(idx_smem[s, t] * Y_ROWS, Y_ROWS)
                dst = pl.multiple_of(t * Y_ROWS, Y_ROWS)
                pltpu.make_async_copy(y_hbm.at[pl.ds(src, Y_ROWS)], ybuf.at[s, pl.ds(dst, Y_ROWS)],
                                      sem_row.at[s]).start(priority=u % 2)
            return carry

        lax.fori_loop(0, TD // ROW_UNROLL, gather, 0)

    @pl.when(i == 0)
    def _():
        idx_copy(0, 0).start()
        idx_copy(0, 0).wait()
        issue_gather(0)

        @pl.when(n_steps > 1)
        def _():
            idx_copy(1, 1).start()

    @pl.when(i + 1 < n_steps)
    def _():
        idx_copy(i + 1, 1 - slot).wait()
        issue_gather(1 - slot)

    @pl.when(i + 2 < n_steps)
    def _():
        idx_copy(i + 2, slot).start()

    pltpu.make_async_copy(y_hbm.at[pl.ds(0, TD * Y_ROWS)], ybuf.at[slot], sem_row.at[slot]).wait()

    y = jnp.concatenate([ybuf[slot, pl.ds(k, TD, stride=Y_ROWS), :] for k in range(Y_ROWS)], axis=1)
    g2 = mod_ref[0][5:6]
    o_ref[0] = _layer_norm(ALPHA * x1_ref[...] + g2 * y, g_ref[...], b_ref[...])


def _combine_call(dest2d, y_rows, x1, mod, mod_off, tile_off, ln2g, ln2b, bsz, seq):
    d = x1.shape[1]
    ns = seq // TD
    return pl.pallas_call(
        functools.partial(_combine_kernel, tile_off=tile_off),
        out_shape=jax.ShapeDtypeStruct((bsz, seq, d), F32),
        grid=(bsz, ns),
        in_specs=[pl.BlockSpec(memory_space=pl.ANY),
                  pl.BlockSpec(memory_space=pl.ANY),
                  pl.BlockSpec((TD, d), lambda b, s: (tile_off + b * ns + s, 0)),
                  pl.BlockSpec((1, N_MOD, d), lambda b, s: (mod_off + b, 0, 0)),
                  pl.BlockSpec((1, d), lambda b, s: (0, 0)),
                  pl.BlockSpec((1, d), lambda b, s: (0, 0))],
        out_specs=pl.BlockSpec((1, TD, d), lambda b, s: (b, s, 0)),
        scratch_shapes=[pltpu.SMEM((2, TD), I32), pltpu.VMEM((2, TD * Y_ROWS, LANES), F32),
                        pltpu.SemaphoreType.DMA((2,)), pltpu.SemaphoreType.DMA((2,))],
        compiler_params=pltpu.CompilerParams(dimension_semantics=("arbitrary", "arbitrary"),
                                             vmem_limit_bytes=VMEM_LIMIT),
        name="combine_ln2",
    )(dest2d, y_rows, x1, mod, ln2g, ln2b)


def _band_matrices():
    rows = np.arange(CHUNK)[:, None]
    cols = np.arange(CHUNK + 2 * POOL_HALO)[None, :] - POOL_HALO
    return np.stack([((cols >= rows - w // 2) & (cols < rows + w // 2)) for w in POOL_WINDOWS]).astype(np.float32)


def _class_tables():
    ea, eb = [], []
    for g in range(N_GROUPS):
        for a in range(EPG):
            for b in range(a + 1, EPG):
                ea.append(g * EPG + a)
                eb.append(g * EPG + b)
    return np.asarray(ea, np.int32), np.asarray(eb, np.int32)


def _split_hi_lo(w):
    hi = w.astype(MXU_DTYPE)
    lo = (w - hi.astype(F32)).astype(MXU_DTYPE)
    return hi, lo


def kernel(x_prompt, x_sample, c_prompt, c_sample, w_ada, b_ada, w_in, v_norm_g, v_norm_b, w_spatial, b_spatial,
           w_pool, pool_scale, w_out, ln1_g, ln1_b, w_route_group, b_route_group, w_route_expert, b_route_expert,
           w_gate_up, w_down, ln2_g, ln2_b):
    assert w_ada.shape[0] == 1, "single-layer kernel"
    bp, sp, d = x_prompt.shape
    bs, ss, _ = x_sample.shape
    assert d == D_MODEL and sp % TS == 0 and ss % TS == 0
    n_prompt, n_sample = bp * sp, bs * ss
    n_tok = n_prompt + n_sample
    assert n_tok % BL == 0 and n_prompt % TD == 0

    c_all = jnp.concatenate([c_prompt, c_sample], axis=0)
    mod = _modulation(c_all, w_ada[0], b_ada).reshape(bp + bs, N_MOD, d)

    wr = jnp.concatenate([w_route_group[0], w_route_expert[0]], axis=1).T
    wr = jnp.pad(wr, ((0, ROUTE_PAD - N_ROUTE), (0, 0)))
    wr_hi, wr_lo = _split_hi_lo(wr)
    rbias = jnp.concatenate([b_route_group[0], b_route_expert[0].reshape(-1),
                             jnp.zeros((ROUTE_PAD - N_ROUTE,), F32)]).reshape(ROUTE_PAD, 1)
    row = lambda a: a.reshape(1, -1)
    weights = (w_in[0].astype(MXU_DTYPE), row(v_norm_g[0]), row(v_norm_b[0]), w_spatial[0].astype(MXU_DTYPE),
               b_spatial[0].T, jnp.asarray(_band_matrices(), MXU_DTYPE), w_pool[0].astype(MXU_DTYPE),
               row(pool_scale[0]), w_out[0].astype(MXU_DTYPE), row(ln1_g[0]), row(ln1_b[0]),
               jnp.concatenate([wr_hi, wr_lo], axis=0), rbias)

    x1, h2, rid, gates = _mixer_call(x_prompt, x_sample, mod, weights)

    nblk = (n_tok + N_CLASSES * (RB - 1)) // RB
    nb_pad = -(-nblk // 128) * 128
    u_mat = jnp.asarray(np.triu(np.ones((SUB, SUB), np.float32)), MXU_DTYPE)
    tri = jnp.asarray(np.tril(np.ones((CLS_PAD, CLS_PAD), np.float32)))
    dest, meta = _rank_call(rid, u_mat, tri, nb_pad)
    dest2d = dest.reshape(n_tok // TD, TD)
    tab_a, tab_b = _class_tables()
    blk_cls = meta[0, :nblk]
    blk_ea = jnp.asarray(tab_a)[blk_cls]
    blk_eb = jnp.asarray(tab_b)[blk_cls]
    nused = meta[1, 0:1]
    cum_cls = meta[2, :CLS_PAD]
    nblk_cls = meta[3, :CLS_PAD]

    blk_valid = meta[4, :nblk]
    sorted_rows, wgu, wd = _dispatch_call(cum_cls, nblk_cls, nused, blk_valid, dest2d, h2, gates, nblk * RB,
                                          w_gate_up[0], w_down[0])
    y_rows = _expert_call(blk_ea, blk_eb, nused, blk_valid, sorted_rows, wgu, wd)

    ln2g, ln2b = row(ln2_g[0]), row(ln2_b[0])
    y_prompt = _combine_call(dest2d, y_rows, x1, mod, 0, 0, ln2g, ln2b, bp, sp)
    y_sample = _combine_call(dest2d, y_rows, x1, mod, bp, n_prompt // TD, ln2g, ln2b, bs, ss)
    return (y_prompt, y_sample)
```
